```python
import jax, jax.numpy as jnp
from jax import lax
import numpy as np


D_MODEL = 1024
BATCH = 16
SEQ = 4096
DEPTH = 2
DEC_BATCH = 16
DEC_SEQ = 64
PAST_LEN = 2048

CHUNK = 64
Q_BLOCK = 128
D_MIX = D_MODEL
SB_HEADS = 8
SB_HEAD_DIM = 64
SB_WIDTH = SB_HEADS * SB_HEAD_DIM
SB_SCALE = SB_HEAD_DIM ** -0.5
HG_EXPAND = 128
HG_WIDTH = D_MIX - SB_WIDTH
HG_HEADS = HG_WIDTH // HG_EXPAND
D_IN = 3 * SB_WIDTH + 4 * HG_WIDTH
SPLIT_POINTS = (SB_WIDTH, 2 * SB_WIDTH, 3 * SB_WIDTH,
                3 * SB_WIDTH + HG_WIDTH, 3 * SB_WIDTH + 2 * HG_WIDTH, 3 * SB_WIDTH + 3 * HG_WIDTH)
D_FF = 2816
N_EXPERTS = 8
TOP_K = 2
D_FF_EXPERT = 2816
N_DENSE = (DEPTH + 1) // 2
N_MOE = DEPTH // 2
EPS = 1e-6

kernel_name = 'hymba_stickbreak_hgrn2_streaming_step'


def rms_norm(x, g):
    xf = x.astype(jnp.float32)
    y = xf * lax.rsqrt(jnp.mean(xf * xf, axis=-1, keepdims=True) + EPS)
    return (y * g.astype(jnp.float32)).astype(x.dtype)


def stick_breaking(q, k, v, q_pos, k_pos):
    z = jnp.einsum('bqhd,bshd->bhqs', q, k).astype(jnp.float32) * SB_SCALE
    mask = k_pos[None, :] < q_pos[:, None]
    log_rem = jnp.where(mask, jax.nn.log_sigmoid(-z), 0.0)
    after = lax.cumsum(log_rem, axis=3, reverse=True) - log_rem
    w = jnp.where(mask, jnp.exp(jax.nn.log_sigmoid(z) + after), 0.0)
    return jnp.einsum('bhqs,bshd->bqhd', w.astype(v.dtype), v)


def stick_breaking_prompt(q, k, v):
    b, t = q.shape[0], q.shape[1]
    k_pos = jnp.arange(t)

    def one_block(i):
        qb = lax.dynamic_slice_in_dim(q, i * Q_BLOCK, Q_BLOCK, axis=1)
        return stick_breaking(qb, k, v, i * Q_BLOCK + jnp.arange(Q_BLOCK), k_pos)

    o = lax.map(one_block, jnp.arange(t // Q_BLOCK))
    return jnp.moveaxis(o, 0, 1).reshape(b, t, SB_HEADS, SB_HEAD_DIM)


def hgrn_chunk(s, q, k, lf, v):
    c = q.shape[1]
    bcum = jnp.cumsum(lf, axis=1)
    causal = jnp.tril(jnp.ones((c, c), dtype=bool))
    diff = bcum[:, :, None] - bcum[:, None, :]
    decay = jnp.exp(jnp.where(causal[None, :, :, None, None], diff, -jnp.inf))
    scores = jnp.einsum('bthd,bshd,btshd->bhts', q, k, decay)
    o = (jnp.einsum('bhts,bshv->bthv', scores, v)
         + jnp.einsum('bthd,bhdv->bthv', q * jnp.exp(bcum), s))
    b_last = bcum[:, -1]
    s_new = (jnp.exp(b_last)[..., None] * s
             + jnp.einsum('bshd,bshv->bhdv', k * jnp.exp(b_last[:, None] - bcum), v))
    return s_new, o


def hgrn_prompt(q, k, lf, v):
    b, t = q.shape[0], q.shape[1]
    nc = t // CHUNK

    def split(a):
        return jnp.moveaxis(a.reshape(b, nc, CHUNK, a.shape[2], a.shape[3]), 1, 0)

    s0 = jnp.zeros((b, HG_HEADS, HG_EXPAND, HG_EXPAND), jnp.float32)
    s_fin, o = lax.scan(lambda s, xs: hgrn_chunk(s, *xs), s0, (split(q), split(k), split(lf), split(v)))
    return jnp.moveaxis(o, 0, 1).reshape(b, t, HG_HEADS, HG_EXPAND), s_fin


def mix_inputs(h, l, w_in, sb_q_gain, sb_k_gain, lb):
    b, t = h.shape[0], h.shape[1]
    qa, ka, va, qb, fb, ib, gb = jnp.split(h @ w_in[l], SPLIT_POINTS, axis=-1)
    heads_a = lambda a: a.reshape(b, t, SB_HEADS, SB_HEAD_DIM)
    heads_b = lambda a: a.reshape(b, t, HG_HEADS, HG_EXPAND).astype(jnp.float32)
    qa = rms_norm(heads_a(qa), sb_q_gain[l])
    ka = rms_norm(heads_a(ka), sb_k_gain[l])
    va = heads_a(va)
    fb = heads_b(fb)
    lb_h = lb.reshape(HG_HEADS, HG_EXPAND)
    lf = jnp.logaddexp(jnp.log(lb_h), jnp.log1p(-lb_h) + jax.nn.log_sigmoid(fb))
    kb = (1.0 - lb_h) * jax.nn.sigmoid(-fb)
    qb = jax.nn.silu(heads_b(qb))
    return (qa, ka, va), (qb, kb, lf, heads_b(ib)), heads_b(gb)


def mix_output(o_a, o_b, g_b, l, sb_out_gain, hg_out_gain, w_out, dtype):
    b, t = o_a.shape[0], o_a.shape[1]
    ya = rms_norm(o_a, sb_out_gain[l]).astype(dtype).reshape(b, t, SB_WIDTH)
    yb = (rms_norm(o_b, hg_out_gain[l]) * jax.nn.silu(g_b)).astype(dtype).reshape(b, t, HG_WIDTH)
    return jnp.concatenate([ya, yb], axis=-1) @ w_out[l]


def swiglu(h, wg, wu, wd):
    return (jax.nn.silu(h @ wg) * (h @ wu)) @ wd


def moe_ffn(h, router, wg, wu, wd):
    logits = (h @ router).astype(jnp.float32)
    top_v, top_i = lax.top_k(logits, TOP_K)
    gates = jax.nn.softmax(top_v, axis=-1)
    combine = jnp.sum(jax.nn.one_hot(top_i, N_EXPERTS, dtype=jnp.float32) * gates[..., None], axis=-2)
    y = jnp.zeros_like(h)
    for e in range(N_EXPERTS):
        y = y + combine[..., e:e + 1].astype(h.dtype) * swiglu(h, wg[e], wu[e], wd[e])
    return y


def channel_mixer(h, l, ffn_w_gate, ffn_w_up, ffn_w_down, moe_router, moe_w_gate, moe_w_up, moe_w_down):
    j = l // 2
    if l % 2 == 0:
        return swiglu(h, ffn_w_gate[j], ffn_w_up[j], ffn_w_down[j])
    return moe_ffn(h, moe_router[j], moe_w_gate[j], moe_w_up[j], moe_w_down[j])


def setup_inputs(seed: int = 0) -> dict:
    key = jax.random.key(seed)
    ks = jax.random.split(key, 24)
    f32 = jnp.float32

    def nrm(k, shape, scale=1.0):
        return jax.random.normal(k, shape, f32) * scale

    def gain(k, shape):
        return 1.0 + 0.01 * nrm(k, shape)

    return {
        'x_prompt': nrm(ks[0], (BATCH, SEQ, D_MODEL)),
        'x_sample': nrm(ks[1], (DEC_BATCH, DEC_SEQ, D_MODEL)),
        'cache_k': nrm(ks[2], (DEPTH, DEC_BATCH, PAST_LEN, SB_HEADS, SB_HEAD_DIM)),
        'cache_v': nrm(ks[3], (DEPTH, DEC_BATCH, PAST_LEN, SB_HEADS, SB_HEAD_DIM)),
        'state_hgrn': nrm(ks[4], (DEPTH, DEC_BATCH, HG_HEADS, HG_EXPAND, HG_EXPAND), 0.5),
        'norm_mix': gain(ks[5], (DEPTH, D_MODEL)),
        'w_in': nrm(ks[6], (DEPTH, D_MODEL, D_IN), D_MODEL ** -0.5),
        'sb_q_gain': gain(ks[7], (DEPTH, SB_HEAD_DIM)),
        'sb_k_gain': gain(ks[8], (DEPTH, SB_HEAD_DIM)),
        'hg_lower_bounds': nrm(ks[9], (DEPTH, HG_WIDTH), 0.5),
        'sb_out_gain': gain(ks[10], (DEPTH, SB_HEAD_DIM)),
        'hg_out_gain': gain(ks[11], (DEPTH, HG_EXPAND)),
        'w_out': nrm(ks[12], (DEPTH, D_MIX, D_MODEL), D_MIX ** -0.5),
        'norm_ffn': gain(ks[13], (DEPTH, D_MODEL)),
        'ffn_w_gate': nrm(ks[14], (N_DENSE, D_MODEL, D_FF), D_MODEL ** -0.5),
        'ffn_w_up': nrm(ks[15], (N_DENSE, D_MODEL, D_FF), D_MODEL ** -0.5),
        'ffn_w_down': nrm(ks[16], (N_DENSE, D_FF, D_MODEL), D_FF ** -0.5),
        'moe_router': nrm(ks[17], (N_MOE, D_MODEL, N_EXPERTS), D_MODEL ** -0.5),
        'moe_w_gate': nrm(ks[18], (N_MOE, N_EXPERTS, D_MODEL, D_FF_EXPERT), D_MODEL ** -0.5),
        'moe_w_up': nrm(ks[19], (N_MOE, N_EXPERTS, D_MODEL, D_FF_EXPERT), D_MODEL ** -0.5),
        'moe_w_down': nrm(ks[20], (N_MOE, N_EXPERTS, D_FF_EXPERT, D_MODEL), D_FF_EXPERT ** -0.5),
    }


def reference(x_prompt, x_sample, cache_k, cache_v, state_hgrn, norm_mix, w_in, sb_q_gain, sb_k_gain,
              hg_lower_bounds, sb_out_gain, hg_out_gain, w_out, norm_ffn, ffn_w_gate, ffn_w_up, ffn_w_down,
              moe_router, moe_w_gate, moe_w_up, moe_w_down):
    lbs = jnp.cumsum(jax.nn.softmax(hg_lower_bounds.astype(jnp.float32), axis=0), axis=0)
    lbs = lbs - lbs[0:1]
    past = cache_k.shape[2]
    t_s = x_sample.shape[1]
    s_q_pos = past + jnp.arange(t_s)
    s_k_pos = jnp.arange(past + t_s)
    xp, xs = x_prompt, x_sample
    new_kp, new_vp, new_sp, new_ks, new_vs, new_ss = [], [], [], [], [], []
    for l in range(DEPTH):
        hp = rms_norm(xp, norm_mix[l])
        (qa, ka, va), (qb, kb, lf, ib), gb = mix_inputs(hp, l, w_in, sb_q_gain, sb_k_gain, lbs[l])
        o_a = stick_breaking_prompt(qa, ka, va)
        o_b, s_p = hgrn_prompt(qb, kb, lf, ib)
        xp = xp + mix_output(o_a, o_b, gb, l, sb_out_gain, hg_out_gain, w_out, xp.dtype)
        new_kp.append(ka)
        new_vp.append(va)
        new_sp.append(s_p.astype(x_prompt.dtype))
        hs = rms_norm(xs, norm_mix[l])
        (qa, ka, va), (qb, kb, lf, ib), gb = mix_inputs(hs, l, w_in, sb_q_gain, sb_k_gain, lbs[l])
        k_all = jnp.concatenate([cache_k[l].astype(ka.dtype), ka], axis=1)
        v_all = jnp.concatenate([cache_v[l].astype(va.dtype), va], axis=1)
        o_a = stick_breaking(qa, k_all, v_all, s_q_pos, s_k_pos)
        s_s, o_b = hgrn_chunk(state_hgrn[l].astype(jnp.float32), qb, kb, lf, ib)
        xs = xs + mix_output(o_a, o_b, gb, l, sb_out_gain, hg_out_gain, w_out, xs.dtype)
        new_ks.append(ka)
        new_vs.append(va)
        new_ss.append(s_s.astype(state_hgrn.dtype))
        xp = xp + channel_mixer(rms_norm(xp, norm_ffn[l]), l, ffn_w_gate, ffn_w_up, ffn_w_down,
                                moe_router, moe_w_gate, moe_w_up, moe_w_down)
        xs = xs + channel_mixer(rms_norm(xs, norm_ffn[l]), l, ffn_w_gate, ffn_w_up, ffn_w_down,
                                moe_router, moe_w_gate, moe_w_up, moe_w_down)
    return (xp, xs, jnp.stack(new_kp), jnp.stack(new_vp), jnp.stack(new_sp),
            jnp.stack(new_ks), jnp.stack(new_vs), jnp.stack(new_ss))
```

```python
import functools

import numpy as np
import jax
import jax.numpy as jnp
from jax import lax
from jax.experimental import pallas as pl
from jax.experimental.pallas import tpu as pltpu

EPS = 1e-6
TOP_K = 2
LANES = 128
F32 = jnp.float32
BF16 = jnp.bfloat16
VMEM_LIMIT_BYTES = 56 * 1024 * 1024


def _cparams(semantics):
    return pltpu.CompilerParams(dimension_semantics=semantics, vmem_limit_bytes=VMEM_LIMIT_BYTES)


def _dot(a, b):
    return jnp.dot(a, b, preferred_element_type=F32)


def _dot_nt(a, b):
    return lax.dot_general(a, b, (((1,), (1,)), ((), ())), preferred_element_type=F32)


def _dot_tn(a, b):
    return lax.dot_general(a, b, (((0,), (0,)), ((), ())), preferred_element_type=F32)


def _sigmoid(x):
    return 1.0 / (1.0 + jnp.exp(-x))


def _rms(x):
    return x * lax.rsqrt(jnp.mean(x * x, axis=-1, keepdims=True) + EPS)


def _tile(n, pref):
    t = min(n, pref)
    assert n % t == 0, (n, t)
    return t


def _inproj_kernel(x_ref, g_ref, w_ref, hm_ref, qg_ref, kg_ref, llb_ref, l1m_ref, oml_ref,
                   q_ref, kf_ref, kb_ref, vf_ref, vb_ref, qh_ref, kh_ref, lf_ref, ih_ref, gh_ref,
                   *, sw, hw):
    h = (_rms(x_ref[...]) * g_ref[...]).astype(BF16)

    def seg(lo, width):
        return _dot(h, w_ref[:, lo:lo + width])

    def headnorm(y, gain):
        m = _dot((y * y).astype(BF16), hm_ref[...])
        return (y * lax.rsqrt(m + EPS)) * gain

    q_ref[...] = headnorm(seg(0, sw), qg_ref[...]).astype(BF16)
    ka = headnorm(seg(sw, sw), kg_ref[...])
    kf_ref[...] = ka
    kb_ref[...] = ka.astype(BF16)
    va = seg(2 * sw, sw)
    vf_ref[...] = va
    vb_ref[...] = va.astype(BF16)

    o = 3 * sw
    qb = seg(o, hw)
    qh_ref[...] = qb * _sigmoid(qb)

    fb = seg(o + hw, hw)
    e = jnp.exp(-jnp.abs(fb))
    log_sig = jnp.minimum(fb, 0.0) - jnp.log(1.0 + e)
    c = l1m_ref[...] + log_sig
    a = llb_ref[...]
    lf_ref[...] = jnp.maximum(a, c) + jnp.log(1.0 + jnp.exp(-jnp.abs(a - c)))
    kh_ref[...] = oml_ref[...] * (jnp.where(fb >= 0.0, e, 1.0) / (1.0 + e))

    ih_ref[...] = seg(o + 2 * hw, hw)
    gh_ref[...] = seg(o + 3 * hw, hw)


def _inproj(x, g, w, hm, qg, kg, llb, l1m, oml, *, sw, hw):
    n, d = x.shape
    tm = _tile(n, 512)
    row = lambda i: (i, 0)
    const = lambda i: (0, 0)
    out_dtypes = [BF16, F32, BF16, F32, BF16, F32, F32, F32, F32, F32]
    widths = [sw] * 5 + [hw] * 5
    return pl.pallas_call(
        functools.partial(_inproj_kernel, sw=sw, hw=hw),
        grid=(n // tm,),
        in_specs=[pl.BlockSpec((tm, d), row), pl.BlockSpec((1, d), const),
                  pl.BlockSpec(w.shape, const), pl.BlockSpec(hm.shape, const),
                  pl.BlockSpec((1, sw), const), pl.BlockSpec((1, sw), const),
                  pl.BlockSpec((1, hw), const), pl.BlockSpec((1, hw), const), pl.BlockSpec((1, hw), const)],
        out_specs=[pl.BlockSpec((tm, wd), row) for wd in widths],
        out_shape=[jax.ShapeDtypeStruct((n, wd), dt) for wd, dt in zip(widths, out_dtypes)],
        compiler_params=_cparams(("parallel",)),
        name="inproj",
    )(x, g, w, hm, qg, kg, llb, l1m, oml)


def _attn_kernel(qi_tab, kn_tab, kp_tab, fl_tab,
                 q_ref, kn_ref, vn_ref, kp_ref, vp_ref, un_ref, up_ref, gain_ref,
                 y_ref, qm_ref, acc_ref, carry_ref, *, bq, bkn, hd):
    s = pl.program_id(2)
    flags = fl_tab[s]
    masked = (flags & 1) == 1
    first = (flags & 2) == 2
    last = (flags & 4) == 4
    lane = lax.broadcasted_iota(jnp.int32, (1, LANES), 1)
    heads = LANES // hd

    @pl.when(first)
    def _():
        q = q_ref[...]
        for hh in range(heads):
            in_head = (lane >= hh * hd) & (lane < (hh + 1) * hd)
            qm_ref[hh] = jnp.where(in_head, q, jnp.zeros_like(q))
        acc_ref[...] = jnp.zeros_like(acc_ref)
        carry_ref[...] = jnp.zeros_like(carry_ref)

    def sweep(k_ref, v_ref, u_ref, mask):
        k = k_ref[...].astype(BF16)
        v = v_ref[...].astype(BF16)
        u = u_ref[...]
        for hh in range(heads):
            z = _dot_nt(qm_ref[hh], k)
            sp = jnp.maximum(z, 0.0) + jnp.log(1.0 + jnp.exp(-jnp.abs(z)))
            if mask is not None:
                sp = jnp.where(mask, sp, 0.0)
            after = _dot(sp.astype(BF16), u)
            w = jnp.exp(z - sp - after - carry_ref[hh])
            if mask is not None:
                w = jnp.where(mask, w, 0.0)
            pv = _dot(w.astype(BF16), v)
            in_head = (lane >= hh * hd) & (lane < (hh + 1) * hd)
            acc_ref[...] += jnp.where(in_head, pv, 0.0)
            carry_ref[hh] += jnp.sum(sp, axis=-1, keepdims=True)

    @pl.when(masked)
    def _():
        rows = qi_tab[s] * bq + lax.broadcasted_iota(jnp.int32, (bq, bkn), 0)
        cols = kn_tab[s] * bkn + lax.broadcasted_iota(jnp.int32, (bq, bkn), 1)
        sweep(kn_ref, vn_ref, un_ref, cols < rows)

    @pl.when(jnp.logical_not(masked))
    def _():
        sweep(kp_ref, vp_ref, up_ref, None)

    @pl.when(last)
    def _():
        o = acc_ref[...]
        o2 = o * o
        ms = jnp.zeros_like(o)
        for hh in range(heads):
            in_head = (lane >= hh * hd) & (lane < (hh + 1) * hd)
            ssum = jnp.sum(jnp.where(in_head, o2, 0.0), axis=-1, keepdims=True)
            ms = jnp.where(in_head, ssum * (1.0 / hd), ms)
        y_ref[...] = ((o * lax.rsqrt(ms + EPS)) * gain_ref[...]).astype(y_ref.dtype)


def _strict_upper_ones(n):
    return jnp.asarray(np.tril(np.ones((n, n), np.float32), -1), BF16)


def _attention(q, k_new, v_new, k_past, v_past, gain, *, bq, bkn, bkp, past_len, hd):
    b, tq, w = q.shape
    nq = tq // bq
    new_per_q = bq // bkn
    qi_l, kn_l, kp_l, fl_l = [], [], [], []
    for qi in range(nq):
        n_past = (qi * bq if past_len is None else past_len) // bkp
        steps = [(1, kn) for kn in range((qi + 1) * new_per_q - 1, qi * new_per_q - 1, -1)]
        steps += [(0, kp) for kp in range(n_past - 1, -1, -1)]
        first_past = max(n_past - 1, 0)
        for idx, (m, blk) in enumerate(steps):
            qi_l.append(qi)
            kn_l.append(blk if m else qi * new_per_q)
            kp_l.append(first_past if m else blk)
            fl_l.append(m | (2 if idx == 0 else 0) | (4 if idx == len(steps) - 1 else 0))
    tabs = [jnp.asarray(np.asarray(t, np.int32)) for t in (qi_l, kn_l, kp_l, fl_l)]
    n_steps = len(qi_l)
    heads = LANES // hd

    qmap = lambda bb, p, s, qi, kn, kp, fl: (bb, qi[s], p)
    nmap = lambda bb, p, s, qi, kn, kp, fl: (bb, kn[s], p)
    pmap = lambda bb, p, s, qi, kn, kp, fl: (bb, kp[s], p)
    cmap = lambda bb, p, s, qi, kn, kp, fl: (0, 0)
    grid_spec = pltpu.PrefetchScalarGridSpec(
        num_scalar_prefetch=4,
        grid=(b, w // LANES, n_steps),
        in_specs=[pl.BlockSpec((None, bq, LANES), qmap),
                  pl.BlockSpec((None, bkn, LANES), nmap), pl.BlockSpec((None, bkn, LANES), nmap),
                  pl.BlockSpec((None, bkp, LANES), pmap), pl.BlockSpec((None, bkp, LANES), pmap),
                  pl.BlockSpec((bkn, bkn), cmap), pl.BlockSpec((bkp, bkp), cmap),
                  pl.BlockSpec((1, LANES), cmap)],
        out_specs=pl.BlockSpec((None, bq, LANES), qmap),
        scratch_shapes=[pltpu.VMEM((heads, bq, LANES), BF16), pltpu.VMEM((bq, LANES), F32),
                        pltpu.VMEM((heads, bq, 1), F32)],
    )
    return pl.pallas_call(
        functools.partial(_attn_kernel, bq=bq, bkn=bkn, hd=hd),
        grid_spec=grid_spec,
        out_shape=jax.ShapeDtypeStruct((b, tq, w), BF16),
        compiler_params=_cparams(("parallel", "parallel", "arbitrary")),
        name="stickbreak_attn",
    )(*tabs, q, k_new, v_new, k_past, v_past, _strict_upper_ones(bkn), _strict_upper_ones(bkp), gain)


def _hgrn_tables(c):
    t = np.arange(c)
    groups = [(t[None, :] <= t[:, None]), (t[None, :] > t[:, None])]
    masks = [np.eye(c, dtype=bool)]
    h = c // 2
    while h >= 1:
        blk = t // (2 * h)
        second = (t // h) % 2 == 1
        bound = blk * 2 * h + h
        j = t[None, :]
        wq = second[:, None] & (j >= bound[:, None]) & (j <= t[:, None])
        wk = (~second)[:, None] & (j > t[:, None]) & (j < bound[:, None])
        groups.append(wq | wk)
        masks.append((blk[:, None] == blk[None, :]) & second[:, None] & (~second)[None, :])
        h //= 2
    wst = np.concatenate(groups, axis=0).astype(np.float32)
    msk = np.stack(masks).astype(np.float32)
    return jnp.asarray(wst, BF16), jnp.asarray(msk)


def _hgrn_kernel(q_ref, k_ref, lf_ref, v_ref, g_ref, s0_ref, wst_ref, msk_ref, gain_ref,
                 y_ref, sout_ref, st_ref, *, c, n_chunks, n_levels):
    t = pl.program_id(2)

    @pl.when(t == 0)
    def _():
        st_ref[...] = s0_ref[...].T

    for ci in range(n_chunks):
        sl = slice(ci * c, (ci + 1) * c)
        q = q_ref[sl, :]
        k = k_ref[sl, :]
        lf = lf_ref[sl, :]
        v = v_ref[sl, :].astype(BF16)
        g = g_ref[sl, :]
        hi = lf.astype(BF16)
        r1 = lf - hi.astype(F32)
        mid = r1.astype(BF16)
        lo = (r1 - mid.astype(F32)).astype(BF16)
        wst = wst_ref[...]
        d = _dot(wst, hi) + _dot(wst, mid) + _dot(wst, lo)
        e = jnp.exp(d)
        st = st_ref[...]
        o = _dot_nt((q * e[0:c]).astype(BF16), st.astype(BF16))
        a = jnp.where(msk_ref[0] > 0.0, _dot_nt(q.astype(BF16), k.astype(BF16)), 0.0)
        for lv in range(n_levels):
            el = e[(2 + lv) * c:(3 + lv) * c]
            a += jnp.where(msk_ref[1 + lv] > 0.0, _dot_nt((q * el).astype(BF16), (k * el).astype(BF16)), 0.0)
        o += _dot(a.astype(BF16), v)
        k_end = (k * e[c:2 * c]).astype(BF16)
        st_ref[...] = st * e[c - 1:c] + _dot_tn(v, k_end)
        y = (_rms(o) * gain_ref[...]) * (g * _sigmoid(g))
        y_ref[sl, :] = y.astype(y_ref.dtype)

    @pl.when(t == pl.num_programs(2) - 1)
    def _():
        sout_ref[...] = st_ref[...].T


def _hgrn(q, k, lf, v, g, s0, gain, *, dk):
    b, t, hw = q.shape
    nh = hw // dk
    c = 64
    tt = _tile(t, 256)
    n_levels = int(np.log2(c))
    wst, msk = _hgrn_tables(c)
    tok = lambda bb, h, ti: (bb, ti, h)
    smap = lambda bb, h, ti: (bb, h, 0, 0)
    return pl.pallas_call(
        functools.partial(_hgrn_kernel, c=c, n_chunks=tt // c, n_levels=n_levels),
        grid=(b, nh, t // tt),
        in_specs=[pl.BlockSpec((None, tt, dk), tok)] * 5
        + [pl.BlockSpec((None, None, dk, dk), smap),
           pl.BlockSpec(wst.shape, lambda bb, h, ti: (0, 0)),
           pl.BlockSpec(msk.shape, lambda bb, h, ti: (0, 0, 0)),
           pl.BlockSpec((1, dk), lambda bb, h, ti: (0, 0))],
        out_specs=[pl.BlockSpec((None, tt, dk), tok), pl.BlockSpec((None, None, dk, dk), smap)],
        out_shape=[jax.ShapeDtypeStruct((b, t, hw), BF16), jax.ShapeDtypeStruct((b, nh, dk, dk), F32)],
        scratch_shapes=[pltpu.VMEM((dk, dk), F32)],
        compiler_params=_cparams(("parallel", "parallel", "arbitrary")),
        name="hgrn2",
    )(q, k, lf, v, g, s0, wst, msk, gain)


def _outproj_kernel(x_ref, ya_ref, yb_ref, wa_ref, wb_ref, o_ref):
    o_ref[...] = x_ref[...] + _dot(ya_ref[...], wa_ref[...]) + _dot(yb_ref[...], wb_ref[...])


def _outproj(x, ya, yb, wa, wb):
    n, d = x.shape
    tm = _tile(n, 512)
    row = lambda i: (i, 0)
    const = lambda i: (0, 0)
    return pl.pallas_call(
        _outproj_kernel,
        grid=(n // tm,),
        in_specs=[pl.BlockSpec((tm, d), row), pl.BlockSpec((tm, ya.shape[1]), row),
                  pl.BlockSpec((tm, yb.shape[1]), row), pl.BlockSpec(wa.shape, const),
                  pl.BlockSpec(wb.shape, const)],
        out_specs=pl.BlockSpec((tm, d), row),
        out_shape=jax.ShapeDtypeStruct((n, d), F32),
        compiler_params=_cparams(("parallel",)),
        name="outproj",
    )(x, ya, yb, wa, wb)


def _ffn_kernel(x_ref, g_ref, wg_ref, wu_ref, wd_ref, o_ref, h_ref, acc_ref):
    f = pl.program_id(1)

    @pl.when(f == 0)
    def _():
        h_ref[...] = (_rms(x_ref[...]) * g_ref[...]).astype(BF16)
        acc_ref[...] = jnp.zeros_like(acc_ref)

    h = h_ref[...]
    gate = _dot(h, wg_ref[...])
    up = _dot(h, wu_ref[...])
    acc_ref[...] += _dot(((gate * _sigmoid(gate)) * up).astype(BF16), wd_ref[...])

    @pl.when(f == pl.num_programs(1) - 1)
    def _():
        o_ref[...] = x_ref[...] + acc_ref[...]


def _ffn(x, g, wg, wu, wd):
    n, d = x.shape
    ff = wg.shape[1]
    tm = _tile(n, 1024)
    tf = _tile(ff, 256)
    return pl.pallas_call(
        _ffn_kernel,
        grid=(n // tm, ff // tf),
        in_specs=[pl.BlockSpec((tm, d), lambda i, f: (i, 0)), pl.BlockSpec((1, d), lambda i, f: (0, 0)),
                  pl.BlockSpec((d, tf), lambda i, f: (0, f)), pl.BlockSpec((d, tf), lambda i, f: (0, f)),
                  pl.BlockSpec((tf, d), lambda i, f: (f, 0))],
        out_specs=pl.BlockSpec((tm, d), lambda i, f: (i, 0)),
        out_shape=jax.ShapeDtypeStruct((n, d), F32),
        scratch_shapes=[pltpu.VMEM((tm, d), BF16), pltpu.VMEM((tm, d), F32)],
        compiler_params=_cparams(("parallel", "arbitrary")),
        name="ffn_dense",
    )(x, g, wg, wu, wd)


def _moe_kernel(x_ref, g_ref, rhi_ref, rlo_ref, wg_ref, wu_ref, wd_ref, o_ref,
                h_ref, acc_ref, comb_ref, ce_ref, *, n_experts):
    e = pl.program_id(1)
    f = pl.program_id(2)
    lane = lax.broadcasted_iota(jnp.int32, (1, LANES), 1)

    @pl.when((e == 0) & (f == 0))
    def _():
        hf = _rms(x_ref[...]) * g_ref[...]
        h_ref[...] = hf.astype(BF16)
        acc_ref[...] = jnp.zeros_like(acc_ref)
        h_hi = hf.astype(BF16)
        h_lo = (hf - h_hi.astype(F32)).astype(BF16)
        logits = _dot(h_hi, rhi_ref[...]) + _dot(h_lo, rhi_ref[...]) + _dot(h_hi, rlo_ref[...])
        logits = jnp.where(lane < n_experts, logits, -jnp.inf)
        m1 = jnp.max(logits, axis=-1, keepdims=True)
        i1 = jnp.min(jnp.where(logits == m1, lane, LANES), axis=-1, keepdims=True)
        rest = jnp.where(lane == i1, -jnp.inf, logits)
        m2 = jnp.max(rest, axis=-1, keepdims=True)
        i2 = jnp.min(jnp.where(rest == m2, lane, LANES), axis=-1, keepdims=True)
        e2 = jnp.exp(m2 - m1)
        g1 = 1.0 / (1.0 + e2)
        g2 = e2 / (1.0 + e2)
        comb_ref[...] = jnp.where(lane == i1, g1, 0.0) + jnp.where(lane == i2, g2, 0.0)

    @pl.when(f == 0)
    def _():
        ce_ref[...] = jnp.sum(jnp.where(lane == e, comb_ref[...], 0.0), axis=-1, keepdims=True)

    h = h_ref[...]
    gate = _dot(h, wg_ref[...])
    up = _dot(h, wu_ref[...])
    act = ((gate * _sigmoid(gate)) * up) * ce_ref[...]
    acc_ref[...] += _dot(act.astype(BF16), wd_ref[...])

    @pl.when((e == pl.num_programs(1) - 1) & (f == pl.num_programs(2) - 1))
    def _():
        o_ref[...] = x_ref[...] + acc_ref[...]


def _moe(x, g, router, wg, wu, wd):
    n, d = x.shape
    ne, _, ff = wg.shape
    tm = _tile(n, 1024)
    tf = _tile(ff, 256)
    rpad = jnp.zeros((d, LANES), F32).at[:, :ne].set(router)
    rhi = rpad.astype(BF16)
    rlo = (rpad - rhi.astype(F32)).astype(BF16)
    return pl.pallas_call(
        functools.partial(_moe_kernel, n_experts=ne),
        grid=(n // tm, ne, ff // tf),
        in_specs=[pl.BlockSpec((tm, d), lambda i, e, f: (i, 0)), pl.BlockSpec((1, d), lambda i, e, f: (0, 0)),
                  pl.BlockSpec((d, LANES), lambda i, e, f: (0, 0)), pl.BlockSpec((d, LANES), lambda i, e, f: (0, 0)),
                  pl.BlockSpec((None, d, tf), lambda i, e, f: (e, 0, f)),
                  pl.BlockSpec((None, d, tf), lambda i, e, f: (e, 0, f)),
                  pl.BlockSpec((None, tf, d), lambda i, e, f: (e, f, 0))],
        out_specs=pl.BlockSpec((tm, d), lambda i, e, f: (i, 0)),
        out_shape=jax.ShapeDtypeStruct((n, d), F32),
        scratch_shapes=[pltpu.VMEM((tm, d), BF16), pltpu.VMEM((tm, d), F32),
                        pltpu.VMEM((tm, LANES), F32), pltpu.VMEM((tm, 1), F32)],
        compiler_params=_cparams(("parallel", "arbitrary", "arbitrary")),
        name="moe_dense",
    )(x, g, rhi, rlo, wg, wu, wd)


def kernel(x_prompt, x_sample, cache_k, cache_v, state_hgrn, norm_mix, w_in, sb_q_gain, sb_k_gain,
           hg_lower_bounds, sb_out_gain, hg_out_gain, w_out, norm_ffn, ffn_w_gate, ffn_w_up, ffn_w_down,
           moe_router, moe_w_gate, moe_w_up, moe_w_down):
    depth = w_in.shape[0]
    bp, tp, d = x_prompt.shape
    bs, ts, _ = x_sample.shape
    past = cache_k.shape[2]
    sb_heads, hd = cache_k.shape[3], cache_k.shape[4]
    sw = sb_heads * hd
    dk = hg_out_gain.shape[1]
    hw = hg_lower_bounds.shape[1]
    hg_heads = hw // dk
    sb_scale = hd ** -0.5

    lbs = jnp.cumsum(jax.nn.softmax(hg_lower_bounds.astype(F32), axis=0), axis=0)
    lbs = lbs - lbs[0:1]
    head_mean = jnp.asarray(np.kron(np.eye(sb_heads), np.full((hd, hd), 1.0 / hd)), BF16)

    xp = x_prompt.reshape(bp * tp, d)
    xs = x_sample.reshape(bs * ts, d)
    zeros_state = jnp.zeros((bp, hg_heads, dk, dk), F32)
    outs = {k: [] for k in ("kp", "vp", "sp", "ks", "vs", "ss")}

    def mixer(x, b, t, l, *, k_past, v_past, past_len, s0, bq, bkn, bkp):
        qg = (jnp.tile(sb_q_gain[l], sb_heads) * sb_scale)[None, :]
        kg = jnp.tile(sb_k_gain[l], sb_heads)[None, :]
        q, kf, kb, vf, vb, qh, kh, lf, ih, gh = _inproj(
            x, norm_mix[l][None, :], w_in[l].astype(BF16), head_mean, qg, kg,
            jnp.log(lbs[l])[None, :], jnp.log1p(-lbs[l])[None, :], (1.0 - lbs[l])[None, :], sw=sw, hw=hw)
        r3 = lambda a: a.reshape(b, t, a.shape[-1])
        kb3, vb3 = r3(kb), r3(vb)
        ya = _attention(r3(q), kb3, vb3, kb3 if k_past is None else k_past, vb3 if v_past is None else v_past,
                        jnp.tile(sb_out_gain[l], LANES // hd)[None, :],
                        bq=bq, bkn=bkn, bkp=bkp, past_len=past_len, hd=hd)
        yb, s_fin = _hgrn(r3(qh), r3(kh), r3(lf), r3(ih), r3(gh), s0, hg_out_gain[l][None, :], dk=dk)
        wo = w_out[l].astype(BF16)
        x1 = _outproj(x, ya.reshape(b * t, sw), yb.reshape(b * t, hw), wo[:sw], wo[sw:])
        return x1, kf.reshape(b, t, sb_heads, hd), vf.reshape(b, t, sb_heads, hd), s_fin

    def channel(x, l):
        j = l // 2
        if l % 2 == 0:
            return _ffn(x, norm_ffn[l][None, :], ffn_w_gate[j].astype(BF16), ffn_w_up[j].astype(BF16),
                        ffn_w_down[j].astype(BF16))
        return _moe(x, norm_ffn[l][None, :], moe_router[j], moe_w_gate[j].astype(BF16),
                    moe_w_up[j].astype(BF16), moe_w_down[j].astype(BF16))

    bq_p = _tile(tp, 512)
    bk_p = _tile(bq_p, 256)
    for l in range(depth):
        xp, k4, v4, s_p = mixer(xp, bp, tp, l, k_past=None, v_past=None, past_len=None, s0=zeros_state,
                                bq=bq_p, bkn=bk_p, bkp=bk_p)
        outs["kp"].append(k4)
        outs["vp"].append(v4)
        outs["sp"].append(s_p)
        xs, k4, v4, s_s = mixer(xs, bs, ts, l,
                                k_past=cache_k[l].reshape(bs, past, sw), v_past=cache_v[l].reshape(bs, past, sw),
                                past_len=past, s0=state_hgrn[l].astype(F32), bq=ts, bkn=ts, bkp=_tile(past, 256))
        outs["ks"].append(k4)
        outs["vs"].append(v4)
        outs["ss"].append(s_s)
        xp = channel(xp, l)
        xs = channel(xs, l)

    return (xp.reshape(bp, tp, d), xs.reshape(bs, ts, d), jnp.stack(outs["kp"]), jnp.stack(outs["vp"]),
            jnp.stack(outs["sp"]), jnp.stack(outs["ks"]), jnp.stack(outs["vs"]), jnp.stack(outs["ss"]))
```

```python
import functools

import numpy as np
import jax
import jax.numpy as jnp
from jax import lax
from jax.experimental import pallas as pl
from jax.experimental.pallas import tpu as pltpu

EPS = 1e-6
TOP_K = 2
LANES = 128
F32 = jnp.float32
BF16 = jnp.bfloat16
VMEM_LIMIT_BYTES = 56 * 1024 * 1024


def _cparams(semantics):
    return pltpu.CompilerParams(dimension_semantics=semantics, vmem_limit_bytes=VMEM_LIMIT_BYTES)


def _dot(a, b):
    return jnp.dot(a, b, preferred_element_type=F32)


def _dot_nt(a, b):
    return lax.dot_general(a, b, (((1,), (1,)), ((), ())), preferred_element_type=F32)


def _dot_tn(a, b):
    return lax.dot_general(a, b, (((0,), (0,)), ((), ())), preferred_element_type=F32)


def _sigmoid(x):
    return 1.0 / (1.0 + jnp.exp(-x))


def _rms(x):
    return x * lax.rsqrt(jnp.mean(x * x, axis=-1, keepdims=True) + EPS)


def _tile(n, pref):
    t = min(n, pref)
    assert n % t == 0, (n, t)
    return t


def _inproj_kernel(x_ref, g_ref, w_ref, hm_ref, qg_ref, kg_ref, llb_ref, l1m_ref, oml_ref,
                   q_ref, kf_ref, kb_ref, vf_ref, vb_ref, qh_ref, kh_ref, lf_ref, ih_ref, gh_ref,
                   *, sw, hw):
    h = (_rms(x_ref[...]) * g_ref[...]).astype(BF16)

    def seg(lo, width):
        return _dot(h, w_ref[:, lo:lo + width])

    def headnorm(y, gain):
        m = _dot((y * y).astype(BF16), hm_ref[...])
        return (y * lax.rsqrt(m + EPS)) * gain

    q_ref[...] = headnorm(seg(0, sw), qg_ref[...]).astype(BF16)
    ka = headnorm(seg(sw, sw), kg_ref[...])
    kf_ref[...] = ka
    kb_ref[...] = ka.astype(BF16)
    va = seg(2 * sw, sw)
    vf_ref[...] = va
    vb_ref[...] = va.astype(BF16)

    o = 3 * sw
    qb = seg(o, hw)
    qh_ref[...] = qb * _sigmoid(qb)

    fb = seg(o + hw, hw)
    e = jnp.exp(-jnp.abs(fb))
    log_sig = jnp.minimum(fb, 0.0) - jnp.log(1.0 + e)
    c = l1m_ref[...] + log_sig
    a = llb_ref[...]
    lf_ref[...] = jnp.maximum(a, c) + jnp.log(1.0 + jnp.exp(-jnp.abs(a - c)))
    kh_ref[...] = oml_ref[...] * (jnp.where(fb >= 0.0, e, 1.0) / (1.0 + e))

    ih_ref[...] = seg(o + 2 * hw, hw)
    gh_ref[...] = seg(o + 3 * hw, hw)


def _inproj(x, g, w, hm, qg, kg, llb, l1m, oml, *, sw, hw):
    n, d = x.shape
    tm = _tile(n, 512)
    row = lambda i: (i, 0)
    const = lambda i: (0, 0)
    out_dtypes = [BF16, F32, BF16, F32, BF16, F32, F32, F32, F32, F32]
    widths = [sw] * 5 + [hw] * 5
    return pl.pallas_call(
        functools.partial(_inproj_kernel, sw=sw, hw=hw),
        grid=(n // tm,),
        in_specs=[pl.BlockSpec((tm, d), row), pl.BlockSpec((1, d), const),
                  pl.BlockSpec(w.shape, const), pl.BlockSpec(hm.shape, const),
                  pl.BlockSpec((1, sw), const), pl.BlockSpec((1, sw), const),
                  pl.BlockSpec((1, hw), const), pl.BlockSpec((1, hw), const), pl.BlockSpec((1, hw), const)],
        out_specs=[pl.BlockSpec((tm, wd), row) for wd in widths],
        out_shape=[jax.ShapeDtypeStruct((n, wd), dt) for wd, dt in zip(widths, out_dtypes)],
        compiler_params=_cparams(("parallel",)),
        name="inproj",
    )(x, g, w, hm, qg, kg, llb, l1m, oml)


KEY_SUB = 256
SOFTPLUS_CLAMP = 80.0


def _attn_kernel(qi_tab, kp_tab, fl_tab,
                 q_ref, kn_ref, vn_ref, kp_ref, vp_ref, un_ref, up_ref, gain_ref,
                 y_ref, qm_ref, acc_ref, carry_ref, *, bq, bkp, subn, subp, hd):
    s = pl.program_id(2)
    flags = fl_tab[s]
    masked = (flags & 1) == 1
    first = (flags & 2) == 2
    last = (flags & 4) == 4
    lane = lax.broadcasted_iota(jnp.int32, (1, LANES), 1)
    heads = LANES // hd

    @pl.when(first)
    def _():
        q = q_ref[...]
        for hh in range(heads):
            in_head = (lane >= hh * hd) & (lane < (hh + 1) * hd)
            qm_ref[hh] = jnp.where(in_head, q, jnp.zeros_like(q))
        acc_ref[...] = jnp.zeros_like(acc_ref)
        carry_ref[...] = jnp.zeros_like(carry_ref)

    def sweep(k_ref, v_ref, u_ref, bk, sub, diagonal):
        u = u_ref[...]
        for j in range(bk // sub - 1, -1, -1):
            r0 = j * sub if diagonal else 0
            k = k_ref[j * sub:(j + 1) * sub, :].astype(BF16)
            v = v_ref[j * sub:(j + 1) * sub, :].astype(BF16)
            if diagonal:
                rows = r0 + lax.broadcasted_iota(jnp.int32, (bq - r0, sub), 0)
                cols = j * sub + lax.broadcasted_iota(jnp.int32, (bq - r0, sub), 1)
                mask = cols < rows
            for hh in range(heads):
                z = _dot_nt(qm_ref[hh, r0:, :], k)
                sp = jnp.maximum(jnp.log(1.0 + jnp.exp(jnp.minimum(z, SOFTPLUS_CLAMP))), z)
                if diagonal:
                    sp = jnp.where(mask, sp, 0.0)
                after = _dot(sp.astype(BF16), u)
                w = jnp.exp((z - sp) - after)
                if diagonal:
                    w = jnp.where(mask, w, 0.0)
                pv = _dot(w.astype(BF16), v)
                c = carry_ref[hh, r0:, :]
                acc_ref[hh, r0:, :] += jnp.exp(-c) * pv
                carry_ref[hh, r0:, :] = c + jnp.sum(sp, axis=-1, keepdims=True)

    @pl.when(masked)
    def _():
        sweep(kn_ref, vn_ref, un_ref, bq, subn, True)

    @pl.when(jnp.logical_not(masked))
    def _():
        sweep(kp_ref, vp_ref, up_ref, bkp, subp, False)

    @pl.when(last)
    def _():
        o = acc_ref[0]
        for hh in range(1, heads):
            in_head = (lane >= hh * hd) & (lane < (hh + 1) * hd)
            o = jnp.where(in_head, acc_ref[hh], o)
        o2 = o * o
        ms = jnp.zeros_like(o)
        for hh in range(heads):
            in_head = (lane >= hh * hd) & (lane < (hh + 1) * hd)
            ssum = jnp.sum(jnp.where(in_head, o2, 0.0), axis=-1, keepdims=True)
            ms = jnp.where(in_head, ssum * (1.0 / hd), ms)
        y_ref[...] = ((o * lax.rsqrt(ms + EPS)) * gain_ref[...]).astype(y_ref.dtype)


def _tril_strict(n):
    return jnp.asarray(np.tril(np.ones((n, n), np.float32), -1), BF16)


def _attention(q, k_new, v_new, k_past, v_past, gain, *, bq, bkp, past_len, hd):
    b, tq, w = q.shape
    nq = tq // bq
    subn = min(bq, KEY_SUB)
    subp = min(bkp, KEY_SUB)
    assert bq % subn == 0 and bkp % subp == 0
    qi_l, kp_l, fl_l = [], [], []
    for qi in range(nq):
        n_past = (qi * bq if past_len is None else past_len) // bkp
        blocks = [None] + list(range(n_past - 1, -1, -1))
        for idx, blk in enumerate(blocks):
            qi_l.append(qi)
            kp_l.append(max(n_past - 1, 0) if blk is None else blk)
            fl_l.append((1 if blk is None else 0) | (2 if idx == 0 else 0) | (4 if idx == len(blocks) - 1 else 0))
    tabs = [jnp.asarray(np.asarray(t, np.int32)) for t in (qi_l, kp_l, fl_l)]
    heads = LANES // hd

    qmap = lambda bb, p, s, qi, kp, fl: (bb, qi[s], p)
    pmap = lambda bb, p, s, qi, kp, fl: (bb, kp[s], p)
    cmap = lambda bb, p, s, qi, kp, fl: (0, 0)
    grid_spec = pltpu.PrefetchScalarGridSpec(
        num_scalar_prefetch=3,
        grid=(b, w // LANES, len(qi_l)),
        in_specs=[pl.BlockSpec((None, bq, LANES), qmap),
                  pl.BlockSpec((None, bq, LANES), qmap), pl.BlockSpec((None, bq, LANES), qmap),
                  pl.BlockSpec((None, bkp, LANES), pmap), pl.BlockSpec((None, bkp, LANES), pmap),
                  pl.BlockSpec((subn, subn), cmap), pl.BlockSpec((subp, subp), cmap),
                  pl.BlockSpec((1, LANES), cmap)],
        out_specs=pl.BlockSpec((None, bq, LANES), qmap),
        scratch_shapes=[pltpu.VMEM((heads, bq, LANES), BF16), pltpu.VMEM((heads, bq, LANES), F32),
                        pltpu.VMEM((heads, bq, LANES), F32)],
    )
    return pl.pallas_call(
        functools.partial(_attn_kernel, bq=bq, bkp=bkp, subn=subn, subp=subp, hd=hd),
        grid_spec=grid_spec,
        out_shape=jax.ShapeDtypeStruct((b, tq, w), BF16),
        compiler_params=_cparams(("parallel", "parallel", "arbitrary")),
        name="stickbreak_attn",
    )(*tabs, q, k_new, v_new, k_past, v_past, _tril_strict(subn), _tril_strict(subp), gain)


def _hgrn_halvings(c):
    return [c >> (i + 1) for i in range(int(np.log2(c)))]


def _hgrn_masks(c):
    t = np.arange(c)
    masks = [np.eye(c, dtype=bool)]
    for h in _hgrn_halvings(c):
        blk = t // (2 * h)
        second = (t // h) % 2 == 1
        masks.append((blk[:, None] == blk[None, :]) & second[:, None] & (~second)[None, :])
    return jnp.asarray(np.stack(masks).astype(np.float32))


def _boundary_rows(b, h, row):
    c, dk = b.shape
    if 2 * h >= 8:
        n = c // (2 * h)
        ref = b.reshape(n, 2 * h, dk)[:, h - 1:h, :]
        return jnp.broadcast_to(ref, (n, 2 * h, dk)).reshape(c, dk)
    down1 = pltpu.roll(b, 1, 0)
    if h == 1:
        return jnp.where((row & 1) == 1, down1, b)
    m = row & 3
    up1 = pltpu.roll(b, c - 1, 0)
    down2 = pltpu.roll(b, 2, 0)
    return jnp.where(m == 0, up1, jnp.where(m == 1, b, jnp.where(m == 2, down1, down2)))


def _hgrn_kernel(q_ref, k_ref, lf_ref, v_ref, g_ref, s0_ref, tri_ref, msk_ref, gain_ref,
                 y_ref, sout_ref, st_ref, *, c, n_chunks):
    t = pl.program_id(2)
    dk = q_ref.shape[-1]

    @pl.when(t == 0)
    def _():
        st_ref[...] = s0_ref[...].T

    row = lax.broadcasted_iota(jnp.int32, (c, dk), 0)
    tri = tri_ref[...]
    for ci in range(n_chunks):
        sl = slice(ci * c, (ci + 1) * c)
        q = q_ref[sl, :]
        k = k_ref[sl, :]
        lf = lf_ref[sl, :]
        v = v_ref[sl, :].astype(BF16)
        g = g_ref[sl, :]
        hi = lf.astype(BF16)
        r1 = lf - hi.astype(F32)
        mid = r1.astype(BF16)
        lo = (r1 - mid.astype(F32)).astype(BF16)
        b = _dot(tri, hi) + _dot(tri, mid) + _dot(tri, lo)
        st = st_ref[...]
        o = _dot_nt((q * jnp.exp(b)).astype(BF16), st.astype(BF16))
        a = jnp.where(msk_ref[0] > 0.0, _dot_nt(q.astype(BF16), k.astype(BF16)), 0.0)
        for lv, h in enumerate(_hgrn_halvings(c)):
            gap = b - _boundary_rows(b, h, row)
            el = jnp.exp(jnp.where((row & h) != 0, gap, -gap))
            a += jnp.where(msk_ref[1 + lv] > 0.0, _dot_nt((q * el).astype(BF16), (k * el).astype(BF16)), 0.0)
        o += _dot(a.astype(BF16), v)
        b_last = b[c - 1:c, :]
        k_end = (k * jnp.exp(b_last - b)).astype(BF16)
        st_ref[...] = st * jnp.exp(b_last) + _dot_tn(v, k_end)
        y = (_rms(o) * gain_ref[...]) * (g * _sigmoid(g))
        y_ref[sl, :] = y.astype(y_ref.dtype)

    @pl.when(t == pl.num_programs(2) - 1)
    def _():
        sout_ref[...] = st_ref[...].T


def _hgrn(q, k, lf, v, g, s0, gain, *, dk):
    b, t, hw = q.shape
    nh = hw // dk
    c = _tile(t, 256)
    tt = _tile(t, 512)
    msk = _hgrn_masks(c)
    tri = jnp.asarray(np.tril(np.ones((c, c), np.float32)), BF16)
    tok = lambda bb, h, ti: (bb, ti, h)
    smap = lambda bb, h, ti: (bb, h, 0, 0)
    return pl.pallas_call(
        functools.partial(_hgrn_kernel, c=c, n_chunks=tt // c),
        grid=(b, nh, t // tt),
        in_specs=[pl.BlockSpec((None, tt, dk), tok)] * 5
        + [pl.BlockSpec((None, None, dk, dk), smap),
           pl.BlockSpec((c, c), lambda bb, h, ti: (0, 0)),
           pl.BlockSpec(msk.shape, lambda bb, h, ti: (0, 0, 0)),
           pl.BlockSpec((1, dk), lambda bb, h, ti: (0, 0))],
        out_specs=[pl.BlockSpec((None, tt, dk), tok), pl.BlockSpec((None, None, dk, dk), smap)],
        out_shape=[jax.ShapeDtypeStruct((b, t, hw), BF16), jax.ShapeDtypeStruct((b, nh, dk, dk), F32)],
        scratch_shapes=[pltpu.VMEM((dk, dk), F32)],
        compiler_params=_cparams(("parallel", "parallel", "arbitrary")),
        name="hgrn2",
    )(q, k, lf, v, g, s0, tri, msk, gain)


def _outproj_kernel(x_ref, ya_ref, yb_ref, wa_ref, wb_ref, o_ref):
    o_ref[...] = x_ref[...] + _dot(ya_ref[...], wa_ref[...]) + _dot(yb_ref[...], wb_ref[...])


def _outproj(x, ya, yb, wa, wb):
    n, d = x.shape
    tm = _tile(n, 512)
    row = lambda i: (i, 0)
    const = lambda i: (0, 0)
    return pl.pallas_call(
        _outproj_kernel,
        grid=(n // tm,),
        in_specs=[pl.BlockSpec((tm, d), row), pl.BlockSpec((tm, ya.shape[1]), row),
                  pl.BlockSpec((tm, yb.shape[1]), row), pl.BlockSpec(wa.shape, const),
                  pl.BlockSpec(wb.shape, const)],
        out_specs=pl.BlockSpec((tm, d), row),
        out_shape=jax.ShapeDtypeStruct((n, d), F32),
        compiler_params=_cparams(("parallel",)),
        name="outproj",
    )(x, ya, yb, wa, wb)


def _ffn_kernel(x_ref, g_ref, wg_ref, wu_ref, wd_ref, o_ref, h_ref, acc_ref):
    f = pl.program_id(1)

    @pl.when(f == 0)
    def _():
        h_ref[...] = (_rms(x_ref[...]) * g_ref[...]).astype(BF16)
        acc_ref[...] = jnp.zeros_like(acc_ref)

    h = h_ref[...]
    gate = _dot(h, wg_ref[...])
    up = _dot(h, wu_ref[...])
    acc_ref[...] += _dot(((gate * _sigmoid(gate)) * up).astype(BF16), wd_ref[...])

    @pl.when(f == pl.num_programs(1) - 1)
    def _():
        o_ref[...] = x_ref[...] + acc_ref[...]


def _ffn(x, g, wg, wu, wd):
    n, d = x.shape
    ff = wg.shape[1]
    tm = _tile(n, 1024)
    tf = _tile(ff, 256)
    return pl.pallas_call(
        _ffn_kernel,
        grid=(n // tm, ff // tf),
        in_specs=[pl.BlockSpec((tm, d), lambda i, f: (i, 0)), pl.BlockSpec((1, d), lambda i, f: (0, 0)),
                  pl.BlockSpec((d, tf), lambda i, f: (0, f)), pl.BlockSpec((d, tf), lambda i, f: (0, f)),
                  pl.BlockSpec((tf, d), lambda i, f: (f, 0))],
        out_specs=pl.BlockSpec((tm, d), lambda i, f: (i, 0)),
        out_shape=jax.ShapeDtypeStruct((n, d), F32),
        scratch_shapes=[pltpu.VMEM((tm, d), BF16), pltpu.VMEM((tm, d), F32)],
        compiler_params=_cparams(("parallel", "arbitrary")),
        name="ffn_dense",
    )(x, g, wg, wu, wd)


ROUTE_E1, ROUTE_E2, ROUTE_R1, ROUTE_R2, ROUTE_G1, ROUTE_G2 = range(6)


def _router_kernel(x_ref, g_ref, rhi_ref, rlo_ref, tri_ref, route_ref, cnt_ref, base_ref, *, n_experts):
    i = pl.program_id(0)
    lane = lax.broadcasted_iota(jnp.int32, (1, LANES), 1)

    @pl.when(i == 0)
    def _():
        base_ref[...] = jnp.zeros_like(base_ref)

    hf = _rms(x_ref[...]) * g_ref[...]
    h_hi = hf.astype(BF16)
    h_lo = (hf - h_hi.astype(F32)).astype(BF16)
    logits = _dot(h_hi, rhi_ref[...]) + _dot(h_lo, rhi_ref[...]) + _dot(h_hi, rlo_ref[...])
    logits = jnp.where(lane < n_experts, logits, -jnp.inf)
    m1 = jnp.max(logits, axis=-1, keepdims=True)
    i1 = jnp.min(jnp.where(logits == m1, lane, LANES), axis=-1, keepdims=True)
    rest = jnp.where(lane == i1, -jnp.inf, logits)
    m2 = jnp.max(rest, axis=-1, keepdims=True)
    i2 = jnp.min(jnp.where(rest == m2, lane, LANES), axis=-1, keepdims=True)
    e2 = jnp.exp(m2 - m1)
    g1 = 1.0 / (1.0 + e2)
    g2 = e2 / (1.0 + e2)
    oh1 = (lane == i1).astype(F32)
    oh2 = (lane == i2).astype(F32)
    both = oh1 + oh2
    before = _dot(tri_ref[...], both.astype(BF16)) + base_ref[...]
    r1 = jnp.sum(oh1 * before, axis=-1, keepdims=True)
    r2 = jnp.sum(oh2 * before, axis=-1, keepdims=True)
    rec = jnp.zeros(route_ref.shape, F32)
    for slot, val in ((ROUTE_E1, i1.astype(F32)), (ROUTE_E2, i2.astype(F32)), (ROUTE_R1, r1),
                      (ROUTE_R2, r2), (ROUTE_G1, g1), (ROUTE_G2, g2)):
        rec = jnp.where(lane == slot, val, rec)
    route_ref[...] = rec
    base_ref[...] += jnp.sum(both, axis=0, keepdims=True)
    cnt_ref[...] = base_ref[...]


def _router(x, g, router):
    n, d = x.shape
    ne = router.shape[1]
    tm = _tile(n, 1024)
    rpad = jnp.zeros((d, LANES), F32).at[:, :ne].set(router)
    rhi = rpad.astype(BF16)
    rlo = (rpad - rhi.astype(F32)).astype(BF16)
    const = lambda i: (0, 0)
    return pl.pallas_call(
        functools.partial(_router_kernel, n_experts=ne),
        grid=(n // tm,),
        in_specs=[pl.BlockSpec((tm, d), lambda i: (i, 0)), pl.BlockSpec((1, d), const),
                  pl.BlockSpec((d, LANES), const), pl.BlockSpec((d, LANES), const),
                  pl.BlockSpec((tm, tm), const)],
        out_specs=[pl.BlockSpec((tm, LANES), lambda i: (i, 0)), pl.BlockSpec((1, LANES), const)],
        out_shape=[jax.ShapeDtypeStruct((n, LANES), F32), jax.ShapeDtypeStruct((1, LANES), F32)],
        scratch_shapes=[pltpu.VMEM((1, LANES), F32)],
        compiler_params=_cparams(("arbitrary",)),
        name="moe_router",
    )(x, g, rhi, rlo, _tril_strict(tm))


def _row_copy(src_ref, src_row, dst_ref, dst_row, sem):
    return pltpu.make_async_copy(src_ref.at[pl.ds(src_row, 1), :], dst_ref.at[pl.ds(dst_row, 1), :], sem)


def _dispatch_kernel(p1_ref, p2_ref, x_ref, xs_in_ref, xs_ref, sem, *, tb):
    del xs_in_ref

    def issue(t, carry):
        _row_copy(x_ref, t, xs_ref, p1_ref[0, 0, t], sem).start()
        _row_copy(x_ref, t, xs_ref, p2_ref[0, 0, t], sem).start()
        return carry

    lax.fori_loop(0, tb, issue, 0, unroll=8)
    for _ in range(TOP_K):
        pltpu.make_async_copy(x_ref, xs_ref.at[pl.ds(0, tb), :], sem).wait()


def _dispatch(x, pos1, pos2, n_rows):
    n, d = x.shape
    tb = pos1.shape[-1]
    smem = lambda: pl.BlockSpec((1, 1, tb), lambda i: (i, 0, 0), memory_space=pltpu.SMEM)
    return pl.pallas_call(
        functools.partial(_dispatch_kernel, tb=tb),
        grid=(n // tb,),
        in_specs=[smem(), smem(), pl.BlockSpec((tb, d), lambda i: (i, 0)), pl.BlockSpec(memory_space=pl.ANY)],
        out_specs=pl.BlockSpec(memory_space=pl.ANY),
        out_shape=jax.ShapeDtypeStruct((n_rows, d), F32),
        scratch_shapes=[pltpu.SemaphoreType.DMA(())],
        input_output_aliases={3: 0},
        compiler_params=_cparams(("arbitrary",)),
        name="moe_dispatch",
    )(pos1, pos2, x, jnp.zeros((n_rows, d), F32))


def _experts_kernel(te_ref, nv_ref, x_ref, g_ref, wg_ref, wu_ref, wd_ref, o_ref, h_ref, acc_ref):
    del te_ref
    i = pl.program_id(0)
    f = pl.program_id(1)
    valid = i < nv_ref[0]

    @pl.when(valid & (f == 0))
    def _():
        h_ref[...] = (_rms(x_ref[...]) * g_ref[...]).astype(BF16)
        acc_ref[...] = jnp.zeros_like(acc_ref)

    @pl.when(valid)
    def _():
        h = h_ref[...]
        gate = _dot(h, wg_ref[...])
        up = _dot(h, wu_ref[...])
        acc_ref[...] += _dot(((gate * _sigmoid(gate)) * up).astype(BF16), wd_ref[...])

    @pl.when(valid & (f == pl.num_programs(1) - 1))
    def _():
        o_ref[...] = acc_ref[...]

    @pl.when(jnp.logical_not(valid) & (f == pl.num_programs(1) - 1))
    def _():
        o_ref[...] = jnp.zeros_like(o_ref)


def _experts(xs, g, wg, wu, wd, tile_expert, n_valid, tr):
    n_rows, d = xs.shape
    ff = wg.shape[2]
    tf = _tile(ff, 256)
    nf = ff // tf
    rowmap = lambda i, f, te, nv: (jnp.minimum(i, nv[0] - 1), 0)
    fsel = lambda i, f, nv: jnp.where(i < nv[0], f, nf - 1)
    grid_spec = pltpu.PrefetchScalarGridSpec(
        num_scalar_prefetch=2,
        grid=(n_rows // tr, nf),
        in_specs=[pl.BlockSpec((tr, d), rowmap), pl.BlockSpec((1, d), lambda i, f, te, nv: (0, 0)),
                  pl.BlockSpec((None, d, tf), lambda i, f, te, nv: (te[i], 0, fsel(i, f, nv))),
                  pl.BlockSpec((None, d, tf), lambda i, f, te, nv: (te[i], 0, fsel(i, f, nv))),
                  pl.BlockSpec((None, tf, d), lambda i, f, te, nv: (te[i], fsel(i, f, nv), 0))],
        out_specs=pl.BlockSpec((tr, d), lambda i, f, te, nv: (i, 0)),
        scratch_shapes=[pltpu.VMEM((tr, d), BF16), pltpu.VMEM((tr, d), F32)],
    )
    return pl.pallas_call(
        _experts_kernel,
        grid_spec=grid_spec,
        out_shape=jax.ShapeDtypeStruct((n_rows, d), F32),
        compiler_params=_cparams(("arbitrary", "arbitrary")),
        name="moe_experts",
    )(tile_expert, n_valid, xs, g, wg, wu, wd)


def _combine_kernel(p1_ref, p2_ref, x_ref, route_ref, ys_ref, o_ref, buf_ref, sem, *, tc):
    def issue(t, carry):
        _row_copy(ys_ref, p1_ref[0, 0, t], buf_ref.at[0], t, sem).start()
        _row_copy(ys_ref, p2_ref[0, 0, t], buf_ref.at[1], t, sem).start()
        return carry

    lax.fori_loop(0, tc, issue, 0, unroll=8)
    for slot in range(TOP_K):
        pltpu.make_async_copy(ys_ref.at[pl.ds(0, tc), :], buf_ref.at[slot], sem).wait()
    lane = lax.broadcasted_iota(jnp.int32, (1, LANES), 1)
    route = route_ref[...]
    g1 = jnp.sum(jnp.where(lane == ROUTE_G1, route, 0.0), axis=-1, keepdims=True)
    g2 = jnp.sum(jnp.where(lane == ROUTE_G2, route, 0.0), axis=-1, keepdims=True)
    o_ref[...] = x_ref[...] + (g1 * buf_ref[0] + g2 * buf_ref[1])


def _combine(x, route, ys, pos1, pos2):
    n, d = x.shape
    tc = pos1.shape[-1]
    smem = lambda: pl.BlockSpec((1, 1, tc), lambda i: (i, 0, 0), memory_space=pltpu.SMEM)
    return pl.pallas_call(
        functools.partial(_combine_kernel, tc=tc),
        grid=(n // tc,),
        in_specs=[smem(), smem(), pl.BlockSpec((tc, d), lambda i: (i, 0)),
                  pl.BlockSpec((tc, LANES), lambda i: (i, 0)), pl.BlockSpec(memory_space=pl.ANY)],
        out_specs=pl.BlockSpec((tc, d), lambda i: (i, 0)),
        out_shape=jax.ShapeDtypeStruct((n, d), F32),
        scratch_shapes=[pltpu.VMEM((TOP_K, tc, d), F32), pltpu.SemaphoreType.DMA(())],
        compiler_params=_cparams(("arbitrary",)),
        name="moe_combine",
    )(pos1, pos2, x, route, ys)


def _moe(x, g, router, wg, wu, wd):
    n, d = x.shape
    ne = wg.shape[0]
    tr = 1024 if n >= 8192 else 256
    tb = _tile(n, 512)
    route, cnt = _router(x, g, router)
    counts = cnt[0, :ne].astype(jnp.int32)
    padded = ((counts + tr - 1) // tr) * tr
    ends = jnp.cumsum(padded)
    starts = ends - padded
    e1 = route[:, ROUTE_E1].astype(jnp.int32)
    e2 = route[:, ROUTE_E2].astype(jnp.int32)
    pos1 = (starts[e1] + route[:, ROUTE_R1].astype(jnp.int32)).reshape(n // tb, 1, tb)
    pos2 = (starts[e2] + route[:, ROUTE_R2].astype(jnp.int32)).reshape(n // tb, 1, tb)
    max_tiles = -(-(TOP_K * n + ne * (tr - 1)) // tr)
    n_valid = (ends[-1] // tr).reshape(1)
    tile_start = jnp.minimum(jnp.arange(max_tiles, dtype=jnp.int32), n_valid[0] - 1) * tr
    tile_expert = jnp.minimum(jnp.searchsorted(ends, tile_start, side="right"), ne - 1).astype(jnp.int32)
    xs = _dispatch(x, pos1, pos2, max_tiles * tr)
    ys = _experts(xs, g, wg, wu, wd, tile_expert, n_valid.astype(jnp.int32), tr)
    return _combine(x, route, ys, pos1, pos2)


def kernel(x_prompt, x_sample, cache_k, cache_v, state_hgrn, norm_mix, w_in, sb_q_gain, sb_k_gain,
           hg_lower_bounds, sb_out_gain, hg_out_gain, w_out, norm_ffn, ffn_w_gate, ffn_w_up, ffn_w_down,
           moe_router, moe_w_gate, moe_w_up, moe_w_down):
    depth = w_in.shape[0]
    bp, tp, d = x_prompt.shape
    bs, ts, _ = x_sample.shape
    past = cache_k.shape[2]
    sb_heads, hd = cache_k.shape[3], cache_k.shape[4]
    sw = sb_heads * hd
    dk = hg_out_gain.shape[1]
    hw = hg_lower_bounds.shape[1]
    hg_heads = hw // dk
    sb_scale = hd ** -0.5

    lbs = jnp.cumsum(jax.nn.softmax(hg_lower_bounds.astype(F32), axis=0), axis=0)
    lbs = lbs - lbs[0:1]
    head_mean = jnp.asarray(np.kron(np.eye(sb_heads), np.full((hd, hd), 1.0 / hd)), BF16)

    xp = x_prompt.reshape(bp * tp, d)
    xs = x_sample.reshape(bs * ts, d)
    zeros_state = jnp.zeros((bp, hg_heads, dk, dk), F32)
    outs = {k: [] for k in ("kp", "vp", "sp", "ks", "vs", "ss")}

    def mixer(x, b, t, l, *, k_past, v_past, past_len, s0, bq, bkp):
        qg = (jnp.tile(sb_q_gain[l], sb_heads) * sb_scale)[None, :]
        kg = jnp.tile(sb_k_gain[l], sb_heads)[None, :]
        q, kf, kb, vf, vb, qh, kh, lf, ih, gh = _inproj(
            x, norm_mix[l][None, :], w_in[l].astype(BF16), head_mean, qg, kg,
            jnp.log(lbs[l])[None, :], jnp.log1p(-lbs[l])[None, :], (1.0 - lbs[l])[None, :], sw=sw, hw=hw)
        r3 = lambda a: a.reshape(b, t, a.shape[-1])
        kb3, vb3 = r3(kb), r3(vb)
        ya = _attention(r3(q), kb3, vb3, kb3 if k_past is None else k_past, vb3 if v_past is None else v_past,
                        jnp.tile(sb_out_gain[l], LANES // hd)[None, :],
                        bq=bq, bkp=bkp, past_len=past_len, hd=hd)
        yb, s_fin = _hgrn(r3(qh), r3(kh), r3(lf), r3(ih), r3(gh), s0, hg_out_gain[l][None, :], dk=dk)
        wo = w_out[l].astype(BF16)
        x1 = _outproj(x, ya.reshape(b * t, sw), yb.reshape(b * t, hw), wo[:sw], wo[sw:])
        return x1, kf.reshape(b, t, sb_heads, hd), vf.reshape(b, t, sb_heads, hd), s_fin

    def channel(x, l):
        j = l // 2
        if l % 2 == 0:
            return _ffn(x, norm_ffn[l][None, :], ffn_w_gate[j].astype(BF16), ffn_w_up[j].astype(BF16),
                        ffn_w_down[j].astype(BF16))
        return _moe(x, norm_ffn[l][None, :], moe_router[j], moe_w_gate[j].astype(BF16),
                    moe_w_up[j].astype(BF16), moe_w_down[j].astype(BF16))

    bq_p = _tile(tp, 512)
    for l in range(depth):
        xp, k4, v4, s_p = mixer(xp, bp, tp, l, k_past=None, v_past=None, past_len=None, s0=zeros_state,
                                bq=bq_p, bkp=bq_p)
        outs["kp"].append(k4)
        outs["vp"].append(v4)
        outs["sp"].append(s_p)
        xs, k4, v4, s_s = mixer(xs, bs, ts, l,
                                k_past=cache_k[l].reshape(bs, past, sw), v_past=cache_v[l].reshape(bs, past, sw),
                                past_len=past, s0=state_hgrn[l].astype(F32), bq=ts, bkp=_tile(past, 512))
        outs["ks"].append(k4)
        outs["vs"].append(v4)
        outs["ss"].append(s_s)
        xp = channel(xp, l)
        xs = channel(xs, l)

    return (xp.reshape(bp, tp, d), xs.reshape(bs, ts, d), jnp.stack(outs["kp"]), jnp.stack(outs["vp"]),
            jnp.stack(outs["sp"]), jnp.stack(outs["ks"]), jnp.stack(outs["vs"]), jnp.stack(outs["ss"]))
```

```python
import functools

import numpy as np
import jax
import jax.numpy as jnp
from jax import lax
from jax.experimental import pallas as pl
from jax.experimental.pallas import tpu as pltpu

EPS = 1e-6
TOP_K = 2
LANES = 128
F32 = jnp.float32
BF16 = jnp.bfloat16
VMEM_LIMIT_BYTES = 56 * 1024 * 1024


def _cparams(semantics):
    return pltpu.CompilerParams(dimension_semantics=semantics, vmem_limit_bytes=VMEM_LIMIT_BYTES)


def _dot(a, b):
    return jnp.dot(a, b, preferred_element_type=F32)


def _dot_nt(a, b):
    return lax.dot_general(a, b, (((1,), (1,)), ((), ())), preferred_element_type=F32)


def _dot_tn(a, b):
    return lax.dot_general(a, b, (((0,), (0,)), ((), ())), preferred_element_type=F32)


def _sigmoid(x):
    return 1.0 / (1.0 + jnp.exp(-x))


def _rms(x):
    return x * lax.rsqrt(jnp.mean(x * x, axis=-1, keepdims=True) + EPS)


def _tile(n, pref):
    t = min(n, pref)
    assert n % t == 0, (n, t)
    return t


def _inproj_kernel(*refs, sw, hw, n_prev, seq_per_tile):
    x_ref, g_ref, w_ref, hm_ref, qg_ref, kg_ref, llb_ref, l1m_ref, oml_ref = refs[:9]
    prev = refs[9:9 + (2 if n_prev else 0)]
    q_ref, kf_ref, kb_ref, vf_ref, vb_ref, qh_ref, kh_ref, lf_ref, ih_ref, gh_ref = refs[9 + len(prev):]
    if n_prev:
        kf_ref[:n_prev] = prev[0][...]
        vf_ref[:n_prev] = prev[1][...]

    def store_time_minor(ref, y):
        t = y.shape[0] // seq_per_tile
        for sq in range(seq_per_tile):
            ref[n_prev, sq] = y[sq * t:(sq + 1) * t, :].T

    h = (_rms(x_ref[...]) * g_ref[...]).astype(BF16)

    def seg(lo, width):
        return _dot(h, w_ref[:, lo:lo + width])

    def headnorm(y, gain):
        m = _dot((y * y).astype(BF16), hm_ref[...])
        return (y * lax.rsqrt(m + EPS)) * gain

    q_ref[...] = headnorm(seg(0, sw), qg_ref[...]).astype(BF16)
    ka = headnorm(seg(sw, sw), kg_ref[...])
    store_time_minor(kf_ref, ka)
    kb_ref[...] = ka.astype(BF16)
    va = seg(2 * sw, sw)
    store_time_minor(vf_ref, va)
    vb_ref[...] = va.astype(BF16)

    o = 3 * sw
    qb = seg(o, hw)
    qh_ref[...] = qb * _sigmoid(qb)

    fb = seg(o + hw, hw)
    e = jnp.exp(-jnp.abs(fb))
    log_sig = jnp.minimum(fb, 0.0) - jnp.log(1.0 + e)
    c = l1m_ref[...] + log_sig
    a = llb_ref[...]
    lf_ref[...] = jnp.maximum(a, c) + jnp.log(1.0 + jnp.exp(-jnp.abs(a - c)))
    kh_ref[...] = oml_ref[...] * (jnp.where(fb >= 0.0, e, 1.0) / (1.0 + e))

    ih_ref[...] = seg(o + 2 * hw, hw)
    gh_ref[...] = seg(o + 3 * hw, hw)


def _inproj(x, g, w, hm, qg, kg, llb, l1m, oml, kv_prev, *, sw, hw, seq_len):
    n, d = x.shape
    tm = _tile(n, 512)
    n_prev = 0 if kv_prev is None else kv_prev[0].shape[0]
    spt = max(tm // seq_len, 1)
    tt = tm // spt
    nt = seq_len // tt
    assert spt * tt == tm and nt * tt == seq_len
    row = lambda i: (i, 0)
    const = lambda i: (0, 0)
    stacked = lambda i: (0, i // nt, 0, i % nt)
    kv_shape = (n_prev + 1, n // seq_len, sw, seq_len)
    specs = {"row_sw": pl.BlockSpec((tm, sw), row), "row_hw": pl.BlockSpec((tm, hw), row),
             "stack": pl.BlockSpec((n_prev + 1, spt, sw, tt), stacked)}
    outs = [("row_sw", (n, sw), BF16), ("stack", kv_shape, F32), ("row_sw", (n, sw), BF16),
            ("stack", kv_shape, F32), ("row_sw", (n, sw), BF16)] + [("row_hw", (n, hw), F32)] * 5
    prev_specs = [pl.BlockSpec((n_prev, spt, sw, tt), stacked)] * 2 if n_prev else []
    return pl.pallas_call(
        functools.partial(_inproj_kernel, sw=sw, hw=hw, n_prev=n_prev, seq_per_tile=spt),
        grid=(n // tm,),
        in_specs=[pl.BlockSpec((tm, d), row), pl.BlockSpec((1, d), const),
                  pl.BlockSpec(w.shape, const), pl.BlockSpec(hm.shape, const),
                  pl.BlockSpec((1, sw), const), pl.BlockSpec((1, sw), const),
                  pl.BlockSpec((1, hw), const), pl.BlockSpec((1, hw), const), pl.BlockSpec((1, hw), const)]
        + prev_specs,
        out_specs=[specs[kind] for kind, _, _ in outs],
        out_shape=[jax.ShapeDtypeStruct(shape, dt) for _, shape, dt in outs],
        compiler_params=_cparams(("parallel",)),
        name="inproj",
    )(x, g, w, hm, qg, kg, llb, l1m, oml, *(kv_prev or ()))


KEY_SUB = 256
SOFTPLUS_CLAMP = 80.0
CARRY_DEAD = 104.0


def _attn_kernel(qi_tab, kp_tab, fl_tab,
                 q_ref, kn_ref, vn_ref, kp_ref, vp_ref, un_ref, up_ref, gain_ref,
                 y_ref, qm_ref, acc_ref, carry_ref, alive_ref, *, bq, bkp, subn, subp, hd, past_transposed):
    s = pl.program_id(2)
    flags = fl_tab[s]
    masked = (flags & 1) == 1
    first = (flags & 2) == 2
    last = (flags & 4) == 4
    lane = lax.broadcasted_iota(jnp.int32, (1, LANES), 1)
    heads = LANES // hd

    @pl.when(first)
    def _():
        q = q_ref[...]
        for hh in range(heads):
            in_head = (lane >= hh * hd) & (lane < (hh + 1) * hd)
            qm_ref[hh] = jnp.where(in_head, q, jnp.zeros_like(q))
        acc_ref[...] = jnp.zeros_like(acc_ref)
        carry_ref[...] = jnp.zeros_like(carry_ref)

    def sweep(k_ref, v_ref, u_ref, bk, sub, diagonal, transposed):
        u = u_ref[...]
        for j in range(bk // sub - 1, -1, -1):
            r0 = j * sub if diagonal else 0
            if diagonal:
                rows = r0 + lax.broadcasted_iota(jnp.int32, (bq - r0, sub), 0)
                cols = j * sub + lax.broadcasted_iota(jnp.int32, (bq - r0, sub), 1)
                mask = cols < rows
            for hh in range(heads):
                def one_head(hh=hh, j=j, r0=r0):
                    if transposed:
                        k = k_ref[:, j * sub:(j + 1) * sub].astype(BF16)
                        v = v_ref[:, j * sub:(j + 1) * sub].astype(BF16)
                        z = _dot(qm_ref[hh, r0:, :], k)
                    else:
                        k = k_ref[j * sub:(j + 1) * sub, :].astype(BF16)
                        v = v_ref[j * sub:(j + 1) * sub, :].astype(BF16)
                        z = _dot_nt(qm_ref[hh, r0:, :], k)
                    sp = jnp.maximum(jnp.log(1.0 + jnp.exp(jnp.minimum(z, SOFTPLUS_CLAMP))), z)
                    if diagonal:
                        sp = jnp.where(mask, sp, 0.0)
                    after = _dot(sp.astype(BF16), u)
                    w = jnp.exp((z - sp) - after)
                    if diagonal:
                        w = jnp.where(mask, w, 0.0)
                    pv = _dot_nt(w.astype(BF16), v) if transposed else _dot(w.astype(BF16), v)
                    c = carry_ref[hh, r0:, :]
                    acc_ref[hh, r0:, :] += jnp.exp(-c) * pv
                    c_new = c + jnp.sum(sp, axis=-1, keepdims=True)
                    carry_ref[hh, r0:, :] = c_new
                    if r0 == 0:
                        alive_ref[hh] = (jnp.min(c_new) < CARRY_DEAD).astype(jnp.int32)

                if diagonal:
                    one_head()
                else:
                    pl.when(alive_ref[hh] == 1)(one_head)

    @pl.when(masked)
    def _():
        sweep(kn_ref, vn_ref, un_ref, bq, subn, True, False)

    @pl.when(jnp.logical_not(masked))
    def _():
        sweep(kp_ref, vp_ref, up_ref, bkp, subp, False, past_transposed)

    @pl.when(last)
    def _():
        o = acc_ref[0]
        for hh in range(1, heads):
            in_head = (lane >= hh * hd) & (lane < (hh + 1) * hd)
            o = jnp.where(in_head, acc_ref[hh], o)
        o2 = o * o
        ms = jnp.zeros_like(o)
        for hh in range(heads):
            in_head = (lane >= hh * hd) & (lane < (hh + 1) * hd)
            ssum = jnp.sum(jnp.where(in_head, o2, 0.0), axis=-1, keepdims=True)
            ms = jnp.where(in_head, ssum * (1.0 / hd), ms)
        y_ref[...] = ((o * lax.rsqrt(ms + EPS)) * gain_ref[...]).astype(y_ref.dtype)


def _tril_strict(n):
    return jnp.asarray(np.tril(np.ones((n, n), np.float32), -1), BF16)


def _attention(q, k_new, v_new, k_past, v_past, gain, *, bq, bkp, past_len, hd, past_transposed=False):
    b, tq, w = q.shape
    nq = tq // bq
    subn = min(bq, KEY_SUB)
    subp = min(bkp, KEY_SUB)
    assert bq % subn == 0 and bkp % subp == 0
    qi_l, kp_l, fl_l = [], [], []
    for qi in range(nq):
        n_past = (qi * bq if past_len is None else past_len) // bkp
        blocks = [None] + list(range(n_past - 1, -1, -1))
        for idx, blk in enumerate(blocks):
            qi_l.append(qi)
            kp_l.append(max(n_past - 1, 0) if blk is None else blk)
            fl_l.append((1 if blk is None else 0) | (2 if idx == 0 else 0) | (4 if idx == len(blocks) - 1 else 0))
    tabs = [jnp.asarray(np.asarray(t, np.int32)) for t in (qi_l, kp_l, fl_l)]
    heads = LANES // hd

    qmap = lambda bb, p, s, qi, kp, fl: (bb, qi[s], p)
    pmap = lambda bb, p, s, qi, kp, fl: (bb, kp[s], p)
    cmap = lambda bb, p, s, qi, kp, fl: (0, 0)
    if past_transposed:
        past_spec = pl.BlockSpec((None, LANES, bkp), lambda bb, p, s, qi, kp, fl: (bb, p, kp[s]))
    else:
        past_spec = pl.BlockSpec((None, bkp, LANES), pmap)
    grid_spec = pltpu.PrefetchScalarGridSpec(
        num_scalar_prefetch=3,
        grid=(b, w // LANES, len(qi_l)),
        in_specs=[pl.BlockSpec((None, bq, LANES), qmap),
                  pl.BlockSpec((None, bq, LANES), qmap), pl.BlockSpec((None, bq, LANES), qmap),
                  past_spec, past_spec,
                  pl.BlockSpec((subn, subn), cmap), pl.BlockSpec((subp, subp), cmap),
                  pl.BlockSpec((1, LANES), cmap)],
        out_specs=pl.BlockSpec((None, bq, LANES), qmap),
        scratch_shapes=[pltpu.VMEM((heads, bq, LANES), BF16), pltpu.VMEM((heads, bq, LANES), F32),
                        pltpu.VMEM((heads, bq, LANES), F32), pltpu.SMEM((heads,), jnp.int32)],
    )
    return pl.pallas_call(
        functools.partial(_attn_kernel, bq=bq, bkp=bkp, subn=subn, subp=subp, hd=hd,
                          past_transposed=past_transposed),
        grid_spec=grid_spec,
        out_shape=jax.ShapeDtypeStruct((b, tq, w), BF16),
        compiler_params=_cparams(("parallel", "parallel", "arbitrary")),
        name="stickbreak_attn",
    )(*tabs, q, k_new, v_new, k_past, v_past, _tril_strict(subn), _tril_strict(subp), gain)


def _hgrn_halvings(c):
    return [c >> (i + 1) for i in range(int(np.log2(c)))]


def _hgrn_masks(c):
    t = np.arange(c)
    masks = [np.eye(c, dtype=bool)]
    for h in _hgrn_halvings(c):
        blk = t // (2 * h)
        second = (t // h) % 2 == 1
        masks.append((blk[:, None] == blk[None, :]) & second[:, None] & (~second)[None, :])
    return jnp.asarray(np.stack(masks).astype(np.float32))


def _boundary_rows(b, h, row):
    c, dk = b.shape
    if 2 * h >= 8:
        n = c // (2 * h)
        ref = b.reshape(n, 2 * h, dk)[:, h - 1:h, :]
        return jnp.broadcast_to(ref, (n, 2 * h, dk)).reshape(c, dk)
    down1 = pltpu.roll(b, 1, 0)
    if h == 1:
        return jnp.where((row & 1) == 1, down1, b)
    m = row & 3
    up1 = pltpu.roll(b, c - 1, 0)
    down2 = pltpu.roll(b, 2, 0)
    return jnp.where(m == 0, up1, jnp.where(m == 1, b, jnp.where(m == 2, down1, down2)))


def _hgrn_kernel(q_ref, k_ref, lf_ref, v_ref, g_ref, s0_ref, tri_ref, msk_ref, gain_ref,
                 y_ref, sout_ref, st_ref, *, c, n_chunks):
    t = pl.program_id(2)
    dk = q_ref.shape[-1]

    @pl.when(t == 0)
    def _():
        st_ref[...] = s0_ref[...].T

    row = lax.broadcasted_iota(jnp.int32, (c, dk), 0)
    tri = tri_ref[...]
    for ci in range(n_chunks):
        sl = slice(ci * c, (ci + 1) * c)
        q = q_ref[sl, :]
        k = k_ref[sl, :]
        lf = lf_ref[sl, :]
        v = v_ref[sl, :].astype(BF16)
        g = g_ref[sl, :]
        hi = lf.astype(BF16)
        r1 = lf - hi.astype(F32)
        mid = r1.astype(BF16)
        lo = (r1 - mid.astype(F32)).astype(BF16)
        b = _dot(tri, hi) + _dot(tri, mid) + _dot(tri, lo)
        st = st_ref[...]
        o = _dot_nt((q * jnp.exp(b)).astype(BF16), st.astype(BF16))
        a = jnp.where(msk_ref[0] > 0.0, _dot_nt(q.astype(BF16), k.astype(BF16)), 0.0)
        for lv, h in enumerate(_hgrn_halvings(c)):
            gap = b - _boundary_rows(b, h, row)
            el = jnp.exp(jnp.where((row & h) != 0, gap, -gap))
            a += jnp.where(msk_ref[1 + lv] > 0.0, _dot_nt((q * el).astype(BF16), (k * el).astype(BF16)), 0.0)
        o += _dot(a.astype(BF16), v)
        b_last = b[c - 1:c, :]
        k_end = (k * jnp.exp(b_last - b)).astype(BF16)
        st_ref[...] = st * jnp.exp(b_last) + _dot_tn(v, k_end)
        y = (_rms(o) * gain_ref[...]) * (g * _sigmoid(g))
        y_ref[sl, :] = y.astype(y_ref.dtype)

    @pl.when(t == pl.num_programs(2) - 1)
    def _():
        sout_ref[...] = st_ref[...].T


def _hgrn(q, k, lf, v, g, s0, gain, *, dk):
    b, t, hw = q.shape
    nh = hw // dk
    c = _tile(t, 256)
    tt = _tile(t, 512)
    msk = _hgrn_masks(c)
    tri = jnp.asarray(np.tril(np.ones((c, c), np.float32)), BF16)
    tok = lambda bb, h, ti: (bb, ti, h)
    smap = lambda bb, h, ti: (bb, h, 0, 0)
    return pl.pallas_call(
        functools.partial(_hgrn_kernel, c=c, n_chunks=tt // c),
        grid=(b, nh, t // tt),
        in_specs=[pl.BlockSpec((None, tt, dk), tok)] * 5
        + [pl.BlockSpec((None, None, dk, dk), smap),
           pl.BlockSpec((c, c), lambda bb, h, ti: (0, 0)),
           pl.BlockSpec(msk.shape, lambda bb, h, ti: (0, 0, 0)),
           pl.BlockSpec((1, dk), lambda bb, h, ti: (0, 0))],
        out_specs=[pl.BlockSpec((None, tt, dk), tok), pl.BlockSpec((None, None, dk, dk), smap)],
        out_shape=[jax.ShapeDtypeStruct((b, t, hw), BF16), jax.ShapeDtypeStruct((b, nh, dk, dk), F32)],
        scratch_shapes=[pltpu.VMEM((dk, dk), F32)],
        compiler_params=_cparams(("parallel", "parallel", "arbitrary")),
        name="hgrn2",
    )(q, k, lf, v, g, s0, tri, msk, gain)


def _outproj_kernel(x_ref, ya_ref, yb_ref, wa_ref, wb_ref, o_ref):
    o_ref[...] = x_ref[...] + _dot(ya_ref[...], wa_ref[...]) + _dot(yb_ref[...], wb_ref[...])


def _outproj(x, ya, yb, wa, wb):
    n, d = x.shape
    tm = _tile(n, 512)
    row = lambda i: (i, 0)
    const = lambda i: (0, 0)
    return pl.pallas_call(
        _outproj_kernel,
        grid=(n // tm,),
        in_specs=[pl.BlockSpec((tm, d), row), pl.BlockSpec((tm, ya.shape[1]), row),
                  pl.BlockSpec((tm, yb.shape[1]), row), pl.BlockSpec(wa.shape, const),
                  pl.BlockSpec(wb.shape, const)],
        out_specs=pl.BlockSpec((tm, d), row),
        out_shape=jax.ShapeDtypeStruct((n, d), F32),
        compiler_params=_cparams(("parallel",)),
        name="outproj",
    )(x, ya, yb, wa, wb)


def _ffn_kernel(x_ref, g_ref, wg_ref, wu_ref, wd_ref, o_ref, h_ref, acc_ref):
    f = pl.program_id(1)

    @pl.when(f == 0)
    def _():
        h_ref[...] = (_rms(x_ref[...]) * g_ref[...]).astype(BF16)
        acc_ref[...] = jnp.zeros_like(acc_ref)

    h = h_ref[...]
    gate = _dot(h, wg_ref[...])
    up = _dot(h, wu_ref[...])
    acc_ref[...] += _dot(((gate * _sigmoid(gate)) * up).astype(BF16), wd_ref[...])

    @pl.when(f == pl.num_programs(1) - 1)
    def _():
        o_ref[...] = x_ref[...] + acc_ref[...]


def _ffn(x, g, wg, wu, wd):
    n, d = x.shape
    ff = wg.shape[1]
    tm = _tile(n, 1024)
    tf = _tile(ff, 256)
    return pl.pallas_call(
        _ffn_kernel,
        grid=(n // tm, ff // tf),
        in_specs=[pl.BlockSpec((tm, d), lambda i, f: (i, 0)), pl.BlockSpec((1, d), lambda i, f: (0, 0)),
                  pl.BlockSpec((d, tf), lambda i, f: (0, f)), pl.BlockSpec((d, tf), lambda i, f: (0, f)),
                  pl.BlockSpec((tf, d), lambda i, f: (f, 0))],
        out_specs=pl.BlockSpec((tm, d), lambda i, f: (i, 0)),
        out_shape=jax.ShapeDtypeStruct((n, d), F32),
        scratch_shapes=[pltpu.VMEM((tm, d), BF16), pltpu.VMEM((tm, d), F32)],
        compiler_params=_cparams(("parallel", "arbitrary")),
        name="ffn_dense",
    )(x, g, wg, wu, wd)


ROUTE_E1, ROUTE_E2, ROUTE_R1, ROUTE_R2, ROUTE_G1, ROUTE_G2 = range(6)


def _router_kernel(x_ref, g_ref, rhi_ref, rlo_ref, tri_ref, route_ref, cnt_ref, base_ref, *, n_experts):
    i = pl.program_id(0)
    lane = lax.broadcasted_iota(jnp.int32, (1, LANES), 1)

    @pl.when(i == 0)
    def _():
        base_ref[...] = jnp.zeros_like(base_ref)

    hf = _rms(x_ref[...]) * g_ref[...]
    h_hi = hf.astype(BF16)
    h_lo = (hf - h_hi.astype(F32)).astype(BF16)
    logits = _dot(h_hi, rhi_ref[...]) + _dot(h_lo, rhi_ref[...]) + _dot(h_hi, rlo_ref[...])
    logits = jnp.where(lane < n_experts, logits, -jnp.inf)
    m1 = jnp.max(logits, axis=-1, keepdims=True)
    i1 = jnp.min(jnp.where(logits == m1, lane, LANES), axis=-1, keepdims=True)
    rest = jnp.where(lane == i1, -jnp.inf, logits)
    m2 = jnp.max(rest, axis=-1, keepdims=True)
    i2 = jnp.min(jnp.where(rest == m2, lane, LANES), axis=-1, keepdims=True)
    e2 = jnp.exp(m2 - m1)
    g1 = 1.0 / (1.0 + e2)
    g2 = e2 / (1.0 + e2)
    oh1 = (lane == i1).astype(F32)
    oh2 = (lane == i2).astype(F32)
    both = oh1 + oh2
    before = _dot(tri_ref[...], both.astype(BF16)) + base_ref[...]
    r1 = jnp.sum(oh1 * before, axis=-1, keepdims=True)
    r2 = jnp.sum(oh2 * before, axis=-1, keepdims=True)
    rec = jnp.zeros(route_ref.shape, F32)
    for slot, val in ((ROUTE_E1, i1.astype(F32)), (ROUTE_E2, i2.astype(F32)), (ROUTE_R1, r1),
                      (ROUTE_R2, r2), (ROUTE_G1, g1), (ROUTE_G2, g2)):
        rec = jnp.where(lane == slot, val, rec)
    route_ref[...] = rec
    base_ref[...] += jnp.sum(both, axis=0, keepdims=True)
    cnt_ref[...] = base_ref[...]


def _router(x, g, router):
    n, d = x.shape
    ne = router.shape[1]
    tm = _tile(n, 1024)
    rpad = jnp.zeros((d, LANES), F32).at[:, :ne].set(router)
    rhi = rpad.astype(BF16)
    rlo = (rpad - rhi.astype(F32)).astype(BF16)
    const = lambda i: (0, 0)
    return pl.pallas_call(
        functools.partial(_router_kernel, n_experts=ne),
        grid=(n // tm,),
        in_specs=[pl.BlockSpec((tm, d), lambda i: (i, 0)), pl.BlockSpec((1, d), const),
                  pl.BlockSpec((d, LANES), const), pl.BlockSpec((d, LANES), const),
                  pl.BlockSpec((tm, tm), const)],
        out_specs=[pl.BlockSpec((tm, LANES), lambda i: (i, 0)), pl.BlockSpec((1, LANES), const)],
        out_shape=[jax.ShapeDtypeStruct((n, LANES), F32), jax.ShapeDtypeStruct((1, LANES), F32)],
        scratch_shapes=[pltpu.VMEM((1, LANES), F32)],
        compiler_params=_cparams(("arbitrary",)),
        name="moe_router",
    )(x, g, rhi, rlo, _tril_strict(tm))


def _row_copy(src_ref, src_row, dst_ref, dst_row, sem):
    return pltpu.make_async_copy(src_ref.at[pl.ds(src_row, 1), :], dst_ref.at[pl.ds(dst_row, 1), :], sem)


def _dispatch_kernel(p1_ref, p2_ref, x_ref, xs_in_ref, xs_ref, sem, *, tb):
    del xs_in_ref

    def issue(t, carry):
        _row_copy(x_ref, t, xs_ref, p1_ref[0, 0, t], sem).start()
        _row_copy(x_ref, t, xs_ref, p2_ref[0, 0, t], sem).start()
        return carry

    lax.fori_loop(0, tb, issue, 0, unroll=8)
    for _ in range(TOP_K):
        pltpu.make_async_copy(x_ref, xs_ref.at[pl.ds(0, tb), :], sem).wait()


def _dispatch(x, pos1, pos2, n_rows):
    n, d = x.shape
    tb = pos1.shape[-1]
    smem = lambda: pl.BlockSpec((1, 1, tb), lambda i: (i, 0, 0), memory_space=pltpu.SMEM)
    return pl.pallas_call(
        functools.partial(_dispatch_kernel, tb=tb),
        grid=(n // tb,),
        in_specs=[smem(), smem(), pl.BlockSpec((tb, d), lambda i: (i, 0)), pl.BlockSpec(memory_space=pl.ANY)],
        out_specs=pl.BlockSpec(memory_space=pl.ANY),
        out_shape=jax.ShapeDtypeStruct((n_rows, d), F32),
        scratch_shapes=[pltpu.SemaphoreType.DMA(())],
        input_output_aliases={3: 0},
        compiler_params=_cparams(("arbitrary",)),
        name="moe_dispatch",
    )(pos1, pos2, x, jnp.zeros((n_rows, d), F32))


def _experts_kernel(te_ref, nv_ref, x_ref, g_ref, wg_ref, wu_ref, wd_ref, o_ref, h_ref, acc_ref):
    del te_ref
    i = pl.program_id(0)
    f = pl.program_id(1)
    valid = i < nv_ref[0]

    @pl.when(valid & (f == 0))
    def _():
        h_ref[...] = (_rms(x_ref[...]) * g_ref[...]).astype(BF16)
        acc_ref[...] = jnp.zeros_like(acc_ref)

    @pl.when(valid)
    def _():
        h = h_ref[...]
        gate = _dot(h, wg_ref[...])
        up = _dot(h, wu_ref[...])
        acc_ref[...] += _dot(((gate * _sigmoid(gate)) * up).astype(BF16), wd_ref[...])

    @pl.when(valid & (f == pl.num_programs(1) - 1))
    def _():
        o_ref[...] = acc_ref[...]

    @pl.when(jnp.logical_not(valid) & (f == pl.num_programs(1) - 1))
    def _():
        o_ref[...] = jnp.zeros_like(o_ref)


def _experts(xs, g, wg, wu, wd, tile_expert, n_valid, tr):
    n_rows, d = xs.shape
    ff = wg.shape[2]
    tf = _tile(ff, 256)
    nf = ff // tf
    rowmap = lambda i, f, te, nv: (jnp.minimum(i, nv[0] - 1), 0)
    fsel = lambda i, f, nv: jnp.where(i < nv[0], f, nf - 1)
    grid_spec = pltpu.PrefetchScalarGridSpec(
        num_scalar_prefetch=2,
        grid=(n_rows // tr, nf),
        in_specs=[pl.BlockSpec((tr, d), rowmap), pl.BlockSpec((1, d), lambda i, f, te, nv: (0, 0)),
                  pl.BlockSpec((None, d, tf), lambda i, f, te, nv: (te[i], 0, fsel(i, f, nv))),
                  pl.BlockSpec((None, d, tf), lambda i, f, te, nv: (te[i], 0, fsel(i, f, nv))),
                  pl.BlockSpec((None, tf, d), lambda i, f, te, nv: (te[i], fsel(i, f, nv), 0))],
        out_specs=pl.BlockSpec((tr, d), lambda i, f, te, nv: (i, 0)),
        scratch_shapes=[pltpu.VMEM((tr, d), BF16), pltpu.VMEM((tr, d), F32)],
    )
    return pl.pallas_call(
        _experts_kernel,
        grid_spec=grid_spec,
        out_shape=jax.ShapeDtypeStruct((n_rows, d), F32),
        compiler_params=_cparams(("arbitrary", "arbitrary")),
        name="moe_experts",
    )(tile_expert, n_valid, xs, g, wg, wu, wd)


def _combine_kernel(p1_ref, p2_ref, x_ref, route_ref, ys_ref, o_ref, buf_ref, sem, *, tc):
    def issue(t, carry):
        _row_copy(ys_ref, p1_ref[0, 0, t], buf_ref.at[0], t, sem).start()
        _row_copy(ys_ref, p2_ref[0, 0, t], buf_ref.at[1], t, sem).start()
        return carry

    lax.fori_loop(0, tc, issue, 0, unroll=8)
    for slot in range(TOP_K):
        pltpu.make_async_copy(ys_ref.at[pl.ds(0, tc), :], buf_ref.at[slot], sem).wait()
    lane = lax.broadcasted_iota(jnp.int32, (1, LANES), 1)
    route = route_ref[...]
    g1 = jnp.sum(jnp.where(lane == ROUTE_G1, route, 0.0), axis=-1, keepdims=True)
    g2 = jnp.sum(jnp.where(lane == ROUTE_G2, route, 0.0), axis=-1, keepdims=True)
    o_ref[...] = x_ref[...] + (g1 * buf_ref[0] + g2 * buf_ref[1])


def _combine(x, route, ys, pos1, pos2):
    n, d = x.shape
    tc = pos1.shape[-1]
    smem = lambda: pl.BlockSpec((1, 1, tc), lambda i: (i, 0, 0), memory_space=pltpu.SMEM)
    return pl.pallas_call(
        functools.partial(_combine_kernel, tc=tc),
        grid=(n // tc,),
        in_specs=[smem(), smem(), pl.BlockSpec((tc, d), lambda i: (i, 0)),
                  pl.BlockSpec((tc, LANES), lambda i: (i, 0)), pl.BlockSpec(memory_space=pl.ANY)],
        out_specs=pl.BlockSpec((tc, d), lambda i: (i, 0)),
        out_shape=jax.ShapeDtypeStruct((n, d), F32),
        scratch_shapes=[pltpu.VMEM((TOP_K, tc, d), F32), pltpu.SemaphoreType.DMA(())],
        compiler_params=_cparams(("arbitrary",)),
        name="moe_combine",
    )(pos1, pos2, x, route, ys)


def _moe(x, g, router, wg, wu, wd):
    n, d = x.shape
    ne = wg.shape[0]
    tr = 1024 if n >= 8192 else 256
    tb = _tile(n, 512)
    route, cnt = _router(x, g, router)
    counts = cnt[0, :ne].astype(jnp.int32)
    padded = ((counts + tr - 1) // tr) * tr
    ends = jnp.cumsum(padded)
    starts = ends - padded
    e1 = route[:, ROUTE_E1].astype(jnp.int32)
    e2 = route[:, ROUTE_E2].astype(jnp.int32)
    pos1 = (starts[e1] + route[:, ROUTE_R1].astype(jnp.int32)).reshape(n // tb, 1, tb)
    pos2 = (starts[e2] + route[:, ROUTE_R2].astype(jnp.int32)).reshape(n // tb, 1, tb)
    max_tiles = -(-(TOP_K * n + ne * (tr - 1)) // tr)
    n_valid = (ends[-1] // tr).reshape(1)
    tile_start = jnp.minimum(jnp.arange(max_tiles, dtype=jnp.int32), n_valid[0] - 1) * tr
    tile_expert = jnp.minimum(jnp.searchsorted(ends, tile_start, side="right"), ne - 1).astype(jnp.int32)
    xs = _dispatch(x, pos1, pos2, max_tiles * tr)
    ys = _experts(xs, g, wg, wu, wd, tile_expert, n_valid.astype(jnp.int32), tr)
    return _combine(x, route, ys, pos1, pos2)


def kernel(x_prompt, x_sample, cache_k, cache_v, state_hgrn, norm_mix, w_in, sb_q_gain, sb_k_gain,
           hg_lower_bounds, sb_out_gain, hg_out_gain, w_out, norm_ffn, ffn_w_gate, ffn_w_up, ffn_w_down,
           moe_router, moe_w_gate, moe_w_up, moe_w_down):
    depth = w_in.shape[0]
    bp, tp, d = x_prompt.shape
    bs, ts, _ = x_sample.shape
    past = cache_k.shape[2]
    sb_heads, hd = cache_k.shape[3], cache_k.shape[4]
    sw = sb_heads * hd
    dk = hg_out_gain.shape[1]
    hw = hg_lower_bounds.shape[1]
    hg_heads = hw // dk
    sb_scale = hd ** -0.5

    lbs = jnp.cumsum(jax.nn.softmax(hg_lower_bounds.astype(F32), axis=0), axis=0)
    lbs = lbs - lbs[0:1]
    head_mean = jnp.asarray(np.kron(np.eye(sb_heads), np.full((hd, hd), 1.0 / hd)), BF16)

    xp = x_prompt.reshape(bp * tp, d)
    xs = x_sample.reshape(bs * ts, d)
    zeros_state = jnp.zeros((bp, hg_heads, dk, dk), F32)
    outs = {k: [] for k in ("sp", "ss")}
    cache_kt = jnp.transpose(cache_k, (0, 1, 3, 4, 2)).reshape(depth, bs, sw, past)
    cache_vt = jnp.transpose(cache_v, (0, 1, 3, 4, 2)).reshape(depth, bs, sw, past)

    def mixer(x, b, t, l, kv_prev, *, k_past, v_past, past_len, s0, bq, bkp):
        qg = (jnp.tile(sb_q_gain[l], sb_heads) * sb_scale)[None, :]
        kg = jnp.tile(sb_k_gain[l], sb_heads)[None, :]
        q, kf, kb, vf, vb, qh, kh, lf, ih, gh = _inproj(
            x, norm_mix[l][None, :], w_in[l].astype(BF16), head_mean, qg, kg,
            jnp.log(lbs[l])[None, :], jnp.log1p(-lbs[l])[None, :], (1.0 - lbs[l])[None, :], kv_prev,
            sw=sw, hw=hw, seq_len=t)
        r3 = lambda a: a.reshape(b, t, a.shape[-1])
        kb3, vb3 = r3(kb), r3(vb)
        ya = _attention(r3(q), kb3, vb3, kb3 if k_past is None else k_past, vb3 if v_past is None else v_past,
                        jnp.tile(sb_out_gain[l], LANES // hd)[None, :],
                        bq=bq, bkp=bkp, past_len=past_len, hd=hd, past_transposed=k_past is not None)
        yb, s_fin = _hgrn(r3(qh), r3(kh), r3(lf), r3(ih), r3(gh), s0, hg_out_gain[l][None, :], dk=dk)
        wo = w_out[l].astype(BF16)
        x1 = _outproj(x, ya.reshape(b * t, sw), yb.reshape(b * t, hw), wo[:sw], wo[sw:])
        return x1, (kf, vf), s_fin

    def channel(x, l):
        j = l // 2
        if l % 2 == 0:
            return _ffn(x, norm_ffn[l][None, :], ffn_w_gate[j].astype(BF16), ffn_w_up[j].astype(BF16),
                        ffn_w_down[j].astype(BF16))
        return _moe(x, norm_ffn[l][None, :], moe_router[j], moe_w_gate[j].astype(BF16),
                    moe_w_up[j].astype(BF16), moe_w_down[j].astype(BF16))

    bq_p = _tile(tp, 512)
    kv_p = kv_s = None
    for l in range(depth):
        xp, kv_p, s_p = mixer(xp, bp, tp, l, kv_p, k_past=None, v_past=None, past_len=None, s0=zeros_state,
                              bq=bq_p, bkp=bq_p)
        outs["sp"].append(s_p)
        xs, kv_s, s_s = mixer(xs, bs, ts, l, kv_s, k_past=cache_kt[l], v_past=cache_vt[l],
                              past_len=past, s0=state_hgrn[l].astype(F32), bq=ts, bkp=_tile(past, 512))
        outs["ss"].append(s_s)
        xp = channel(xp, l)
        xs = channel(xs, l)

    heads5 = lambda a, b, t: jnp.transpose(a.reshape(depth, b, sb_heads, hd, t), (0, 1, 4, 2, 3))
    return (xp.reshape(bp, tp, d), xs.reshape(bs, ts, d), heads5(kv_p[0], bp, tp), heads5(kv_p[1], bp, tp),
            jnp.stack(outs["sp"]), heads5(kv_s[0], bs, ts), heads5(kv_s[1], bs, ts), jnp.stack(outs["ss"]))
```

```python
import functools

import numpy as np
import jax
import jax.numpy as jnp
from jax import lax
from jax.experimental import pallas as pl
from jax.experimental.pallas import tpu as pltpu

EPS = 1e-6
TOP_K = 2
LANES = 128
F32 = jnp.float32
BF16 = jnp.bfloat16
VMEM_LIMIT_BYTES = 56 * 1024 * 1024


def _cparams(semantics):
    return pltpu.CompilerParams(dimension_semantics=semantics, vmem_limit_bytes=VMEM_LIMIT_BYTES)


def _dot(a, b):
    return jnp.dot(a, b, preferred_element_type=F32)


def _dot_nt(a, b):
    return lax.dot_general(a, b, (((1,), (1,)), ((), ())), preferred_element_type=F32)


def _dot_tn(a, b):
    return lax.dot_general(a, b, (((0,), (0,)), ((), ())), preferred_element_type=F32)


def _sigmoid(x):
    return 1.0 / (1.0 + jnp.exp(-x))


def _rms(x):
    return x * lax.rsqrt(jnp.mean(x * x, axis=-1, keepdims=True) + EPS)


def _tile(n, pref):
    t = min(n, pref)
    assert n % t == 0, (n, t)
    return t


def _inproj_kernel(*refs, sw, hw, n_prev, seq_per_tile):
    x_ref, g_ref, w_ref, hm_ref, qg_ref, kg_ref, llb_ref, l1m_ref, oml_ref = refs[:9]
    prev = refs[9:9 + (2 if n_prev else 0)]
    q_ref, kf_ref, kb_ref, vf_ref, vb_ref, qh_ref, kh_ref, lf_ref, ih_ref, gh_ref = refs[9 + len(prev):]
    if n_prev:
        kf_ref[:n_prev] = prev[0][...]
        vf_ref[:n_prev] = prev[1][...]

    def store_time_minor(ref, y):
        t = y.shape[0] // seq_per_tile
        for sq in range(seq_per_tile):
            ref[n_prev, sq] = y[sq * t:(sq + 1) * t, :].T

    h = (_rms(x_ref[...]) * g_ref[...]).astype(BF16)

    def seg(lo, width):
        return _dot(h, w_ref[:, lo:lo + width])

    def headnorm(y, gain):
        m = _dot((y * y).astype(BF16), hm_ref[...])
        return (y * lax.rsqrt(m + EPS)) * gain

    q_ref[...] = headnorm(seg(0, sw), qg_ref[...]).astype(BF16)
    ka = headnorm(seg(sw, sw), kg_ref[...])
    store_time_minor(kf_ref, ka)
    kb_ref[...] = ka.astype(BF16)
    va = seg(2 * sw, sw)
    store_time_minor(vf_ref, va)
    vb_ref[...] = va.astype(BF16)

    o = 3 * sw
    qb = seg(o, hw)
    qh_ref[...] = qb * _sigmoid(qb)

    fb = seg(o + hw, hw)
    e = jnp.exp(-jnp.abs(fb))
    log_sig = jnp.minimum(fb, 0.0) - jnp.log(1.0 + e)
    c = l1m_ref[...] + log_sig
    a = llb_ref[...]
    lf_ref[...] = jnp.maximum(a, c) + jnp.log(1.0 + jnp.exp(-jnp.abs(a - c)))
    kh_ref[...] = oml_ref[...] * (jnp.where(fb >= 0.0, e, 1.0) / (1.0 + e))

    ih_ref[...] = seg(o + 2 * hw, hw)
    gh_ref[...] = seg(o + 3 * hw, hw)


def _inproj(x, g, w, hm, qg, kg, llb, l1m, oml, kv_prev, *, sw, hw, seq_len):
    n, d = x.shape
    tm = _tile(n, 512)
    n_prev = 0 if kv_prev is None else kv_prev[0].shape[0]
    spt = max(tm // seq_len, 1)
    tt = tm // spt
    nt = seq_len // tt
    assert spt * tt == tm and nt * tt == seq_len
    row = lambda i: (i, 0)
    const = lambda i: (0, 0)
    stacked = lambda i: (0, i // nt, 0, i % nt)
    kv_shape = (n_prev + 1, n // seq_len, sw, seq_len)
    specs = {"row_sw": pl.BlockSpec((tm, sw), row), "row_hw": pl.BlockSpec((tm, hw), row),
             "stack": pl.BlockSpec((n_prev + 1, spt, sw, tt), stacked)}
    outs = [("row_sw", (n, sw), BF16), ("stack", kv_shape, F32), ("row_sw", (n, sw), BF16),
            ("stack", kv_shape, F32), ("row_sw", (n, sw), BF16)] + [("row_hw", (n, hw), F32)] * 5
    prev_specs = [pl.BlockSpec((n_prev, spt, sw, tt), stacked)] * 2 if n_prev else []
    return pl.pallas_call(
        functools.partial(_inproj_kernel, sw=sw, hw=hw, n_prev=n_prev, seq_per_tile=spt),
        grid=(n // tm,),
        in_specs=[pl.BlockSpec((tm, d), row), pl.BlockSpec((1, d), const),
                  pl.BlockSpec(w.shape, const), pl.BlockSpec(hm.shape, const),
                  pl.BlockSpec((1, sw), const), pl.BlockSpec((1, sw), const),
                  pl.BlockSpec((1, hw), const), pl.BlockSpec((1, hw), const), pl.BlockSpec((1, hw), const)]
        + prev_specs,
        out_specs=[specs[kind] for kind, _, _ in outs],
        out_shape=[jax.ShapeDtypeStruct(shape, dt) for _, shape, dt in outs],
        compiler_params=_cparams(("parallel",)),
        name="inproj",
    )(x, g, w, hm, qg, kg, llb, l1m, oml, *(kv_prev or ()))


KEY_SUB = 256
SOFTPLUS_CLAMP = 80.0
CARRY_DEAD = 104.0


def _attn_kernel(q_ref, kn_ref, vn_ref, kp_hbm, vp_hbm, un_ref, up_ref, gain_ref,
                 y_ref, qm_ref, acc_ref, carry_ref, kbuf, vbuf, sem, alive_ref,
                 *, bq, subn, subp, hd, past_len, past_layer, past_transposed):
    bb = pl.program_id(0)
    p = pl.program_id(1)
    qi = pl.program_id(2)
    lane = lax.broadcasted_iota(jnp.int32, (1, LANES), 1)
    heads = LANES // hd
    n_past = (qi * bq if past_len is None else past_len) // subp

    def fetch(j, slot):
        if past_transposed:
            window = (past_layer, bb, pl.ds(p * LANES, LANES), pl.ds(j * subp, subp))
        else:
            window = (bb, pl.ds(j * subp, subp), pl.ds(p * LANES, LANES))
        return (pltpu.make_async_copy(kp_hbm.at[window], kbuf.at[slot], sem.at[0, slot]),
                pltpu.make_async_copy(vp_hbm.at[window], vbuf.at[slot], sem.at[1, slot]))

    def slot_of(j):
        return (n_past - 1 - j) & 1

    @pl.when(n_past > 0)
    def _():
        for cp in fetch(n_past - 1, 0):
            cp.start()

    q = q_ref[...]
    for hh in range(heads):
        in_head = (lane >= hh * hd) & (lane < (hh + 1) * hd)
        qm_ref[hh] = jnp.where(in_head, q, jnp.zeros_like(q))
    acc_ref[...] = jnp.zeros_like(acc_ref)
    carry_ref[...] = jnp.zeros_like(carry_ref)

    def one_head(hh, k, v, u, r0, mask, transposed):
        if transposed:
            z = _dot(qm_ref[hh, r0:, :], k)
        else:
            z = _dot_nt(qm_ref[hh, r0:, :], k)
        sp = jnp.maximum(jnp.log(1.0 + jnp.exp(jnp.minimum(z, SOFTPLUS_CLAMP))), z)
        if mask is not None:
            sp = jnp.where(mask, sp, 0.0)
        after = _dot(sp.astype(BF16), u)
        w = jnp.exp((z - sp) - after)
        if mask is not None:
            w = jnp.where(mask, w, 0.0)
        pv = _dot_nt(w.astype(BF16), v) if transposed else _dot(w.astype(BF16), v)
        c = carry_ref[hh, r0:, :]
        acc_ref[hh, r0:, :] += jnp.exp(-c) * pv
        c_new = c + jnp.sum(sp, axis=-1, keepdims=True)
        carry_ref[hh, r0:, :] = c_new
        if r0 == 0:
            alive_ref[hh] = (jnp.min(c_new) < CARRY_DEAD).astype(jnp.int32)

    u_new = un_ref[...]
    for j in range(bq // subn - 1, -1, -1):
        r0 = j * subn
        rows = r0 + lax.broadcasted_iota(jnp.int32, (bq - r0, subn), 0)
        cols = r0 + lax.broadcasted_iota(jnp.int32, (bq - r0, subn), 1)
        k = kn_ref[r0:r0 + subn, :].astype(BF16)
        v = vn_ref[r0:r0 + subn, :].astype(BF16)
        for hh in range(heads):
            one_head(hh, k, v, u_new, r0, cols < rows, False)

    def any_alive():
        total = alive_ref[0]
        for hh in range(1, heads):
            total += alive_ref[hh]
        return total > 0

    def past_block(j):
        slot = slot_of(j)
        for cp in fetch(j, slot):
            cp.wait()

        @pl.when(j > 0)
        def _():
            for cp in fetch(j - 1, 1 - slot):
                cp.start()

        u_past = up_ref[...]
        for hh in range(heads):
            @pl.when(alive_ref[hh] == 1)
            def _(hh=hh):
                one_head(hh, kbuf[slot].astype(BF16), vbuf[slot].astype(BF16), u_past, 0, None, past_transposed)
        return j - 1

    j_end = lax.while_loop(lambda j: (j >= 0) & any_alive(), past_block, n_past - 1)

    @pl.when(j_end >= 0)
    def _():
        for cp in fetch(j_end, slot_of(j_end)):
            cp.wait()

    o = acc_ref[0]
    for hh in range(1, heads):
        in_head = (lane >= hh * hd) & (lane < (hh + 1) * hd)
        o = jnp.where(in_head, acc_ref[hh], o)
    o2 = o * o
    ms = jnp.zeros_like(o)
    for hh in range(heads):
        in_head = (lane >= hh * hd) & (lane < (hh + 1) * hd)
        ssum = jnp.sum(jnp.where(in_head, o2, 0.0), axis=-1, keepdims=True)
        ms = jnp.where(in_head, ssum * (1.0 / hd), ms)
    y_ref[...] = ((o * lax.rsqrt(ms + EPS)) * gain_ref[...]).astype(y_ref.dtype)


def _tril_strict(n):
    return jnp.asarray(np.tril(np.ones((n, n), np.float32), -1), BF16)


def _attention(q, k_new, v_new, k_past, v_past, gain, *, bq, past_len, hd, past_layer=None):
    b, tq, w = q.shape
    past_transposed = past_layer is not None
    subn = min(bq, KEY_SUB)
    subp = KEY_SUB
    assert bq % subn == 0 and (bq if past_len is None else past_len) % subp == 0
    heads = LANES // hd
    qmap = lambda bb, p, qi: (bb, qi, p)
    cmap = lambda bb, p, qi: (0, 0)
    past_buf = (2, LANES, subp) if past_transposed else (2, subp, LANES)
    return pl.pallas_call(
        functools.partial(_attn_kernel, bq=bq, subn=subn, subp=subp, hd=hd, past_len=past_len,
                          past_layer=past_layer, past_transposed=past_transposed),
        grid=(b, w // LANES, tq // bq),
        in_specs=[pl.BlockSpec((None, bq, LANES), qmap),
                  pl.BlockSpec((None, bq, LANES), qmap), pl.BlockSpec((None, bq, LANES), qmap),
                  pl.BlockSpec(memory_space=pl.ANY), pl.BlockSpec(memory_space=pl.ANY),
                  pl.BlockSpec((subn, subn), cmap), pl.BlockSpec((subp, subp), cmap),
                  pl.BlockSpec((1, LANES), cmap)],
        out_specs=pl.BlockSpec((None, bq, LANES), qmap),
        out_shape=jax.ShapeDtypeStruct((b, tq, w), BF16),
        scratch_shapes=[pltpu.VMEM((heads, bq, LANES), BF16), pltpu.VMEM((heads, bq, LANES), F32),
                        pltpu.VMEM((heads, bq, LANES), F32),
                        pltpu.VMEM(past_buf, k_past.dtype), pltpu.VMEM(past_buf, v_past.dtype),
                        pltpu.SemaphoreType.DMA((2, 2)), pltpu.SMEM((heads,), jnp.int32)],
        compiler_params=_cparams(("parallel", "parallel", "arbitrary")),
        name="stickbreak_attn",
    )(q, k_new, v_new, k_past, v_past, _tril_strict(subn), _tril_strict(subp), gain)


def _hgrn_halvings(c):
    return [c >> (i + 1) for i in range(int(np.log2(c)))]


def _hgrn_masks(c):
    t = np.arange(c)
    masks = [np.eye(c, dtype=bool)]
    for h in _hgrn_halvings(c):
        blk = t // (2 * h)
        second = (t // h) % 2 == 1
        masks.append((blk[:, None] == blk[None, :]) & second[:, None] & (~second)[None, :])
    return jnp.asarray(np.stack(masks).astype(np.float32))


def _boundary_rows(b, h, row):
    c, dk = b.shape
    if 2 * h >= 8:
        n = c // (2 * h)
        ref = b.reshape(n, 2 * h, dk)[:, h - 1:h, :]
        return jnp.broadcast_to(ref, (n, 2 * h, dk)).reshape(c, dk)
    down1 = pltpu.roll(b, 1, 0)
    if h == 1:
        return jnp.where((row & 1) == 1, down1, b)
    m = row & 3
    up1 = pltpu.roll(b, c - 1, 0)
    down2 = pltpu.roll(b, 2, 0)
    return jnp.where(m == 0, up1, jnp.where(m == 1, b, jnp.where(m == 2, down1, down2)))


def _hgrn_kernel(q_ref, k_ref, lf_ref, v_ref, g_ref, s0_ref, tri_ref, msk_ref, gain_ref,
                 y_ref, sout_ref, st_ref, *, c, n_chunks):
    t = pl.program_id(2)
    dk = q_ref.shape[-1]

    @pl.when(t == 0)
    def _():
        st_ref[...] = s0_ref[...].T

    row = lax.broadcasted_iota(jnp.int32, (c, dk), 0)
    tri = tri_ref[...]
    for ci in range(n_chunks):
        sl = slice(ci * c, (ci + 1) * c)
        q = q_ref[sl, :]
        k = k_ref[sl, :]
        lf = lf_ref[sl, :]
        v = v_ref[sl, :].astype(BF16)
        g = g_ref[sl, :]
        hi = lf.astype(BF16)
        r1 = lf - hi.astype(F32)
        mid = r1.astype(BF16)
        lo = (r1 - mid.astype(F32)).astype(BF16)
        b = _dot(tri, hi) + _dot(tri, mid) + _dot(tri, lo)
        st = st_ref[...]
        o = _dot_nt((q * jnp.exp(b)).astype(BF16), st.astype(BF16))
        a = jnp.where(msk_ref[0] > 0.0, _dot_nt(q.astype(BF16), k.astype(BF16)), 0.0)
        for lv, h in enumerate(_hgrn_halvings(c)):
            gap = b - _boundary_rows(b, h, row)
            el = jnp.exp(jnp.where((row & h) != 0, gap, -gap))
            a += jnp.where(msk_ref[1 + lv] > 0.0, _dot_nt((q * el).astype(BF16), (k * el).astype(BF16)), 0.0)
        o += _dot(a.astype(BF16), v)
        b_last = b[c - 1:c, :]
        k_end = (k * jnp.exp(b_last - b)).astype(BF16)
        st_ref[...] = st * jnp.exp(b_last) + _dot_tn(v, k_end)
        y = (_rms(o) * gain_ref[...]) * (g * _sigmoid(g))
        y_ref[sl, :] = y.astype(y_ref.dtype)

    @pl.when(t == pl.num_programs(2) - 1)
    def _():
        sout_ref[...] = st_ref[...].T


def _hgrn(q, k, lf, v, g, s0, gain, *, dk):
    b, t, hw = q.shape
    nh = hw // dk
    c = _tile(t, 256)
    tt = _tile(t, 512)
    msk = _hgrn_masks(c)
    tri = jnp.asarray(np.tril(np.ones((c, c), np.float32)), BF16)
    tok = lambda bb, h, ti: (bb, ti, h)
    smap = lambda bb, h, ti: (bb, h, 0, 0)
    return pl.pallas_call(
        functools.partial(_hgrn_kernel, c=c, n_chunks=tt // c),
        grid=(b, nh, t // tt),
        in_specs=[pl.BlockSpec((None, tt, dk), tok)] * 5
        + [pl.BlockSpec((None, None, dk, dk), smap),
           pl.BlockSpec((c, c), lambda bb, h, ti: (0, 0)),
           pl.BlockSpec(msk.shape, lambda bb, h, ti: (0, 0, 0)),
           pl.BlockSpec((1, dk), lambda bb, h, ti: (0, 0))],
        out_specs=[pl.BlockSpec((None, tt, dk), tok), pl.BlockSpec((None, None, dk, dk), smap)],
        out_shape=[jax.ShapeDtypeStruct((b, t, hw), BF16), jax.ShapeDtypeStruct((b, nh, dk, dk), F32)],
        scratch_shapes=[pltpu.VMEM((dk, dk), F32)],
        compiler_params=_cparams(("parallel", "parallel", "arbitrary")),
        name="hgrn2",
    )(q, k, lf, v, g, s0, tri, msk, gain)


def _outproj_kernel(x_ref, ya_ref, yb_ref, wa_ref, wb_ref, o_ref):
    o_ref[...] = x_ref[...] + _dot(ya_ref[...], wa_ref[...]) + _dot(yb_ref[...], wb_ref[...])


def _outproj(x, ya, yb, wa, wb):
    n, d = x.shape
    tm = _tile(n, 512)
    row = lambda i: (i, 0)
    const = lambda i: (0, 0)
    return pl.pallas_call(
        _outproj_kernel,
        grid=(n // tm,),
        in_specs=[pl.BlockSpec((tm, d), row), pl.BlockSpec((tm, ya.shape[1]), row),
                  pl.BlockSpec((tm, yb.shape[1]), row), pl.BlockSpec(wa.shape, const),
                  pl.BlockSpec(wb.shape, const)],
        out_specs=pl.BlockSpec((tm, d), row),
        out_shape=jax.ShapeDtypeStruct((n, d), F32),
        compiler_params=_cparams(("parallel",)),
        name="outproj",
    )(x, ya, yb, wa, wb)


def _ffn_kernel(x_ref, g_ref, wg_ref, wu_ref, wd_ref, o_ref, h_ref, acc_ref):
    f = pl.program_id(1)

    @pl.when(f == 0)
    def _():
        h_ref[...] = (_rms(x_ref[...]) * g_ref[...]).astype(BF16)
        acc_ref[...] = jnp.zeros_like(acc_ref)

    h = h_ref[...]
    gate = _dot(h, wg_ref[...])
    up = _dot(h, wu_ref[...])
    acc_ref[...] += _dot(((gate * _sigmoid(gate)) * up).astype(BF16), wd_ref[...])

    @pl.when(f == pl.num_programs(1) - 1)
    def _():
        o_ref[...] = x_ref[...] + acc_ref[...]


def _ffn(x, g, wg, wu, wd):
    n, d = x.shape
    ff = wg.shape[1]
    tm = _tile(n, 1024)
    tf = _tile(ff, 256)
    return pl.pallas_call(
        _ffn_kernel,
        grid=(n // tm, ff // tf),
        in_specs=[pl.BlockSpec((tm, d), lambda i, f: (i, 0)), pl.BlockSpec((1, d), lambda i, f: (0, 0)),
                  pl.BlockSpec((d, tf), lambda i, f: (0, f)), pl.BlockSpec((d, tf), lambda i, f: (0, f)),
                  pl.BlockSpec((tf, d), lambda i, f: (f, 0))],
        out_specs=pl.BlockSpec((tm, d), lambda i, f: (i, 0)),
        out_shape=jax.ShapeDtypeStruct((n, d), F32),
        scratch_shapes=[pltpu.VMEM((tm, d), BF16), pltpu.VMEM((tm, d), F32)],
        compiler_params=_cparams(("parallel", "arbitrary")),
        name="ffn_dense",
    )(x, g, wg, wu, wd)


ROUTE_E1, ROUTE_E2, ROUTE_R1, ROUTE_R2, ROUTE_G1, ROUTE_G2 = range(6)


def _router_kernel(x_ref, g_ref, rhi_ref, rlo_ref, tri_ref, route_ref, cnt_ref, base_ref, *, n_experts):
    i = pl.program_id(0)
    lane = lax.broadcasted_iota(jnp.int32, (1, LANES), 1)

    @pl.when(i == 0)
    def _():
        base_ref[...] = jnp.zeros_like(base_ref)

    hf = _rms(x_ref[...]) * g_ref[...]
    h_hi = hf.astype(BF16)
    h_lo = (hf - h_hi.astype(F32)).astype(BF16)
    logits = _dot(h_hi, rhi_ref[...]) + _dot(h_lo, rhi_ref[...]) + _dot(h_hi, rlo_ref[...])
    logits = jnp.where(lane < n_experts, logits, -jnp.inf)
    m1 = jnp.max(logits, axis=-1, keepdims=True)
    i1 = jnp.min(jnp.where(logits == m1, lane, LANES), axis=-1, keepdims=True)
    rest = jnp.where(lane == i1, -jnp.inf, logits)
    m2 = jnp.max(rest, axis=-1, keepdims=True)
    i2 = jnp.min(jnp.where(rest == m2, lane, LANES), axis=-1, keepdims=True)
    e2 = jnp.exp(m2 - m1)
    g1 = 1.0 / (1.0 + e2)
    g2 = e2 / (1.0 + e2)
    oh1 = (lane == i1).astype(F32)
    oh2 = (lane == i2).astype(F32)
    both = oh1 + oh2
    before = _dot(tri_ref[...], both.astype(BF16)) + base_ref[...]
    r1 = jnp.sum(oh1 * before, axis=-1, keepdims=True)
    r2 = jnp.sum(oh2 * before, axis=-1, keepdims=True)
    rec = jnp.zeros(route_ref.shape, F32)
    for slot, val in ((ROUTE_E1, i1.astype(F32)), (ROUTE_E2, i2.astype(F32)), (ROUTE_R1, r1),
                      (ROUTE_R2, r2), (ROUTE_G1, g1), (ROUTE_G2, g2)):
        rec = jnp.where(lane == slot, val, rec)
    route_ref[...] = rec
    base_ref[...] += jnp.sum(both, axis=0, keepdims=True)
    cnt_ref[...] = base_ref[...]


def _router(x, g, router):
    n, d = x.shape
    ne = router.shape[1]
    tm = _tile(n, 1024)
    rpad = jnp.zeros((d, LANES), F32).at[:, :ne].set(router)
    rhi = rpad.astype(BF16)
    rlo = (rpad - rhi.astype(F32)).astype(BF16)
    const = lambda i: (0, 0)
    return pl.pallas_call(
        functools.partial(_router_kernel, n_experts=ne),
        grid=(n // tm,),
        in_specs=[pl.BlockSpec((tm, d), lambda i: (i, 0)), pl.BlockSpec((1, d), const),
                  pl.BlockSpec((d, LANES), const), pl.BlockSpec((d, LANES), const),
                  pl.BlockSpec((tm, tm), const)],
        out_specs=[pl.BlockSpec((tm, LANES), lambda i: (i, 0)), pl.BlockSpec((1, LANES), const)],
        out_shape=[jax.ShapeDtypeStruct((n, LANES), F32), jax.ShapeDtypeStruct((1, LANES), F32)],
        scratch_shapes=[pltpu.VMEM((1, LANES), F32)],
        compiler_params=_cparams(("arbitrary",)),
        name="moe_router",
    )(x, g, rhi, rlo, _tril_strict(tm))


def _row_copy(src_ref, src_row, dst_ref, dst_row, sem):
    return pltpu.make_async_copy(src_ref.at[pl.ds(src_row, 1), :], dst_ref.at[pl.ds(dst_row, 1), :], sem)


def _dispatch_kernel(p1_ref, p2_ref, x_ref, xs_in_ref, xs_ref, sem, *, tb):
    del xs_in_ref

    def issue(t, carry):
        _row_copy(x_ref, t, xs_ref, p1_ref[0, 0, t], sem).start()
        _row_copy(x_ref, t, xs_ref, p2_ref[0, 0, t], sem).start()
        return carry

    lax.fori_loop(0, tb, issue, 0, unroll=8)
    for _ in range(TOP_K):
        pltpu.make_async_copy(x_ref, xs_ref.at[pl.ds(0, tb), :], sem).wait()


def _dispatch(x, pos1, pos2, n_rows):
    n, d = x.shape
    tb = pos1.shape[-1]
    smem = lambda: pl.BlockSpec((1, 1, tb), lambda i: (i, 0, 0), memory_space=pltpu.SMEM)
    return pl.pallas_call(
        functools.partial(_dispatch_kernel, tb=tb),
        grid=(n // tb,),
        in_specs=[smem(), smem(), pl.BlockSpec((tb, d), lambda i: (i, 0)), pl.BlockSpec(memory_space=pl.ANY)],
        out_specs=pl.BlockSpec(memory_space=pl.ANY),
        out_shape=jax.ShapeDtypeStruct((n_rows, d), F32),
        scratch_shapes=[pltpu.SemaphoreType.DMA(())],
        input_output_aliases={3: 0},
        compiler_params=_cparams(("arbitrary",)),
        name="moe_dispatch",
    )(pos1, pos2, x, jnp.zeros((n_rows, d), F32))


def _experts_kernel(te_ref, nv_ref, x_ref, g_ref, wg_ref, wu_ref, wd_ref, o_ref, h_ref, acc_ref):
    del te_ref
    i = pl.program_id(0)
    f = pl.program_id(1)
    valid = i < nv_ref[0]

    @pl.when(valid & (f == 0))
    def _():
        h_ref[...] = (_rms(x_ref[...]) * g_ref[...]).astype(BF16)
        acc_ref[...] = jnp.zeros_like(acc_ref)

    @pl.when(valid)
    def _():
        h = h_ref[...]
        gate = _dot(h, wg_ref[...])
        up = _dot(h, wu_ref[...])
        acc_ref[...] += _dot(((gate * _sigmoid(gate)) * up).astype(BF16), wd_ref[...])

    @pl.when(valid & (f == pl.num_programs(1) - 1))
    def _():
        o_ref[...] = acc_ref[...]

    @pl.when(jnp.logical_not(valid) & (f == pl.num_programs(1) - 1))
    def _():
        o_ref[...] = jnp.zeros_like(o_ref)


def _experts(xs, g, wg, wu, wd, tile_expert, n_valid, tr):
    n_rows, d = xs.shape
    ff = wg.shape[2]
    tf = _tile(ff, 256)
    nf = ff // tf
    rowmap = lambda i, f, te, nv: (jnp.minimum(i, nv[0] - 1), 0)
    fsel = lambda i, f, nv: jnp.where(i < nv[0], f, nf - 1)
    grid_spec = pltpu.PrefetchScalarGridSpec(
        num_scalar_prefetch=2,
        grid=(n_rows // tr, nf),
        in_specs=[pl.BlockSpec((tr, d), rowmap), pl.BlockSpec((1, d), lambda i, f, te, nv: (0, 0)),
                  pl.BlockSpec((None, d, tf), lambda i, f, te, nv: (te[i], 0, fsel(i, f, nv))),
                  pl.BlockSpec((None, d, tf), lambda i, f, te, nv: (te[i], 0, fsel(i, f, nv))),
                  pl.BlockSpec((None, tf, d), lambda i, f, te, nv: (te[i], fsel(i, f, nv), 0))],
        out_specs=pl.BlockSpec((tr, d), lambda i, f, te, nv: (i, 0)),
        scratch_shapes=[pltpu.VMEM((tr, d), BF16), pltpu.VMEM((tr, d), F32)],
    )
    return pl.pallas_call(
        _experts_kernel,
        grid_spec=grid_spec,
        out_shape=jax.ShapeDtypeStruct((n_rows, d), F32),
        compiler_params=_cparams(("arbitrary", "arbitrary")),
        name="moe_experts",
    )(tile_expert, n_valid, xs, g, wg, wu, wd)


def _combine_kernel(p1_ref, p2_ref, x_ref, route_ref, ys_ref, o_ref, buf_ref, sem, *, tc):
    def issue(t, carry):
        _row_copy(ys_ref, p1_ref[0, 0, t], buf_ref.at[0], t, sem).start()
        _row_copy(ys_ref, p2_ref[0, 0, t], buf_ref.at[1], t, sem).start()
        return carry

    lax.fori_loop(0, tc, issue, 0, unroll=8)
    for slot in range(TOP_K):
        pltpu.make_async_copy(ys_ref.at[pl.ds(0, tc), :], buf_ref.at[slot], sem).wait()
    lane = lax.broadcasted_iota(jnp.int32, (1, LANES), 1)
    route = route_ref[...]
    g1 = jnp.sum(jnp.where(lane == ROUTE_G1, route, 0.0), axis=-1, keepdims=True)
    g2 = jnp.sum(jnp.where(lane == ROUTE_G2, route, 0.0), axis=-1, keepdims=True)
    o_ref[...] = x_ref[...] + (g1 * buf_ref[0] + g2 * buf_ref[1])


def _combine(x, route, ys, pos1, pos2):
    n, d = x.shape
    tc = pos1.shape[-1]
    smem = lambda: pl.BlockSpec((1, 1, tc), lambda i: (i, 0, 0), memory_space=pltpu.SMEM)
    return pl.pallas_call(
        functools.partial(_combine_kernel, tc=tc),
        grid=(n // tc,),
        in_specs=[smem(), smem(), pl.BlockSpec((tc, d), lambda i: (i, 0)),
                  pl.BlockSpec((tc, LANES), lambda i: (i, 0)), pl.BlockSpec(memory_space=pl.ANY)],
        out_specs=pl.BlockSpec((tc, d), lambda i: (i, 0)),
        out_shape=jax.ShapeDtypeStruct((n, d), F32),
        scratch_shapes=[pltpu.VMEM((TOP_K, tc, d), F32), pltpu.SemaphoreType.DMA(())],
        compiler_params=_cparams(("arbitrary",)),
        name="moe_combine",
    )(pos1, pos2, x, route, ys)


def _moe(x, g, router, wg, wu, wd):
    n, d = x.shape
    ne = wg.shape[0]
    tr = 1024 if n >= 8192 else 256
    tb = _tile(n, 512)
    route, cnt = _router(x, g, router)
    counts = cnt[0, :ne].astype(jnp.int32)
    padded = ((counts + tr - 1) // tr) * tr
    ends = jnp.cumsum(padded)
    starts = ends - padded
    e1 = route[:, ROUTE_E1].astype(jnp.int32)
    e2 = route[:, ROUTE_E2].astype(jnp.int32)
    pos1 = (starts[e1] + route[:, ROUTE_R1].astype(jnp.int32)).reshape(n // tb, 1, tb)
    pos2 = (starts[e2] + route[:, ROUTE_R2].astype(jnp.int32)).reshape(n // tb, 1, tb)
    max_tiles = -(-(TOP_K * n + ne * (tr - 1)) // tr)
    n_valid = (ends[-1] // tr).reshape(1)
    tile_start = jnp.minimum(jnp.arange(max_tiles, dtype=jnp.int32), n_valid[0] - 1) * tr
    tile_expert = jnp.minimum(jnp.searchsorted(ends, tile_start, side="right"), ne - 1).astype(jnp.int32)
    xs = _dispatch(x, pos1, pos2, max_tiles * tr)
    ys = _experts(xs, g, wg, wu, wd, tile_expert, n_valid.astype(jnp.int32), tr)
    return _combine(x, route, ys, pos1, pos2)


def kernel(x_prompt, x_sample, cache_k, cache_v, state_hgrn, norm_mix, w_in, sb_q_gain, sb_k_gain,
           hg_lower_bounds, sb_out_gain, hg_out_gain, w_out, norm_ffn, ffn_w_gate, ffn_w_up, ffn_w_down,
           moe_router, moe_w_gate, moe_w_up, moe_w_down):
    depth = w_in.shape[0]
    bp, tp, d = x_prompt.shape
    bs, ts, _ = x_sample.shape
    past = cache_k.shape[2]
    sb_heads, hd = cache_k.shape[3], cache_k.shape[4]
    sw = sb_heads * hd
    dk = hg_out_gain.shape[1]
    hw = hg_lower_bounds.shape[1]
    hg_heads = hw // dk
    sb_scale = hd ** -0.5

    lbs = jnp.cumsum(jax.nn.softmax(hg_lower_bounds.astype(F32), axis=0), axis=0)
    lbs = lbs - lbs[0:1]
    head_mean = jnp.asarray(np.kron(np.eye(sb_heads), np.full((hd, hd), 1.0 / hd)), BF16)

    xp = x_prompt.reshape(bp * tp, d)
    xs = x_sample.reshape(bs * ts, d)
    zeros_state = jnp.zeros((bp, hg_heads, dk, dk), F32)
    outs = {k: [] for k in ("sp", "ss")}
    cache_kt = jnp.transpose(cache_k, (0, 1, 3, 4, 2)).reshape(depth, bs, sw, past)
    cache_vt = jnp.transpose(cache_v, (0, 1, 3, 4, 2)).reshape(depth, bs, sw, past)

    def mixer(x, b, t, l, kv_prev, *, cached, past_len, s0, bq):
        qg = (jnp.tile(sb_q_gain[l], sb_heads) * sb_scale)[None, :]
        kg = jnp.tile(sb_k_gain[l], sb_heads)[None, :]
        q, kf, kb, vf, vb, qh, kh, lf, ih, gh = _inproj(
            x, norm_mix[l][None, :], w_in[l].astype(BF16), head_mean, qg, kg,
            jnp.log(lbs[l])[None, :], jnp.log1p(-lbs[l])[None, :], (1.0 - lbs[l])[None, :], kv_prev,
            sw=sw, hw=hw, seq_len=t)
        r3 = lambda a: a.reshape(b, t, a.shape[-1])
        kb3, vb3 = r3(kb), r3(vb)
        ya = _attention(r3(q), kb3, vb3, cache_kt if cached else kb3, cache_vt if cached else vb3,
                        jnp.tile(sb_out_gain[l], LANES // hd)[None, :],
                        bq=bq, past_len=past_len, hd=hd, past_layer=l if cached else None)
        yb, s_fin = _hgrn(r3(qh), r3(kh), r3(lf), r3(ih), r3(gh), s0, hg_out_gain[l][None, :], dk=dk)
        wo = w_out[l].astype(BF16)
        x1 = _outproj(x, ya.reshape(b * t, sw), yb.reshape(b * t, hw), wo[:sw], wo[sw:])
        return x1, (kf, vf), s_fin

    def channel(x, l):
        j = l // 2
        if l % 2 == 0:
            return _ffn(x, norm_ffn[l][None, :], ffn_w_gate[j].astype(BF16), ffn_w_up[j].astype(BF16),
                        ffn_w_down[j].astype(BF16))
        return _moe(x, norm_ffn[l][None, :], moe_router[j], moe_w_gate[j].astype(BF16),
                    moe_w_up[j].astype(BF16), moe_w_down[j].astype(BF16))

    bq_p = _tile(tp, 512)
    kv_p = kv_s = None
    for l in range(depth):
        xp, kv_p, s_p = mixer(xp, bp, tp, l, kv_p, cached=False, past_len=None, s0=zeros_state, bq=bq_p)
        outs["sp"].append(s_p)
        xs, kv_s, s_s = mixer(xs, bs, ts, l, kv_s, cached=True, past_len=past,
                              s0=state_hgrn[l].astype(F32), bq=ts)
        outs["ss"].append(s_s)
        xp = channel(xp, l)
        xs = channel(xs, l)

    heads5 = lambda a, b, t: jnp.transpose(a.reshape(depth, b, sb_heads, hd, t), (0, 1, 4, 2, 3))
    return (xp.reshape(bp, tp, d), xs.reshape(bs, ts, d), heads5(kv_p[0], bp, tp), heads5(kv_p[1], bp, tp),
            jnp.stack(outs["sp"]), heads5(kv_s[0], bs, ts), heads5(kv_s[1], bs, ts), jnp.stack(outs["ss"]))
```

```python
import functools

import numpy as np
import jax
import jax.numpy as jnp
from jax import lax
from jax.experimental import pallas as pl
from jax.experimental.pallas import tpu as pltpu

EPS = 1e-6
TOP_K = 2
LANES = 128
F32 = jnp.float32
BF16 = jnp.bfloat16
VMEM_LIMIT_BYTES = 56 * 1024 * 1024


def _cparams(semantics):
    return pltpu.CompilerParams(dimension_semantics=semantics, vmem_limit_bytes=VMEM_LIMIT_BYTES)


def _dot(a, b):
    return jnp.dot(a, b, preferred_element_type=F32)


def _dot_nt(a, b):
    return lax.dot_general(a, b, (((1,), (1,)), ((), ())), preferred_element_type=F32)


def _dot_tn(a, b):
    return lax.dot_general(a, b, (((0,), (0,)), ((), ())), preferred_element_type=F32)


def _sigmoid(x):
    return 1.0 / (1.0 + jnp.exp(-x))


def _rms(x):
    return x * lax.rsqrt(jnp.mean(x * x, axis=-1, keepdims=True) + EPS)


def _tile(n, pref):
    t = min(n, pref)
    assert n % t == 0, (n, t)
    return t


def _inproj_kernel(*refs, sw, hw, n_prev, seq_per_tile):
    x_ref, g_ref, w_ref, hm_ref, qg_ref, kg_ref, llb_ref, l1m_ref, oml_ref = refs[:9]
    prev = refs[9:9 + (2 if n_prev else 0)]
    q_ref, kf_ref, kb_ref, vf_ref, vb_ref, qh_ref, kh_ref, lf_ref, ih_ref, gh_ref = refs[9 + len(prev):]
    if n_prev:
        kf_ref[:n_prev] = prev[0][...]
        vf_ref[:n_prev] = prev[1][...]

    def store_time_minor(ref, y):
        t = y.shape[0] // seq_per_tile
        for sq in range(seq_per_tile):
            ref[n_prev, sq] = y[sq * t:(sq + 1) * t, :].T

    h = (_rms(x_ref[...]) * g_ref[...]).astype(BF16)

    def seg(lo, width):
        return _dot(h, w_ref[:, lo:lo + width])

    def headnorm(y, gain):
        m = _dot((y * y).astype(BF16), hm_ref[...])
        return (y * lax.rsqrt(m + EPS)) * gain

    q_ref[...] = headnorm(seg(0, sw), qg_ref[...]).astype(BF16)
    ka = headnorm(seg(sw, sw), kg_ref[...])
    store_time_minor(kf_ref, ka)
    kb_ref[...] = ka.astype(BF16)
    va = seg(2 * sw, sw)
    store_time_minor(vf_ref, va)
    vb_ref[...] = va.astype(BF16)

    o = 3 * sw
    qb = seg(o, hw)
    qh_ref[...] = qb * _sigmoid(qb)

    fb = seg(o + hw, hw)
    e = jnp.exp(-jnp.abs(fb))
    log_sig = jnp.minimum(fb, 0.0) - jnp.log(1.0 + e)
    c = l1m_ref[...] + log_sig
    a = llb_ref[...]
    lf_ref[...] = jnp.maximum(a, c) + jnp.log(1.0 + jnp.exp(-jnp.abs(a - c)))
    kh_ref[...] = oml_ref[...] * (jnp.where(fb >= 0.0, e, 1.0) / (1.0 + e))

    ih_ref[...] = seg(o + 2 * hw, hw)
    gh_ref[...] = seg(o + 3 * hw, hw)


def _inproj(x, g, w, hm, qg, kg, llb, l1m, oml, kv_prev, *, sw, hw, seq_len):
    n, d = x.shape
    tm = _tile(n, 512)
    n_prev = 0 if kv_prev is None else kv_prev[0].shape[0]
    spt = max(tm // seq_len, 1)
    tt = tm // spt
    nt = seq_len // tt
    assert spt * tt == tm and nt * tt == seq_len
    row = lambda i: (i, 0)
    const = lambda i: (0, 0)
    stacked = lambda i: (0, i // nt, 0, i % nt)
    kv_shape = (n_prev + 1, n // seq_len, sw, seq_len)
    specs = {"row_sw": pl.BlockSpec((tm, sw), row), "row_hw": pl.BlockSpec((tm, hw), row),
             "stack": pl.BlockSpec((n_prev + 1, spt, sw, tt), stacked)}
    outs = [("row_sw", (n, sw), BF16), ("stack", kv_shape, F32), ("row_sw", (n, sw), BF16),
            ("stack", kv_shape, F32), ("row_sw", (n, sw), BF16)] + [("row_hw", (n, hw), F32)] * 5
    prev_specs = [pl.BlockSpec((n_prev, spt, sw, tt), stacked)] * 2 if n_prev else []
    return pl.pallas_call(
        functools.partial(_inproj_kernel, sw=sw, hw=hw, n_prev=n_prev, seq_per_tile=spt),
        grid=(n // tm,),
        in_specs=[pl.BlockSpec((tm, d), row), pl.BlockSpec((1, d), const),
                  pl.BlockSpec(w.shape, const), pl.BlockSpec(hm.shape, const),
                  pl.BlockSpec((1, sw), const), pl.BlockSpec((1, sw), const),
                  pl.BlockSpec((1, hw), const), pl.BlockSpec((1, hw), const), pl.BlockSpec((1, hw), const)]
        + prev_specs,
        out_specs=[specs[kind] for kind, _, _ in outs],
        out_shape=[jax.ShapeDtypeStruct(shape, dt) for _, shape, dt in outs],
        compiler_params=_cparams(("parallel",)),
        name="inproj",
    )(x, g, w, hm, qg, kg, llb, l1m, oml, *(kv_prev or ()))


KEY_SUB = 256
SOFTPLUS_CLAMP = 80.0
CARRY_DEAD = 104.0


def _attn_kernel(q_ref, kn_ref, vn_ref, kp_hbm, vp_hbm, un_ref, up_ref, gain_ref,
                 y_ref, qm_ref, acc_ref, carry_ref, kbuf, vbuf, sem, alive_ref,
                 *, bq, subn, subp, hd, past_len, past_layer, past_transposed):
    bb = pl.program_id(0)
    p = pl.program_id(1)
    qi = pl.program_id(2)
    lane = lax.broadcasted_iota(jnp.int32, (1, LANES), 1)
    heads = LANES // hd
    n_past = (qi * bq if past_len is None else past_len) // subp

    def fetch(j, slot):
        if past_transposed:
            window = (past_layer, bb, pl.ds(p * LANES, LANES), pl.ds(j * subp, subp))
        else:
            window = (bb, pl.ds(j * subp, subp), pl.ds(p * LANES, LANES))
        return (pltpu.make_async_copy(kp_hbm.at[window], kbuf.at[slot], sem.at[0, slot]),
                pltpu.make_async_copy(vp_hbm.at[window], vbuf.at[slot], sem.at[1, slot]))

    def slot_of(j):
        return (n_past - 1 - j) & 1

    @pl.when(n_past > 0)
    def _():
        for cp in fetch(n_past - 1, 0):
            cp.start()

    q = q_ref[...]
    for hh in range(heads):
        in_head = (lane >= hh * hd) & (lane < (hh + 1) * hd)
        qm_ref[hh] = jnp.where(in_head, q, jnp.zeros_like(q))
    acc_ref[...] = jnp.zeros_like(acc_ref)
    carry_ref[...] = jnp.zeros_like(carry_ref)

    groups = bq // subn

    def one_head(hh, k, v, u, r0, r1, mask, transposed):
        if transposed:
            z = _dot(qm_ref[hh, r0:r1, :], k)
        else:
            z = _dot_nt(qm_ref[hh, r0:r1, :], k)
        sp = jnp.maximum(jnp.log(1.0 + jnp.exp(jnp.minimum(z, SOFTPLUS_CLAMP))), z)
        if mask is not None:
            sp = jnp.where(mask, sp, 0.0)
        after = _dot(sp.astype(BF16), u)
        w = jnp.exp((z - sp) - after)
        if mask is not None:
            w = jnp.where(mask, w, 0.0)
        pv = _dot_nt(w.astype(BF16), v) if transposed else _dot(w.astype(BF16), v)
        c = carry_ref[hh, r0:r1, :]
        acc_ref[hh, r0:r1, :] += jnp.exp(-c) * pv
        c_new = c + jnp.sum(sp, axis=-1, keepdims=True)
        carry_ref[hh, r0:r1, :] = c_new
        return c_new

    def set_alive(hh, g, c_group):
        alive_ref[hh * groups + g] = (jnp.min(c_group) < CARRY_DEAD).astype(jnp.int32)

    u_new = un_ref[...]
    for j in range(groups - 1, -1, -1):
        r0 = j * subn
        rows = r0 + lax.broadcasted_iota(jnp.int32, (bq - r0, subn), 0)
        cols = r0 + lax.broadcasted_iota(jnp.int32, (bq - r0, subn), 1)
        k = kn_ref[r0:r0 + subn, :].astype(BF16)
        v = vn_ref[r0:r0 + subn, :].astype(BF16)
        for hh in range(heads):
            c_new = one_head(hh, k, v, u_new, r0, bq, cols < rows, False)
            if j == 0:
                for g in range(groups):
                    set_alive(hh, g, c_new[g * subn:(g + 1) * subn])

    def any_alive():
        total = alive_ref[0]
        for i in range(1, heads * groups):
            total += alive_ref[i]
        return total > 0

    def past_block(j):
        slot = slot_of(j)
        for cp in fetch(j, slot):
            cp.wait()

        @pl.when(j > 0)
        def _():
            for cp in fetch(j - 1, 1 - slot):
                cp.start()

        u_past = up_ref[...]
        for hh in range(heads):
            for g in range(groups):
                @pl.when(alive_ref[hh * groups + g] == 1)
                def _(hh=hh, g=g):
                    c_new = one_head(hh, kbuf[slot].astype(BF16), vbuf[slot].astype(BF16), u_past,
                                     g * subn, (g + 1) * subn, None, past_transposed)
                    set_alive(hh, g, c_new)
        return j - 1

    j_end = lax.while_loop(lambda j: (j >= 0) & any_alive(), past_block, n_past - 1)

    @pl.when(j_end >= 0)
    def _():
        for cp in fetch(j_end, slot_of(j_end)):
            cp.wait()

    o = acc_ref[0]
    for hh in range(1, heads):
        in_head = (lane >= hh * hd) & (lane < (hh + 1) * hd)
        o = jnp.where(in_head, acc_ref[hh], o)
    o2 = o * o
    ms = jnp.zeros_like(o)
    for hh in range(heads):
        in_head = (lane >= hh * hd) & (lane < (hh + 1) * hd)
        ssum = jnp.sum(jnp.where(in_head, o2, 0.0), axis=-1, keepdims=True)
        ms = jnp.where(in_head, ssum * (1.0 / hd), ms)
    y_ref[...] = ((o * lax.rsqrt(ms + EPS)) * gain_ref[...]).astype(y_ref.dtype)


def _tril_strict(n):
    return jnp.asarray(np.tril(np.ones((n, n), np.float32), -1), BF16)


def _attention(q, k_new, v_new, k_past, v_past, gain, *, bq, past_len, hd, past_layer=None):
    b, tq, w = q.shape
    past_transposed = past_layer is not None
    subn = min(bq, KEY_SUB)
    subp = KEY_SUB
    assert bq % subn == 0 and (bq if past_len is None else past_len) % subp == 0
    heads = LANES // hd
    qmap = lambda bb, p, qi: (bb, qi, p)
    cmap = lambda bb, p, qi: (0, 0)
    past_buf = (2, LANES, subp) if past_transposed else (2, subp, LANES)
    return pl.pallas_call(
        functools.partial(_attn_kernel, bq=bq, subn=subn, subp=subp, hd=hd, past_len=past_len,
                          past_layer=past_layer, past_transposed=past_transposed),
        grid=(b, w // LANES, tq // bq),
        in_specs=[pl.BlockSpec((None, bq, LANES), qmap),
                  pl.BlockSpec((None, bq, LANES), qmap), pl.BlockSpec((None, bq, LANES), qmap),
                  pl.BlockSpec(memory_space=pl.ANY), pl.BlockSpec(memory_space=pl.ANY),
                  pl.BlockSpec((subn, subn), cmap), pl.BlockSpec((subp, subp), cmap),
                  pl.BlockSpec((1, LANES), cmap)],
        out_specs=pl.BlockSpec((None, bq, LANES), qmap),
        out_shape=jax.ShapeDtypeStruct((b, tq, w), BF16),
        scratch_shapes=[pltpu.VMEM((heads, bq, LANES), BF16), pltpu.VMEM((heads, bq, LANES), F32),
                        pltpu.VMEM((heads, bq, LANES), F32),
                        pltpu.VMEM(past_buf, k_past.dtype), pltpu.VMEM(past_buf, v_past.dtype),
                        pltpu.SemaphoreType.DMA((2, 2)), pltpu.SMEM((heads * (bq // subn),), jnp.int32)],
        compiler_params=_cparams(("parallel", "parallel", "arbitrary")),
        name="stickbreak_attn",
    )(q, k_new, v_new, k_past, v_past, _tril_strict(subn), _tril_strict(subp), gain)


def _hgrn_halvings(c):
    return [c >> (i + 1) for i in range(int(np.log2(c)))]


def _hgrn_masks(c):
    t = np.arange(c)
    masks = [np.eye(c, dtype=bool)]
    for h in _hgrn_halvings(c):
        blk = t // (2 * h)
        second = (t // h) % 2 == 1
        masks.append((blk[:, None] == blk[None, :]) & second[:, None] & (~second)[None, :])
    return jnp.asarray(np.stack(masks).astype(np.float32))


def _boundary_rows(b, h, row):
    c, dk = b.shape
    if 2 * h >= 8:
        n = c // (2 * h)
        ref = b.reshape(n, 2 * h, dk)[:, h - 1:h, :]
        return jnp.broadcast_to(ref, (n, 2 * h, dk)).reshape(c, dk)
    down1 = pltpu.roll(b, 1, 0)
    if h == 1:
        return jnp.where((row & 1) == 1, down1, b)
    m = row & 3
    up1 = pltpu.roll(b, c - 1, 0)
    down2 = pltpu.roll(b, 2, 0)
    return jnp.where(m == 0, up1, jnp.where(m == 1, b, jnp.where(m == 2, down1, down2)))


def _hgrn_kernel(q_ref, k_ref, lf_ref, v_ref, g_ref, s0_ref, tri_ref, msk_ref, gain_ref,
                 y_ref, sout_ref, st_ref, *, c, n_chunks):
    t = pl.program_id(2)
    dk = q_ref.shape[-1]

    @pl.when(t == 0)
    def _():
        st_ref[...] = s0_ref[...].T

    row = lax.broadcasted_iota(jnp.int32, (c, dk), 0)
    tri = tri_ref[...]
    for ci in range(n_chunks):
        sl = slice(ci * c, (ci + 1) * c)
        q = q_ref[sl, :]
        k = k_ref[sl, :]
        lf = lf_ref[sl, :]
        v = v_ref[sl, :].astype(BF16)
        g = g_ref[sl, :]
        hi = lf.astype(BF16)
        r1 = lf - hi.astype(F32)
        mid = r1.astype(BF16)
        lo = (r1 - mid.astype(F32)).astype(BF16)
        b = _dot(tri, hi) + _dot(tri, mid) + _dot(tri, lo)
        st = st_ref[...]
        o = _dot_nt((q * jnp.exp(b)).astype(BF16), st.astype(BF16))
        a = msk_ref[0] * _dot_nt(q.astype(BF16), k.astype(BF16))
        for lv, h in enumerate(_hgrn_halvings(c)):
            gap = b - _boundary_rows(b, h, row)
            el = jnp.exp(jnp.where((row & h) != 0, gap, -gap))
            a += msk_ref[1 + lv] * _dot_nt((q * el).astype(BF16), (k * el).astype(BF16))
        o += _dot(a.astype(BF16), v)
        b_last = b[c - 1:c, :]
        k_end = (k * jnp.exp(b_last - b)).astype(BF16)
        st_ref[...] = st * jnp.exp(b_last) + _dot_tn(v, k_end)
        y = (_rms(o) * gain_ref[...]) * (g * _sigmoid(g))
        y_ref[sl, :] = y.astype(y_ref.dtype)

    @pl.when(t == pl.num_programs(2) - 1)
    def _():
        sout_ref[...] = st_ref[...].T


def _hgrn(q, k, lf, v, g, s0, gain, *, dk):
    b, t, hw = q.shape
    nh = hw // dk
    c = _tile(t, 256)
    tt = _tile(t, 512)
    msk = _hgrn_masks(c)
    tri = jnp.asarray(np.tril(np.ones((c, c), np.float32)), BF16)
    tok = lambda bb, h, ti: (bb, ti, h)
    smap = lambda bb, h, ti: (bb, h, 0, 0)
    return pl.pallas_call(
        functools.partial(_hgrn_kernel, c=c, n_chunks=tt // c),
        grid=(b, nh, t // tt),
        in_specs=[pl.BlockSpec((None, tt, dk), tok)] * 5
        + [pl.BlockSpec((None, None, dk, dk), smap),
           pl.BlockSpec((c, c), lambda bb, h, ti: (0, 0)),
           pl.BlockSpec(msk.shape, lambda bb, h, ti: (0, 0, 0)),
           pl.BlockSpec((1, dk), lambda bb, h, ti: (0, 0))],
        out_specs=[pl.BlockSpec((None, tt, dk), tok), pl.BlockSpec((None, None, dk, dk), smap)],
        out_shape=[jax.ShapeDtypeStruct((b, t, hw), BF16), jax.ShapeDtypeStruct((b, nh, dk, dk), F32)],
        scratch_shapes=[pltpu.VMEM((dk, dk), F32)],
        compiler_params=_cparams(("parallel", "parallel", "arbitrary")),
        name="hgrn2",
    )(q, k, lf, v, g, s0, tri, msk, gain)


def _outproj_kernel(x_ref, ya_ref, yb_ref, wa_ref, wb_ref, o_ref):
    o_ref[...] = x_ref[...] + _dot(ya_ref[...], wa_ref[...]) + _dot(yb_ref[...], wb_ref[...])


def _outproj(x, ya, yb, wa, wb):
    n, d = x.shape
    tm = _tile(n, 512)
    row = lambda i: (i, 0)
    const = lambda i: (0, 0)
    return pl.pallas_call(
        _outproj_kernel,
        grid=(n // tm,),
        in_specs=[pl.BlockSpec((tm, d), row), pl.BlockSpec((tm, ya.shape[1]), row),
                  pl.BlockSpec((tm, yb.shape[1]), row), pl.BlockSpec(wa.shape, const),
                  pl.BlockSpec(wb.shape, const)],
        out_specs=pl.BlockSpec((tm, d), row),
        out_shape=jax.ShapeDtypeStruct((n, d), F32),
        compiler_params=_cparams(("parallel",)),
        name="outproj",
    )(x, ya, yb, wa, wb)


def _ffn_kernel(x_ref, g_ref, wg_ref, wu_ref, wd_ref, o_ref, h_ref, acc_ref):
    f = pl.program_id(1)

    @pl.when(f == 0)
    def _():
        h_ref[...] = (_rms(x_ref[...]) * g_ref[...]).astype(BF16)
        acc_ref[...] = jnp.zeros_like(acc_ref)

    h = h_ref[...]
    gate = _dot(h, wg_ref[...])
    up = _dot(h, wu_ref[...])
    acc_ref[...] += _dot(((gate * _sigmoid(gate)) * up).astype(BF16), wd_ref[...])

    @pl.when(f == pl.num_programs(1) - 1)
    def _():
        o_ref[...] = x_ref[...] + acc_ref[...]


def _ffn(x, g, wg, wu, wd):
    n, d = x.shape
    ff = wg.shape[1]
    tm = _tile(n, 1024)
    tf = _tile(ff, 256)
    return pl.pallas_call(
        _ffn_kernel,
        grid=(n // tm, ff // tf),
        in_specs=[pl.BlockSpec((tm, d), lambda i, f: (i, 0)), pl.BlockSpec((1, d), lambda i, f: (0, 0)),
                  pl.BlockSpec((d, tf), lambda i, f: (0, f)), pl.BlockSpec((d, tf), lambda i, f: (0, f)),
                  pl.BlockSpec((tf, d), lambda i, f: (f, 0))],
        out_specs=pl.BlockSpec((tm, d), lambda i, f: (i, 0)),
        out_shape=jax.ShapeDtypeStruct((n, d), F32),
        scratch_shapes=[pltpu.VMEM((tm, d), BF16), pltpu.VMEM((tm, d), F32)],
        compiler_params=_cparams(("parallel", "arbitrary")),
        name="ffn_dense",
    )(x, g, wg, wu, wd)


ROUTE_E1, ROUTE_E2, ROUTE_R1, ROUTE_R2, ROUTE_G1, ROUTE_G2 = range(6)


def _router_kernel(x_ref, g_ref, rhi_ref, rlo_ref, tri_ref, route_ref, cnt_ref, base_ref, *, n_experts):
    i = pl.program_id(0)
    lane = lax.broadcasted_iota(jnp.int32, (1, LANES), 1)

    @pl.when(i == 0)
    def _():
        base_ref[...] = jnp.zeros_like(base_ref)

    hf = _rms(x_ref[...]) * g_ref[...]
    h_hi = hf.astype(BF16)
    h_lo = (hf - h_hi.astype(F32)).astype(BF16)
    logits = _dot(h_hi, rhi_ref[...]) + _dot(h_lo, rhi_ref[...]) + _dot(h_hi, rlo_ref[...])
    logits = jnp.where(lane < n_experts, logits, -jnp.inf)
    m1 = jnp.max(logits, axis=-1, keepdims=True)
    i1 = jnp.min(jnp.where(logits == m1, lane, LANES), axis=-1, keepdims=True)
    rest = jnp.where(lane == i1, -jnp.inf, logits)
    m2 = jnp.max(rest, axis=-1, keepdims=True)
    i2 = jnp.min(jnp.where(rest == m2, lane, LANES), axis=-1, keepdims=True)
    e2 = jnp.exp(m2 - m1)
    g1 = 1.0 / (1.0 + e2)
    g2 = e2 / (1.0 + e2)
    oh1 = (lane == i1).astype(F32)
    oh2 = (lane == i2).astype(F32)
    both = oh1 + oh2
    before = _dot(tri_ref[...], both.astype(BF16)) + base_ref[...]
    r1 = jnp.sum(oh1 * before, axis=-1, keepdims=True)
    r2 = jnp.sum(oh2 * before, axis=-1, keepdims=True)
    rec = jnp.zeros(route_ref.shape, F32)
    for slot, val in ((ROUTE_E1, i1.astype(F32)), (ROUTE_E2, i2.astype(F32)), (ROUTE_R1, r1),
                      (ROUTE_R2, r2), (ROUTE_G1, g1), (ROUTE_G2, g2)):
        rec = jnp.where(lane == slot, val, rec)
    route_ref[...] = rec
    base_ref[...] += jnp.sum(both, axis=0, keepdims=True)
    cnt_ref[...] = base_ref[...]


def _router(x, g, router):
    n, d = x.shape
    ne = router.shape[1]
    tm = _tile(n, 1024)
    rpad = jnp.zeros((d, LANES), F32).at[:, :ne].set(router)
    rhi = rpad.astype(BF16)
    rlo = (rpad - rhi.astype(F32)).astype(BF16)
    const = lambda i: (0, 0)
    return pl.pallas_call(
        functools.partial(_router_kernel, n_experts=ne),
        grid=(n // tm,),
        in_specs=[pl.BlockSpec((tm, d), lambda i: (i, 0)), pl.BlockSpec((1, d), const),
                  pl.BlockSpec((d, LANES), const), pl.BlockSpec((d, LANES), const),
                  pl.BlockSpec((tm, tm), const)],
        out_specs=[pl.BlockSpec((tm, LANES), lambda i: (i, 0)), pl.BlockSpec((1, LANES), const)],
        out_shape=[jax.ShapeDtypeStruct((n, LANES), F32), jax.ShapeDtypeStruct((1, LANES), F32)],
        scratch_shapes=[pltpu.VMEM((1, LANES), F32)],
        compiler_params=_cparams(("arbitrary",)),
        name="moe_router",
    )(x, g, rhi, rlo, _tril_strict(tm))


def _row_copy(src_ref, src_row, dst_ref, dst_row, sem):
    return pltpu.make_async_copy(src_ref.at[pl.ds(src_row, 1), :], dst_ref.at[pl.ds(dst_row, 1), :], sem)


def _dispatch_kernel(pad_lo_ref, pad_hi_ref, p1_ref, p2_ref, x_ref, xs_ref, sem, pad_sem, *, tb, n_pad_ranges):
    def issue(t, carry):
        _row_copy(x_ref, t, xs_ref, p1_ref[0, 0, t], sem).start(priority=0)
        _row_copy(x_ref, t, xs_ref, p2_ref[0, 0, t], sem).start(priority=1)
        return carry

    lax.fori_loop(0, tb, issue, 0, unroll=8)

    @pl.when(pl.program_id(0) == 0)
    def _():
        def fill(r, carry):
            _row_copy(x_ref, 0, xs_ref, r, pad_sem).start()
            return carry

        def drain(r, carry):
            _row_copy(x_ref, 0, xs_ref, r, pad_sem).wait()
            return carry

        for e in range(n_pad_ranges):
            lax.fori_loop(pad_lo_ref[e], pad_hi_ref[e], fill, 0)
        for e in range(n_pad_ranges):
            lax.fori_loop(pad_lo_ref[e], pad_hi_ref[e], drain, 0)

    for _ in range(TOP_K):
        pltpu.make_async_copy(x_ref, xs_ref.at[pl.ds(0, tb), :], sem).wait()


def _dispatch(x, pos1, pos2, n_rows, pad_lo, pad_hi):
    n, d = x.shape
    tb = pos1.shape[-1]
    smem = lambda: pl.BlockSpec((1, 1, tb), lambda i, lo, hi: (i, 0, 0), memory_space=pltpu.SMEM)
    grid_spec = pltpu.PrefetchScalarGridSpec(
        num_scalar_prefetch=2,
        grid=(n // tb,),
        in_specs=[smem(), smem(), pl.BlockSpec((tb, d), lambda i, lo, hi: (i, 0))],
        out_specs=pl.BlockSpec(memory_space=pl.ANY),
        scratch_shapes=[pltpu.SemaphoreType.DMA(()), pltpu.SemaphoreType.DMA(())],
    )
    return pl.pallas_call(
        functools.partial(_dispatch_kernel, tb=tb, n_pad_ranges=pad_lo.shape[0]),
        grid_spec=grid_spec,
        out_shape=jax.ShapeDtypeStruct((n_rows, d), F32),
        compiler_params=_cparams(("arbitrary",)),
        name="moe_dispatch",
    )(pad_lo, pad_hi, pos1, pos2, x)


def _experts_kernel(te_ref, nv_ref, x_ref, g_ref, wg_ref, wu_ref, wd_ref, o_ref, h_ref, acc_ref):
    del te_ref
    i = pl.program_id(0)
    f = pl.program_id(1)
    valid = i < nv_ref[0]

    @pl.when(valid & (f == 0))
    def _():
        h_ref[...] = (_rms(x_ref[...]) * g_ref[...]).astype(BF16)
        acc_ref[...] = jnp.zeros_like(acc_ref)

    @pl.when(valid)
    def _():
        h = h_ref[...]
        gate = _dot(h, wg_ref[...])
        up = _dot(h, wu_ref[...])
        acc_ref[...] += _dot(((gate * _sigmoid(gate)) * up).astype(BF16), wd_ref[...])

    @pl.when(valid & (f == pl.num_programs(1) - 1))
    def _():
        o_ref[...] = acc_ref[...]

    @pl.when(jnp.logical_not(valid) & (f == pl.num_programs(1) - 1))
    def _():
        o_ref[...] = jnp.zeros_like(o_ref)


def _experts(xs, g, wg, wu, wd, tile_expert, n_valid, tr):
    n_rows, d = xs.shape
    ff = wg.shape[2]
    tf = _tile(ff, 256)
    nf = ff // tf
    rowmap = lambda i, f, te, nv: (jnp.minimum(i, nv[0] - 1), 0)
    fsel = lambda i, f, nv: jnp.where(i < nv[0], f, nf - 1)
    grid_spec = pltpu.PrefetchScalarGridSpec(
        num_scalar_prefetch=2,
        grid=(n_rows // tr, nf),
        in_specs=[pl.BlockSpec((tr, d), rowmap), pl.BlockSpec((1, d), lambda i, f, te, nv: (0, 0)),
                  pl.BlockSpec((None, d, tf), lambda i, f, te, nv: (te[i], 0, fsel(i, f, nv))),
                  pl.BlockSpec((None, d, tf), lambda i, f, te, nv: (te[i], 0, fsel(i, f, nv))),
                  pl.BlockSpec((None, tf, d), lambda i, f, te, nv: (te[i], fsel(i, f, nv), 0))],
        out_specs=pl.BlockSpec((tr, d), lambda i, f, te, nv: (i, 0)),
        scratch_shapes=[pltpu.VMEM((tr, d), BF16), pltpu.VMEM((tr, d), F32)],
    )
    return pl.pallas_call(
        _experts_kernel,
        grid_spec=grid_spec,
        out_shape=jax.ShapeDtypeStruct((n_rows, d), F32),
        compiler_params=_cparams(("arbitrary", "arbitrary")),
        name="moe_experts",
    )(tile_expert, n_valid, xs, g, wg, wu, wd)


def _combine_kernel(p1_ref, p2_ref, x_ref, route_ref, ys_ref, o_ref, buf_ref, sem, *, tc):
    def issue(t, carry):
        _row_copy(ys_ref, p1_ref[0, 0, t], buf_ref.at[0], t, sem).start(priority=0)
        _row_copy(ys_ref, p2_ref[0, 0, t], buf_ref.at[1], t, sem).start(priority=1)
        return carry

    lax.fori_loop(0, tc, issue, 0, unroll=8)
    for slot in range(TOP_K):
        pltpu.make_async_copy(ys_ref.at[pl.ds(0, tc), :], buf_ref.at[slot], sem).wait()
    lane = lax.broadcasted_iota(jnp.int32, (1, LANES), 1)
    route = route_ref[...]
    g1 = jnp.sum(jnp.where(lane == ROUTE_G1, route, 0.0), axis=-1, keepdims=True)
    g2 = jnp.sum(jnp.where(lane == ROUTE_G2, route, 0.0), axis=-1, keepdims=True)
    o_ref[...] = x_ref[...] + (g1 * buf_ref[0] + g2 * buf_ref[1])


def _combine(x, route, ys, pos1, pos2):
    n, d = x.shape
    tc = pos1.shape[-1]
    smem = lambda: pl.BlockSpec((1, 1, tc), lambda i: (i, 0, 0), memory_space=pltpu.SMEM)
    return pl.pallas_call(
        functools.partial(_combine_kernel, tc=tc),
        grid=(n // tc,),
        in_specs=[smem(), smem(), pl.BlockSpec((tc, d), lambda i: (i, 0)),
                  pl.BlockSpec((tc, LANES), lambda i: (i, 0)), pl.BlockSpec(memory_space=pl.ANY)],
        out_specs=pl.BlockSpec((tc, d), lambda i: (i, 0)),
        out_shape=jax.ShapeDtypeStruct((n, d), F32),
        scratch_shapes=[pltpu.VMEM((TOP_K, tc, d), F32), pltpu.SemaphoreType.DMA(())],
        compiler_params=_cparams(("arbitrary",)),
        name="moe_combine",
    )(pos1, pos2, x, route, ys)


def _moe(x, g, router, wg, wu, wd):
    n, d = x.shape
    ne = wg.shape[0]
    tr = 1024 if n >= 8192 else 256
    tb = _tile(n, 512)
    route, cnt = _router(x, g, router)
    counts = cnt[0, :ne].astype(jnp.int32)
    padded = ((counts + tr - 1) // tr) * tr
    ends = jnp.cumsum(padded)
    starts = ends - padded
    e1 = route[:, ROUTE_E1].astype(jnp.int32)
    e2 = route[:, ROUTE_E2].astype(jnp.int32)
    pos1 = (starts[e1] + route[:, ROUTE_R1].astype(jnp.int32)).reshape(n // tb, 1, tb)
    pos2 = (starts[e2] + route[:, ROUTE_R2].astype(jnp.int32)).reshape(n // tb, 1, tb)
    max_tiles = -(-(TOP_K * n + ne * (tr - 1)) // tr)
    n_valid = (ends[-1] // tr).reshape(1)
    tile_start = jnp.minimum(jnp.arange(max_tiles, dtype=jnp.int32), n_valid[0] - 1) * tr
    tile_expert = jnp.minimum(jnp.searchsorted(ends, tile_start, side="right"), ne - 1).astype(jnp.int32)
    n_rows = max_tiles * tr
    pad_lo = jnp.concatenate([starts + counts, ends[-1:]]).astype(jnp.int32)
    pad_hi = jnp.concatenate([ends, jnp.full((1,), n_rows, ends.dtype)]).astype(jnp.int32)
    xs = _dispatch(x, pos1, pos2, n_rows, pad_lo, pad_hi)
    ys = _experts(xs, g, wg, wu, wd, tile_expert, n_valid.astype(jnp.int32), tr)
    return _combine(x, route, ys, pos1, pos2)


def kernel(x_prompt, x_sample, cache_k, cache_v, state_hgrn, norm_mix, w_in, sb_q_gain, sb_k_gain,
           hg_lower_bounds, sb_out_gain, hg_out_gain, w_out, norm_ffn, ffn_w_gate, ffn_w_up, ffn_w_down,
           moe_router, moe_w_gate, moe_w_up, moe_w_down):
    depth = w_in.shape[0]
    bp, tp, d = x_prompt.shape
    bs, ts, _ = x_sample.shape
    past = cache_k.shape[2]
    sb_heads, hd = cache_k.shape[3], cache_k.shape[4]
    sw = sb_heads * hd
    dk = hg_out_gain.shape[1]
    hw = hg_lower_bounds.shape[1]
    hg_heads = hw // dk
    sb_scale = hd ** -0.5

    lbs = jnp.cumsum(jax.nn.softmax(hg_lower_bounds.astype(F32), axis=0), axis=0)
    lbs = lbs - lbs[0:1]
    head_mean = jnp.asarray(np.kron(np.eye(sb_heads), np.full((hd, hd), 1.0 / hd)), BF16)

    xp = x_prompt.reshape(bp * tp, d)
    xs = x_sample.reshape(bs * ts, d)
    zeros_state = jnp.zeros((bp, hg_heads, dk, dk), F32)
    outs = {k: [] for k in ("sp", "ss")}
    cache_kt = jnp.transpose(cache_k, (0, 1, 3, 4, 2)).reshape(depth, bs, sw, past)
    cache_vt = jnp.transpose(cache_v, (0, 1, 3, 4, 2)).reshape(depth, bs, sw, past)

    def mixer(x, b, t, l, kv_prev, *, cached, past_len, s0, bq):
        qg = (jnp.tile(sb_q_gain[l], sb_heads) * sb_scale)[None, :]
        kg = jnp.tile(sb_k_gain[l], sb_heads)[None, :]
        q, kf, kb, vf, vb, qh, kh, lf, ih, gh = _inproj(
            x, norm_mix[l][None, :], w_in[l].astype(BF16), head_mean, qg, kg,
            jnp.log(lbs[l])[None, :], jnp.log1p(-lbs[l])[None, :], (1.0 - lbs[l])[None, :], kv_prev,
            sw=sw, hw=hw, seq_len=t)
        r3 = lambda a: a.reshape(b, t, a.shape[-1])
        kb3, vb3 = r3(kb), r3(vb)
        ya = _attention(r3(q), kb3, vb3, cache_kt if cached else kb3, cache_vt if cached else vb3,
                        jnp.tile(sb_out_gain[l], LANES // hd)[None, :],
                        bq=bq, past_len=past_len, hd=hd, past_layer=l if cached else None)
        yb, s_fin = _hgrn(r3(qh), r3(kh), r3(lf), r3(ih), r3(gh), s0, hg_out_gain[l][None, :], dk=dk)
        wo = w_out[l].astype(BF16)
        x1 = _outproj(x, ya.reshape(b * t, sw), yb.reshape(b * t, hw), wo[:sw], wo[sw:])
        return x1, (kf, vf), s_fin

    def channel(x, l):
        j = l // 2
        if l % 2 == 0:
            return _ffn(x, norm_ffn[l][None, :], ffn_w_gate[j].astype(BF16), ffn_w_up[j].astype(BF16),
                        ffn_w_down[j].astype(BF16))
        return _moe(x, norm_ffn[l][None, :], moe_router[j], moe_w_gate[j].astype(BF16),
                    moe_w_up[j].astype(BF16), moe_w_down[j].astype(BF16))

    bq_p = _tile(tp, 512)
    kv_p = kv_s = None
    for l in range(depth):
        xp, kv_p, s_p = mixer(xp, bp, tp, l, kv_p, cached=False, past_len=None, s0=zeros_state, bq=bq_p)
        outs["sp"].append(s_p)
        xs, kv_s, s_s = mixer(xs, bs, ts, l, kv_s, cached=True, past_len=past,
                              s0=state_hgrn[l].astype(F32), bq=ts)
        outs["ss"].append(s_s)
        xp = channel(xp, l)
        xs = channel(xs, l)

    heads5 = lambda a, b, t: jnp.transpose(a.reshape(depth, b, sb_heads, hd, t), (0, 1, 4, 2, 3))
    return (xp.reshape(bp, tp, d), xs.reshape(bs, ts, d), heads5(kv_p[0], bp, tp), heads5(kv_p[1], bp, tp),
            jnp.stack(outs["sp"]), heads5(kv_s[0], bs, ts), heads5(kv_s[1], bs, ts), jnp.stack(outs["ss"]))
```

```python
import functools

import numpy as np
import jax
import jax.numpy as jnp
from jax import lax
from jax.experimental import pallas as pl
from jax.experimental.pallas import tpu as pltpu

EPS = 1e-6
TOP_K = 2
LANES = 128
F32 = jnp.float32
BF16 = jnp.bfloat16
VMEM_LIMIT_BYTES = 56 * 1024 * 1024


def _cparams(semantics):
    return pltpu.CompilerParams(dimension_semantics=semantics, vmem_limit_bytes=VMEM_LIMIT_BYTES)


def _dot(a, b):
    return jnp.dot(a, b, preferred_element_type=F32)


def _dot_nt(a, b):
    return lax.dot_general(a, b, (((1,), (1,)), ((), ())), preferred_element_type=F32)


def _dot_tn(a, b):
    return lax.dot_general(a, b, (((0,), (0,)), ((), ())), preferred_element_type=F32)


def _sigmoid(x):
    return 1.0 / (1.0 + jnp.exp(-x))


def _rms(x):
    return x * lax.rsqrt(jnp.mean(x * x, axis=-1, keepdims=True) + EPS)


def _tile(n, pref):
    t = min(n, pref)
    assert n % t == 0, (n, t)
    return t


def _inproj_kernel(*refs, sw, hw, hd, n_prev, seq_per_tile):
    x_ref, g_ref, w_ref, qg_ref, kg_ref, llb_ref, l1m_ref, oml_ref = refs[:8]
    prev = refs[8:8 + (2 if n_prev else 0)]
    q_ref, kf_ref, kb_ref, vf_ref, vb_ref, qh_ref, kh_ref, lf_ref, ih_ref, gh_ref = refs[8 + len(prev):]
    if n_prev:
        kf_ref[:n_prev] = prev[0][...]
        vf_ref[:n_prev] = prev[1][...]

    def store_time_minor(ref, y):
        t = y.shape[0] // seq_per_tile
        for sq in range(seq_per_tile):
            ref[n_prev, sq] = y[sq * t:(sq + 1) * t, :].T

    h = (_rms(x_ref[...]) * g_ref[...]).astype(BF16)

    def seg(lo, width):
        return _dot(h, w_ref[:, lo:lo + width])

    def headnorm(y, gain):
        lane = lax.broadcasted_iota(jnp.int32, (1, LANES), 1)
        tiles = []
        for c0 in range(0, y.shape[1], LANES):
            y2 = y[:, c0:c0 + LANES] * y[:, c0:c0 + LANES]
            m = jnp.zeros_like(y2)
            for h0 in range(0, LANES, hd):
                in_head = (lane >= h0) & (lane < h0 + hd)
                m = jnp.where(in_head, jnp.sum(jnp.where(in_head, y2, 0.0), axis=-1, keepdims=True), m)
            tiles.append(m * (1.0 / hd))
        m = jnp.concatenate(tiles, axis=1)
        return (y * lax.rsqrt(m + EPS)) * gain

    q_ref[...] = headnorm(seg(0, sw), qg_ref[...]).astype(BF16)
    ka = headnorm(seg(sw, sw), kg_ref[...])
    store_time_minor(kf_ref, ka)
    kb_ref[...] = ka.astype(BF16)
    va = seg(2 * sw, sw)
    store_time_minor(vf_ref, va)
    vb_ref[...] = va.astype(BF16)

    o = 3 * sw
    qb = seg(o, hw)
    qh_ref[...] = qb * _sigmoid(qb)

    fb = seg(o + hw, hw)
    e = jnp.exp(-jnp.abs(fb))
    log_sig = jnp.minimum(fb, 0.0) - jnp.log(1.0 + e)
    c = l1m_ref[...] + log_sig
    a = llb_ref[...]
    lf_ref[...] = jnp.maximum(a, c) + jnp.log(1.0 + jnp.exp(-jnp.abs(a - c)))
    kh_ref[...] = oml_ref[...] * (jnp.where(fb >= 0.0, e, 1.0) / (1.0 + e))

    ih_ref[...] = seg(o + 2 * hw, hw)
    gh_ref[...] = seg(o + 3 * hw, hw)


def _inproj(x, g, w, qg, kg, llb, l1m, oml, kv_prev, *, sw, hw, hd, seq_len):
    n, d = x.shape
    tm = _tile(n, 512)
    n_prev = 0 if kv_prev is None else kv_prev[0].shape[0]
    spt = max(tm // seq_len, 1)
    tt = tm // spt
    nt = seq_len // tt
    assert spt * tt == tm and nt * tt == seq_len
    row = lambda i: (i, 0)
    const = lambda i: (0, 0)
    stacked = lambda i: (0, i // nt, 0, i % nt)
    kv_shape = (n_prev + 1, n // seq_len, sw, seq_len)
    specs = {"row_sw": pl.BlockSpec((tm, sw), row), "row_hw": pl.BlockSpec((tm, hw), row),
             "stack": pl.BlockSpec((n_prev + 1, spt, sw, tt), stacked)}
    outs = [("row_sw", (n, sw), BF16), ("stack", kv_shape, F32), ("row_sw", (n, sw), BF16),
            ("stack", kv_shape, F32), ("row_sw", (n, sw), BF16)] + [("row_hw", (n, hw), F32)] * 5
    prev_specs = [pl.BlockSpec((n_prev, spt, sw, tt), stacked)] * 2 if n_prev else []
    return pl.pallas_call(
        functools.partial(_inproj_kernel, sw=sw, hw=hw, hd=hd, n_prev=n_prev, seq_per_tile=spt),
        grid=(n // tm,),
        in_specs=[pl.BlockSpec((tm, d), row), pl.BlockSpec((1, d), const),
                  pl.BlockSpec(w.shape, const),
                  pl.BlockSpec((1, sw), const), pl.BlockSpec((1, sw), const),
                  pl.BlockSpec((1, hw), const), pl.BlockSpec((1, hw), const), pl.BlockSpec((1, hw), const)]
        + prev_specs,
        out_specs=[specs[kind] for kind, _, _ in outs],
        out_shape=[jax.ShapeDtypeStruct(shape, dt) for _, shape, dt in outs],
        compiler_params=_cparams(("parallel",)),
        name="inproj",
    )(x, g, w, qg, kg, llb, l1m, oml, *(kv_prev or ()))


KEY_SUB = 256
SOFTPLUS_CLAMP = 80.0
CARRY_DEAD = 104.0


def _attn_kernel(q_ref, kn_ref, vn_ref, kp_hbm, vp_hbm, un_ref, up_ref, gain_ref,
                 y_ref, qm_ref, acc_ref, carry_ref, kbuf, vbuf, sem, alive_ref,
                 *, bq, subn, subp, hd, past_len, past_layer, past_transposed):
    bb = pl.program_id(0)
    p = pl.program_id(1)
    qi = pl.program_id(2)
    lane = lax.broadcasted_iota(jnp.int32, (1, LANES), 1)
    heads = LANES // hd
    n_past = (qi * bq if past_len is None else past_len) // subp

    def fetch(j, slot):
        if past_transposed:
            window = (past_layer, bb, pl.ds(p * LANES, LANES), pl.ds(j * subp, subp))
        else:
            window = (bb, pl.ds(j * subp, subp), pl.ds(p * LANES, LANES))
        return (pltpu.make_async_copy(kp_hbm.at[window], kbuf.at[slot], sem.at[0, slot]),
                pltpu.make_async_copy(vp_hbm.at[window], vbuf.at[slot], sem.at[1, slot]))

    def slot_of(j):
        return (n_past - 1 - j) & 1

    @pl.when(n_past > 0)
    def _():
        for cp in fetch(n_past - 1, 0):
            cp.start()

    q = q_ref[...]
    for hh in range(heads):
        in_head = (lane >= hh * hd) & (lane < (hh + 1) * hd)
        qm_ref[hh] = jnp.where(in_head, q, jnp.zeros_like(q))
    acc_ref[...] = jnp.zeros_like(acc_ref)
    carry_ref[...] = jnp.zeros_like(carry_ref)

    groups = bq // subn

    def one_head(hh, k, v, u, r0, r1, mask, transposed):
        if transposed:
            z = _dot(qm_ref[hh, r0:r1, :], k)
        else:
            z = _dot_nt(qm_ref[hh, r0:r1, :], k)
        sp = jnp.maximum(jnp.log(1.0 + jnp.exp(jnp.minimum(z, SOFTPLUS_CLAMP))), z)
        if mask is not None:
            sp = jnp.where(mask, sp, 0.0)
        after = _dot(sp.astype(BF16), u)
        w = jnp.exp((z - sp) - after)
        if mask is not None:
            w = jnp.where(mask, w, 0.0)
        pv = _dot_nt(w.astype(BF16), v) if transposed else _dot(w.astype(BF16), v)
        c = carry_ref[hh, r0:r1, :]
        acc_ref[hh, r0:r1, :] += jnp.exp(-c) * pv
        c_new = c + jnp.sum(sp, axis=-1, keepdims=True)
        carry_ref[hh, r0:r1, :] = c_new
        return c_new

    def set_alive(hh, g, c_group):
        alive_ref[hh * groups + g] = (jnp.min(c_group) < CARRY_DEAD).astype(jnp.int32)

    u_new = un_ref[...]
    for j in range(groups - 1, -1, -1):
        r0 = j * subn
        rows = r0 + lax.broadcasted_iota(jnp.int32, (bq - r0, subn), 0)
        cols = r0 + lax.broadcasted_iota(jnp.int32, (bq - r0, subn), 1)
        k = kn_ref[r0:r0 + subn, :].astype(BF16)
        v = vn_ref[r0:r0 + subn, :].astype(BF16)
        for hh in range(heads):
            c_new = one_head(hh, k, v, u_new, r0, bq, cols < rows, False)
            if j == 0:
                for g in range(groups):
                    set_alive(hh, g, c_new[g * subn:(g + 1) * subn])

    def any_alive():
        total = alive_ref[0]
        for i in range(1, heads * groups):
            total += alive_ref[i]
        return total > 0

    def past_block(j):
        slot = slot_of(j)
        for cp in fetch(j, slot):
            cp.wait()

        @pl.when(j > 0)
        def _():
            for cp in fetch(j - 1, 1 - slot):
                cp.start()

        u_past = up_ref[...]

        def group_heads(g, head_list):
            k = kbuf[slot].astype(BF16)
            v = vbuf[slot].astype(BF16)
            for hh in head_list:
                c_new = one_head(hh, k, v, u_past, g * subn, (g + 1) * subn, None, past_transposed)
                set_alive(hh, g, c_new)

        for g in range(groups):
            n_alive = alive_ref[g]
            for hh in range(1, heads):
                n_alive += alive_ref[hh * groups + g]
            pl.when(n_alive == heads)(functools.partial(group_heads, g, range(heads)))
            for hh in range(heads):
                pl.when((n_alive < heads) & (alive_ref[hh * groups + g] == 1))(
                    functools.partial(group_heads, g, (hh,)))
        return j - 1

    j_end = lax.while_loop(lambda j: (j >= 0) & any_alive(), past_block, n_past - 1)

    @pl.when(j_end >= 0)
    def _():
        for cp in fetch(j_end, slot_of(j_end)):
            cp.wait()

    o = acc_ref[0]
    for hh in range(1, heads):
        in_head = (lane >= hh * hd) & (lane < (hh + 1) * hd)
        o = jnp.where(in_head, acc_ref[hh], o)
    o2 = o * o
    ms = jnp.zeros_like(o)
    for hh in range(heads):
        in_head = (lane >= hh * hd) & (lane < (hh + 1) * hd)
        ssum = jnp.sum(jnp.where(in_head, o2, 0.0), axis=-1, keepdims=True)
        ms = jnp.where(in_head, ssum * (1.0 / hd), ms)
    y_ref[...] = ((o * lax.rsqrt(ms + EPS)) * gain_ref[...]).astype(y_ref.dtype)


def _tril_strict(n):
    return jnp.asarray(np.tril(np.ones((n, n), np.float32), -1), BF16)


def _attention(q, k_new, v_new, k_past, v_past, gain, *, bq, past_len, hd, past_layer=None):
    b, tq, w = q.shape
    past_transposed = past_layer is not None
    subn = min(bq, KEY_SUB)
    subp = KEY_SUB
    assert bq % subn == 0 and (bq if past_len is None else past_len) % subp == 0
    heads = LANES // hd
    qmap = lambda bb, p, qi: (bb, qi, p)
    cmap = lambda bb, p, qi: (0, 0)
    past_buf = (2, LANES, subp) if past_transposed else (2, subp, LANES)
    return pl.pallas_call(
        functools.partial(_attn_kernel, bq=bq, subn=subn, subp=subp, hd=hd, past_len=past_len,
                          past_layer=past_layer, past_transposed=past_transposed),
        grid=(b, w // LANES, tq // bq),
        in_specs=[pl.BlockSpec((None, bq, LANES), qmap),
                  pl.BlockSpec((None, bq, LANES), qmap), pl.BlockSpec((None, bq, LANES), qmap),
                  pl.BlockSpec(memory_space=pl.ANY), pl.BlockSpec(memory_space=pl.ANY),
                  pl.BlockSpec((subn, subn), cmap), pl.BlockSpec((subp, subp), cmap),
                  pl.BlockSpec((1, LANES), cmap)],
        out_specs=pl.BlockSpec((None, bq, LANES), qmap),
        out_shape=jax.ShapeDtypeStruct((b, tq, w), BF16),
        scratch_shapes=[pltpu.VMEM((heads, bq, LANES), BF16), pltpu.VMEM((heads, bq, LANES), F32),
                        pltpu.VMEM((heads, bq, LANES), F32),
                        pltpu.VMEM(past_buf, k_past.dtype), pltpu.VMEM(past_buf, v_past.dtype),
                        pltpu.SemaphoreType.DMA((2, 2)), pltpu.SMEM((heads * (bq // subn),), jnp.int32)],
        compiler_params=_cparams(("parallel", "parallel", "arbitrary")),
        name="stickbreak_attn",
    )(q, k_new, v_new, k_past, v_past, _tril_strict(subn), _tril_strict(subp), gain)


def _hgrn_halvings(c):
    return [c >> (i + 1) for i in range(int(np.log2(c)))]


def _hgrn_masks(c):
    t = np.arange(c)
    masks = [np.eye(c, dtype=bool)]
    for h in _hgrn_halvings(c):
        blk = t // (2 * h)
        second = (t // h) % 2 == 1
        masks.append((blk[:, None] == blk[None, :]) & second[:, None] & (~second)[None, :])
    return jnp.asarray(np.stack(masks).astype(np.float32))


def _boundary_rows(b, h, row):
    c, dk = b.shape
    if 2 * h >= 8:
        n = c // (2 * h)
        ref = b.reshape(n, 2 * h, dk)[:, h - 1:h, :]
        return jnp.broadcast_to(ref, (n, 2 * h, dk)).reshape(c, dk)
    down1 = pltpu.roll(b, 1, 0)
    if h == 1:
        return jnp.where((row & 1) == 1, down1, b)
    m = row & 3
    up1 = pltpu.roll(b, c - 1, 0)
    down2 = pltpu.roll(b, 2, 0)
    return jnp.where(m == 0, up1, jnp.where(m == 1, b, jnp.where(m == 2, down1, down2)))


def _hgrn_kernel(q_ref, k_ref, lf_ref, v_ref, g_ref, s0_ref, tri_ref, msk_ref, gain_ref,
                 y_ref, sout_ref, st_ref, *, c, n_chunks):
    t = pl.program_id(2)
    dk = q_ref.shape[-1]

    @pl.when(t == 0)
    def _():
        st_ref[...] = s0_ref[...].T

    row = lax.broadcasted_iota(jnp.int32, (c, dk), 0)
    tri = tri_ref[...]
    for ci in range(n_chunks):
        sl = slice(ci * c, (ci + 1) * c)
        q = q_ref[sl, :]
        k = k_ref[sl, :]
        lf = lf_ref[sl, :]
        v = v_ref[sl, :].astype(BF16)
        g = g_ref[sl, :]
        hi = lf.astype(BF16)
        r1 = lf - hi.astype(F32)
        mid = r1.astype(BF16)
        lo = (r1 - mid.astype(F32)).astype(BF16)
        b = _dot(tri, hi) + _dot(tri, mid) + _dot(tri, lo)
        st = st_ref[...]
        o = _dot_nt((q * jnp.exp(b)).astype(BF16), st.astype(BF16))
        a = msk_ref[0] * _dot_nt(q.astype(BF16), k.astype(BF16))
        for lv, h in enumerate(_hgrn_halvings(c)):
            gap = b - _boundary_rows(b, h, row)
            el = jnp.exp(jnp.where((row & h) != 0, gap, -gap))
            a += msk_ref[1 + lv] * _dot_nt((q * el).astype(BF16), (k * el).astype(BF16))
        o += _dot(a.astype(BF16), v)
        b_last = b[c - 1:c, :]
        k_end = (k * jnp.exp(b_last - b)).astype(BF16)
        st_ref[...] = st * jnp.exp(b_last) + _dot_tn(v, k_end)
        y = (_rms(o) * gain_ref[...]) * (g * _sigmoid(g))
        y_ref[sl, :] = y.astype(y_ref.dtype)

    @pl.when(t == pl.num_programs(2) - 1)
    def _():
        sout_ref[...] = st_ref[...].T


def _hgrn(q, k, lf, v, g, s0, gain, *, dk):
    b, t, hw = q.shape
    nh = hw // dk
    c = _tile(t, 256)
    tt = _tile(t, 512)
    msk = _hgrn_masks(c)
    tri = jnp.asarray(np.tril(np.ones((c, c), np.float32)), BF16)
    tok = lambda bb, h, ti: (bb, ti, h)
    smap = lambda bb, h, ti: (bb, h, 0, 0)
    return pl.pallas_call(
        functools.partial(_hgrn_kernel, c=c, n_chunks=tt // c),
        grid=(b, nh, t // tt),
        in_specs=[pl.BlockSpec((None, tt, dk), tok)] * 5
        + [pl.BlockSpec((None, None, dk, dk), smap),
           pl.BlockSpec((c, c), lambda bb, h, ti: (0, 0)),
           pl.BlockSpec(msk.shape, lambda bb, h, ti: (0, 0, 0)),
           pl.BlockSpec((1, dk), lambda bb, h, ti: (0, 0))],
        out_specs=[pl.BlockSpec((None, tt, dk), tok), pl.BlockSpec((None, None, dk, dk), smap)],
        out_shape=[jax.ShapeDtypeStruct((b, t, hw), BF16), jax.ShapeDtypeStruct((b, nh, dk, dk), F32)],
        scratch_shapes=[pltpu.VMEM((dk, dk), F32)],
        compiler_params=_cparams(("parallel", "parallel", "arbitrary")),
        name="hgrn2",
    )(q, k, lf, v, g, s0, tri, msk, gain)


def _outproj_kernel(x_ref, ya_ref, yb_ref, wa_ref, wb_ref, o_ref):
    o_ref[...] = x_ref[...] + _dot(ya_ref[...], wa_ref[...]) + _dot(yb_ref[...], wb_ref[...])


def _outproj(x, ya, yb, wa, wb):
    n, d = x.shape
    tm = _tile(n, 512)
    row = lambda i: (i, 0)
    const = lambda i: (0, 0)
    return pl.pallas_call(
        _outproj_kernel,
        grid=(n // tm,),
        in_specs=[pl.BlockSpec((tm, d), row), pl.BlockSpec((tm, ya.shape[1]), row),
                  pl.BlockSpec((tm, yb.shape[1]), row), pl.BlockSpec(wa.shape, const),
                  pl.BlockSpec(wb.shape, const)],
        out_specs=pl.BlockSpec((tm, d), row),
        out_shape=jax.ShapeDtypeStruct((n, d), F32),
        compiler_params=_cparams(("parallel",)),
        name="outproj",
    )(x, ya, yb, wa, wb)


def _ffn_kernel(x_ref, g_ref, wg_ref, wu_ref, wd_ref, o_ref, h_ref, acc_ref):
    f = pl.program_id(1)

    @pl.when(f == 0)
    def _():
        h_ref[...] = (_rms(x_ref[...]) * g_ref[...]).astype(BF16)
        acc_ref[...] = jnp.zeros_like(acc_ref)

    h = h_ref[...]
    gate = _dot(h, wg_ref[...])
    up = _dot(h, wu_ref[...])
    acc_ref[...] += _dot(((gate * _sigmoid(gate)) * up).astype(BF16), wd_ref[...])

    @pl.when(f == pl.num_programs(1) - 1)
    def _():
        o_ref[...] = x_ref[...] + acc_ref[...]


def _ffn(x, g, wg, wu, wd):
    n, d = x.shape
    ff = wg.shape[1]
    tm = _tile(n, 1024)
    tf = _tile(ff, 256)
    return pl.pallas_call(
        _ffn_kernel,
        grid=(n // tm, ff // tf),
        in_specs=[pl.BlockSpec((tm, d), lambda i, f: (i, 0)), pl.BlockSpec((1, d), lambda i, f: (0, 0)),
                  pl.BlockSpec((d, tf), lambda i, f: (0, f)), pl.BlockSpec((d, tf), lambda i, f: (0, f)),
                  pl.BlockSpec((tf, d), lambda i, f: (f, 0))],
        out_specs=pl.BlockSpec((tm, d), lambda i, f: (i, 0)),
        out_shape=jax.ShapeDtypeStruct((n, d), F32),
        scratch_shapes=[pltpu.VMEM((tm, d), BF16), pltpu.VMEM((tm, d), F32)],
        compiler_params=_cparams(("parallel", "arbitrary")),
        name="ffn_dense",
    )(x, g, wg, wu, wd)


ROUTE_E1, ROUTE_E2, ROUTE_R1, ROUTE_R2, ROUTE_G1, ROUTE_G2 = range(6)


def _router_kernel(x_ref, g_ref, rhi_ref, rlo_ref, tri_ref, route_ref, cnt_ref, base_ref, *, n_experts):
    i = pl.program_id(0)
    lane = lax.broadcasted_iota(jnp.int32, (1, LANES), 1)

    @pl.when(i == 0)
    def _():
        base_ref[...] = jnp.zeros_like(base_ref)

    hf = _rms(x_ref[...]) * g_ref[...]
    h_hi = hf.astype(BF16)
    h_lo = (hf - h_hi.astype(F32)).astype(BF16)
    logits = _dot(h_hi, rhi_ref[...]) + _dot(h_lo, rhi_ref[...]) + _dot(h_hi, rlo_ref[...])
    logits = jnp.where(lane < n_experts, logits, -jnp.inf)
    m1 = jnp.max(logits, axis=-1, keepdims=True)
    i1 = jnp.min(jnp.where(logits == m1, lane, LANES), axis=-1, keepdims=True)
    rest = jnp.where(lane == i1, -jnp.inf, logits)
    m2 = jnp.max(rest, axis=-1, keepdims=True)
    i2 = jnp.min(jnp.where(rest == m2, lane, LANES), axis=-1, keepdims=True)
    e2 = jnp.exp(m2 - m1)
    g1 = 1.0 / (1.0 + e2)
    g2 = e2 / (1.0 + e2)
    oh1 = (lane == i1).astype(F32)
    oh2 = (lane == i2).astype(F32)
    both = oh1 + oh2
    before = _dot(tri_ref[...], both.astype(BF16)) + base_ref[...]
    r1 = jnp.sum(oh1 * before, axis=-1, keepdims=True)
    r2 = jnp.sum(oh2 * before, axis=-1, keepdims=True)
    rec = jnp.zeros(route_ref.shape, F32)
    for slot, val in ((ROUTE_E1, i1.astype(F32)), (ROUTE_E2, i2.astype(F32)), (ROUTE_R1, r1),
                      (ROUTE_R2, r2), (ROUTE_G1, g1), (ROUTE_G2, g2)):
        rec = jnp.where(lane == slot, val, rec)
    route_ref[...] = rec
    base_ref[...] += jnp.sum(both, axis=0, keepdims=True)
    cnt_ref[...] = base_ref[...]


def _router(x, g, router):
    n, d = x.shape
    ne = router.shape[1]
    tm = _tile(n, 1024)
    rpad = jnp.zeros((d, LANES), F32).at[:, :ne].set(router)
    rhi = rpad.astype(BF16)
    rlo = (rpad - rhi.astype(F32)).astype(BF16)
    const = lambda i: (0, 0)
    return pl.pallas_call(
        functools.partial(_router_kernel, n_experts=ne),
        grid=(n // tm,),
        in_specs=[pl.BlockSpec((tm, d), lambda i: (i, 0)), pl.BlockSpec((1, d), const),
                  pl.BlockSpec((d, LANES), const), pl.BlockSpec((d, LANES), const),
                  pl.BlockSpec((tm, tm), const)],
        out_specs=[pl.BlockSpec((tm, LANES), lambda i: (i, 0)), pl.BlockSpec((1, LANES), const)],
        out_shape=[jax.ShapeDtypeStruct((n, LANES), F32), jax.ShapeDtypeStruct((1, LANES), F32)],
        scratch_shapes=[pltpu.VMEM((1, LANES), F32)],
        compiler_params=_cparams(("arbitrary",)),
        name="moe_router",
    )(x, g, rhi, rlo, _tril_strict(tm))


def _row_copy(src_ref, src_row, dst_ref, dst_row, sem):
    return pltpu.make_async_copy(src_ref.at[pl.ds(src_row, 1), :], dst_ref.at[pl.ds(dst_row, 1), :], sem)


def _dispatch_kernel(pad_lo_ref, pad_hi_ref, p1_ref, p2_ref, x_ref, xs_ref, sem, pad_sem, *, tb, n_pad_ranges):
    def issue(t, carry):
        _row_copy(x_ref, t, xs_ref, p1_ref[0, 0, t], sem).start(priority=0)
        _row_copy(x_ref, t, xs_ref, p2_ref[0, 0, t], sem).start(priority=1)
        return carry

    lax.fori_loop(0, tb, issue, 0, unroll=8)

    @pl.when(pl.program_id(0) == 0)
    def _():
        def fill(r, carry):
            _row_copy(x_ref, 0, xs_ref, r, pad_sem).start()
            return carry

        def drain(r, carry):
            _row_copy(x_ref, 0, xs_ref, r, pad_sem).wait()
            return carry

        for e in range(n_pad_ranges):
            lax.fori_loop(pad_lo_ref[e], pad_hi_ref[e], fill, 0)
        for e in range(n_pad_ranges):
            lax.fori_loop(pad_lo_ref[e], pad_hi_ref[e], drain, 0)

    for _ in range(TOP_K):
        pltpu.make_async_copy(x_ref, xs_ref.at[pl.ds(0, tb), :], sem).wait()


def _dispatch(x, pos1, pos2, n_rows, pad_lo, pad_hi):
    n, d = x.shape
    tb = pos1.shape[-1]
    smem = lambda: pl.BlockSpec((1, 1, tb), lambda i, lo, hi: (i, 0, 0), memory_space=pltpu.SMEM)
    grid_spec = pltpu.PrefetchScalarGridSpec(
        num_scalar_prefetch=2,
        grid=(n // tb,),
        in_specs=[smem(), smem(), pl.BlockSpec((tb, d), lambda i, lo, hi: (i, 0))],
        out_specs=pl.BlockSpec(memory_space=pl.ANY),
        scratch_shapes=[pltpu.SemaphoreType.DMA(()), pltpu.SemaphoreType.DMA(())],
    )
    return pl.pallas_call(
        functools.partial(_dispatch_kernel, tb=tb, n_pad_ranges=pad_lo.shape[0]),
        grid_spec=grid_spec,
        out_shape=jax.ShapeDtypeStruct((n_rows, d), F32),
        compiler_params=_cparams(("arbitrary",)),
        name="moe_dispatch",
    )(pad_lo, pad_hi, pos1, pos2, x)


def _experts_kernel(te_ref, nv_ref, x_ref, g_ref, wg_ref, wu_ref, wd_ref, o_ref, h_ref, acc_ref):
    del te_ref
    i = pl.program_id(0)
    f = pl.program_id(1)
    valid = i < nv_ref[0]

    @pl.when(valid & (f == 0))
    def _():
        h_ref[...] = (_rms(x_ref[...]) * g_ref[...]).astype(BF16)
        acc_ref[...] = jnp.zeros_like(acc_ref)

    @pl.when(valid)
    def _():
        h = h_ref[...]
        gate = _dot(h, wg_ref[...])
        up = _dot(h, wu_ref[...])
        acc_ref[...] += _dot(((gate * _sigmoid(gate)) * up).astype(BF16), wd_ref[...])

    @pl.when(valid & (f == pl.num_programs(1) - 1))
    def _():
        o_ref[...] = acc_ref[...]

    @pl.when(jnp.logical_not(valid) & (f == pl.num_programs(1) - 1))
    def _():
        o_ref[...] = jnp.zeros_like(o_ref)


def _experts(xs, g, wg, wu, wd, tile_expert, n_valid, tr):
    n_rows, d = xs.shape
    ff = wg.shape[2]
    tf = _tile(ff, 256)
    nf = ff // tf
    rowmap = lambda i, f, te, nv: (jnp.minimum(i, nv[0] - 1), 0)
    fsel = lambda i, f, nv: jnp.where(i < nv[0], f, nf - 1)
    grid_spec = pltpu.PrefetchScalarGridSpec(
        num_scalar_prefetch=2,
        grid=(n_rows // tr, nf),
        in_specs=[pl.BlockSpec((tr, d), rowmap), pl.BlockSpec((1, d), lambda i, f, te, nv: (0, 0)),
                  pl.BlockSpec((None, d, tf), lambda i, f, te, nv: (te[i], 0, fsel(i, f, nv))),
                  pl.BlockSpec((None, d, tf), lambda i, f, te, nv: (te[i], 0, fsel(i, f, nv))),
                  pl.BlockSpec((None, tf, d), lambda i, f, te, nv: (te[i], fsel(i, f, nv), 0))],
        out_specs=pl.BlockSpec((tr, d), lambda i, f, te, nv: (i, 0)),
        scratch_shapes=[pltpu.VMEM((tr, d), BF16), pltpu.VMEM((tr, d), F32)],
    )
    return pl.pallas_call(
        _experts_kernel,
        grid_spec=grid_spec,
        out_shape=jax.ShapeDtypeStruct((n_rows, d), F32),
        compiler_params=_cparams(("arbitrary", "arbitrary")),
        name="moe_experts",
    )(tile_expert, n_valid, xs, g, wg, wu, wd)


def _combine_kernel(p1_ref, p2_ref, x_ref, route_ref, ys_ref, o_ref, buf_ref, sem, *, tc):
    def issue(t, carry):
        _row_copy(ys_ref, p1_ref[0, 0, t], buf_ref.at[0], t, sem).start(priority=0)
        _row_copy(ys_ref, p2_ref[0, 0, t], buf_ref.at[1], t, sem).start(priority=1)
        return carry

    lax.fori_loop(0, tc, issue, 0, unroll=8)
    for slot in range(TOP_K):
        pltpu.make_async_copy(ys_ref.at[pl.ds(0, tc), :], buf_ref.at[slot], sem).wait()
    lane = lax.broadcasted_iota(jnp.int32, (1, LANES), 1)
    route = route_ref[...]
    g1 = jnp.sum(jnp.where(lane == ROUTE_G1, route, 0.0), axis=-1, keepdims=True)
    g2 = jnp.sum(jnp.where(lane == ROUTE_G2, route, 0.0), axis=-1, keepdims=True)
    o_ref[...] = x_ref[...] + (g1 * buf_ref[0] + g2 * buf_ref[1])


def _combine(x, route, ys, pos1, pos2):
    n, d = x.shape
    tc = pos1.shape[-1]
    smem = lambda: pl.BlockSpec((1, 1, tc), lambda i: (i, 0, 0), memory_space=pltpu.SMEM)
    return pl.pallas_call(
        functools.partial(_combine_kernel, tc=tc),
        grid=(n // tc,),
        in_specs=[smem(), smem(), pl.BlockSpec((tc, d), lambda i: (i, 0)),
                  pl.BlockSpec((tc, LANES), lambda i: (i, 0)), pl.BlockSpec(memory_space=pl.ANY)],
        out_specs=pl.BlockSpec((tc, d), lambda i: (i, 0)),
        out_shape=jax.ShapeDtypeStruct((n, d), F32),
        scratch_shapes=[pltpu.VMEM((TOP_K, tc, d), F32), pltpu.SemaphoreType.DMA(())],
        compiler_params=_cparams(("arbitrary",)),
        name="moe_combine",
    )(pos1, pos2, x, route, ys)


def _moe(x, g, router, wg, wu, wd):
    n, d = x.shape
    ne = wg.shape[0]
    tr = 1024 if n >= 8192 else 256
    tb = _tile(n, 512)
    route, cnt = _router(x, g, router)
    counts = cnt[0, :ne].astype(jnp.int32)
    padded = ((counts + tr - 1) // tr) * tr
    ends = jnp.cumsum(padded)
    starts = ends - padded
    e1 = route[:, ROUTE_E1].astype(jnp.int32)
    e2 = route[:, ROUTE_E2].astype(jnp.int32)
    pos1 = (starts[e1] + route[:, ROUTE_R1].astype(jnp.int32)).reshape(n // tb, 1, tb)
    pos2 = (starts[e2] + route[:, ROUTE_R2].astype(jnp.int32)).reshape(n // tb, 1, tb)
    max_tiles = -(-(TOP_K * n + ne * (tr - 1)) // tr)
    n_valid = (ends[-1] // tr).reshape(1)
    tile_start = jnp.minimum(jnp.arange(max_tiles, dtype=jnp.int32), n_valid[0] - 1) * tr
    tile_expert = jnp.minimum(jnp.searchsorted(ends, tile_start, side="right"), ne - 1).astype(jnp.int32)
    n_rows = max_tiles * tr
    pad_lo = jnp.concatenate([starts + counts, ends[-1:]]).astype(jnp.int32)
    pad_hi = jnp.concatenate([ends, jnp.full((1,), n_rows, ends.dtype)]).astype(jnp.int32)
    xs = _dispatch(x, pos1, pos2, n_rows, pad_lo, pad_hi)
    ys = _experts(xs, g, wg, wu, wd, tile_expert, n_valid.astype(jnp.int32), tr)
    return _combine(x, route, ys, pos1, pos2)


def kernel(x_prompt, x_sample, cache_k, cache_v, state_hgrn, norm_mix, w_in, sb_q_gain, sb_k_gain,
           hg_lower_bounds, sb_out_gain, hg_out_gain, w_out, norm_ffn, ffn_w_gate, ffn_w_up, ffn_w_down,
           moe_router, moe_w_gate, moe_w_up, moe_w_down):
    depth = w_in.shape[0]
    bp, tp, d = x_prompt.shape
    bs, ts, _ = x_sample.shape
    past = cache_k.shape[2]
    sb_heads, hd = cache_k.shape[3], cache_k.shape[4]
    sw = sb_heads * hd
    dk = hg_out_gain.shape[1]
    hw = hg_lower_bounds.shape[1]
    hg_heads = hw // dk
    sb_scale = hd ** -0.5

    lbs = jnp.cumsum(jax.nn.softmax(hg_lower_bounds.astype(F32), axis=0), axis=0)
    lbs = lbs - lbs[0:1]

    xp = x_prompt.reshape(bp * tp, d)
    xs = x_sample.reshape(bs * ts, d)
    zeros_state = jnp.zeros((bp, hg_heads, dk, dk), F32)
    outs = {k: [] for k in ("sp", "ss")}
    cache_kt = jnp.transpose(cache_k, (0, 1, 3, 4, 2)).reshape(depth, bs, sw, past)
    cache_vt = jnp.transpose(cache_v, (0, 1, 3, 4, 2)).reshape(depth, bs, sw, past)

    def mixer(x, b, t, l, kv_prev, *, cached, past_len, s0, bq):
        qg = (jnp.tile(sb_q_gain[l], sb_heads) * sb_scale)[None, :]
        kg = jnp.tile(sb_k_gain[l], sb_heads)[None, :]
        q, kf, kb, vf, vb, qh, kh, lf, ih, gh = _inproj(
            x, norm_mix[l][None, :], w_in[l].astype(BF16), qg, kg,
            jnp.log(lbs[l])[None, :], jnp.log1p(-lbs[l])[None, :], (1.0 - lbs[l])[None, :], kv_prev,
            sw=sw, hw=hw, hd=hd, seq_len=t)
        r3 = lambda a: a.reshape(b, t, a.shape[-1])
        kb3, vb3 = r3(kb), r3(vb)
        ya = _attention(r3(q), kb3, vb3, cache_kt if cached else kb3, cache_vt if cached else vb3,
                        jnp.tile(sb_out_gain[l], LANES // hd)[None, :],
                        bq=bq, past_len=past_len, hd=hd, past_layer=l if cached else None)
        yb, s_fin = _hgrn(r3(qh), r3(kh), r3(lf), r3(ih), r3(gh), s0, hg_out_gain[l][None, :], dk=dk)
        wo = w_out[l].astype(BF16)
        x1 = _outproj(x, ya.reshape(b * t, sw), yb.reshape(b * t, hw), wo[:sw], wo[sw:])
        return x1, (kf, vf), s_fin

    def channel(x, l):
        j = l // 2
        if l % 2 == 0:
            return _ffn(x, norm_ffn[l][None, :], ffn_w_gate[j].astype(BF16), ffn_w_up[j].astype(BF16),
                        ffn_w_down[j].astype(BF16))
        return _moe(x, norm_ffn[l][None, :], moe_router[j], moe_w_gate[j].astype(BF16),
                    moe_w_up[j].astype(BF16), moe_w_down[j].astype(BF16))

    bq_p = _tile(tp, 512)
    kv_p = kv_s = None
    for l in range(depth):
        xp, kv_p, s_p = mixer(xp, bp, tp, l, kv_p, cached=False, past_len=None, s0=zeros_state, bq=bq_p)
        outs["sp"].append(s_p)
        xs, kv_s, s_s = mixer(xs, bs, ts, l, kv_s, cached=True, past_len=past,
                              s0=state_hgrn[l].astype(F32), bq=ts)
        outs["ss"].append(s_s)
        xp = channel(xp, l)
        xs = channel(xs, l)

    heads5 = lambda a, b, t: jnp.transpose(a.reshape(depth, b, sb_heads, hd, t), (0, 1, 4, 2, 3))
    return (xp.reshape(bp, tp, d), xs.reshape(bs, ts, d), heads5(kv_p[0], bp, tp), heads5(kv_p[1], bp, tp),
            jnp.stack(outs["sp"]), heads5(kv_s[0], bs, ts), heads5(kv_s[1], bs, ts), jnp.stack(outs["ss"]))
```

```python
import functools

import numpy as np
import jax
import jax.numpy as jnp
from jax import lax
from jax.experimental import pallas as pl
from jax.experimental.pallas import tpu as pltpu

EPS = 1e-6
TOP_K = 2
LANES = 128
F32 = jnp.float32
BF16 = jnp.bfloat16
VMEM_LIMIT_BYTES = 56 * 1024 * 1024


def _cparams(semantics):
    return pltpu.CompilerParams(dimension_semantics=semantics, vmem_limit_bytes=VMEM_LIMIT_BYTES)


def _dot(a, b):
    return jnp.dot(a, b, preferred_element_type=F32)


def _dot_nt(a, b):
    return lax.dot_general(a, b, (((1,), (1,)), ((), ())), preferred_element_type=F32)


def _dot_tn(a, b):
    return lax.dot_general(a, b, (((0,), (0,)), ((), ())), preferred_element_type=F32)


def _sigmoid(x):
    return 1.0 / (1.0 + jnp.exp(-x))


def _rms(x):
    return x * lax.rsqrt(jnp.mean(x * x, axis=-1, keepdims=True) + EPS)


def _tile(n, pref):
    t = min(n, pref)
    assert n % t == 0, (n, t)
    return t


def _inproj_kernel(*refs, sw, hw, hd, n_prev, seq_per_tile):
    x_ref, g_ref, w_ref, qg_ref, kg_ref, llb_ref, l1m_ref, oml_ref = refs[:8]
    prev = refs[8:8 + (2 if n_prev else 0)]
    q_ref, kf_ref, kb_ref, vf_ref, vb_ref, qh_ref, kh_ref, lf_ref, ih_ref, gh_ref = refs[8 + len(prev):]
    if n_prev:
        kf_ref[:n_prev] = prev[0][...]
        vf_ref[:n_prev] = prev[1][...]

    def store_time_minor(ref, y):
        t = y.shape[0] // seq_per_tile
        for sq in range(seq_per_tile):
            ref[n_prev, sq] = y[sq * t:(sq + 1) * t, :].T

    h = (_rms(x_ref[...]) * g_ref[...]).astype(BF16)

    def seg(lo, width):
        return _dot(h, w_ref[:, lo:lo + width])

    def headnorm(y, gain):
        lane = lax.broadcasted_iota(jnp.int32, (1, LANES), 1)
        tiles = []
        for c0 in range(0, y.shape[1], LANES):
            y2 = y[:, c0:c0 + LANES] * y[:, c0:c0 + LANES]
            m = jnp.zeros_like(y2)
            for h0 in range(0, LANES, hd):
                in_head = (lane >= h0) & (lane < h0 + hd)
                m = jnp.where(in_head, jnp.sum(jnp.where(in_head, y2, 0.0), axis=-1, keepdims=True), m)
            tiles.append(m * (1.0 / hd))
        m = jnp.concatenate(tiles, axis=1)
        return (y * lax.rsqrt(m + EPS)) * gain

    q_ref[...] = headnorm(seg(0, sw), qg_ref[...]).astype(BF16)
    ka = headnorm(seg(sw, sw), kg_ref[...])
    store_time_minor(kf_ref, ka)
    kb_ref[...] = ka.astype(BF16)
    va = seg(2 * sw, sw)
    store_time_minor(vf_ref, va)
    vb_ref[...] = va.astype(BF16)

    o = 3 * sw
    qb = seg(o, hw)
    qh_ref[...] = qb * _sigmoid(qb)

    fb = seg(o + hw, hw)
    e = jnp.exp(-jnp.abs(fb))
    log_sig = jnp.minimum(fb, 0.0) - jnp.log(1.0 + e)
    c = l1m_ref[...] + log_sig
    a = llb_ref[...]
    lf_ref[...] = jnp.maximum(a, c) + jnp.log(1.0 + jnp.exp(-jnp.abs(a - c)))
    kh_ref[...] = oml_ref[...] * (jnp.where(fb >= 0.0, e, 1.0) / (1.0 + e))

    ih_ref[...] = seg(o + 2 * hw, hw)
    gh_ref[...] = seg(o + 3 * hw, hw)


def _inproj(x, g, w, qg, kg, llb, l1m, oml, kv_prev, *, sw, hw, hd, seq_len):
    n, d = x.shape
    tm = _tile(n, 512)
    n_prev = 0 if kv_prev is None else kv_prev[0].shape[0]
    spt = max(tm // seq_len, 1)
    tt = tm // spt
    nt = seq_len // tt
    assert spt * tt == tm and nt * tt == seq_len
    row = lambda i: (i, 0)
    const = lambda i: (0, 0)
    stacked = lambda i: (0, i // nt, 0, i % nt)
    kv_shape = (n_prev + 1, n // seq_len, sw, seq_len)
    specs = {"row_sw": pl.BlockSpec((tm, sw), row), "row_hw": pl.BlockSpec((tm, hw), row),
             "stack": pl.BlockSpec((n_prev + 1, spt, sw, tt), stacked)}
    outs = [("row_sw", (n, sw), BF16), ("stack", kv_shape, F32), ("row_sw", (n, sw), BF16),
            ("stack", kv_shape, F32), ("row_sw", (n, sw), BF16)] + [("row_hw", (n, hw), F32)] * 5
    prev_specs = [pl.BlockSpec((n_prev, spt, sw, tt), stacked)] * 2 if n_prev else []
    return pl.pallas_call(
        functools.partial(_inproj_kernel, sw=sw, hw=hw, hd=hd, n_prev=n_prev, seq_per_tile=spt),
        grid=(n // tm,),
        in_specs=[pl.BlockSpec((tm, d), row), pl.BlockSpec((1, d), const),
                  pl.BlockSpec(w.shape, const),
                  pl.BlockSpec((1, sw), const), pl.BlockSpec((1, sw), const),
                  pl.BlockSpec((1, hw), const), pl.BlockSpec((1, hw), const), pl.BlockSpec((1, hw), const)]
        + prev_specs,
        out_specs=[specs[kind] for kind, _, _ in outs],
        out_shape=[jax.ShapeDtypeStruct(shape, dt) for _, shape, dt in outs],
        compiler_params=_cparams(("parallel",)),
        name="inproj",
    )(x, g, w, qg, kg, llb, l1m, oml, *(kv_prev or ()))


KEY_SUB = 256
SOFTPLUS_CLAMP = 80.0
CARRY_DEAD = 104.0


def _attn_kernel(q_ref, kn_ref, vn_ref, kp_hbm, vp_hbm, un_ref, up_ref, gain_ref,
                 y_ref, qm_ref, acc_ref, carry_ref, kbuf, vbuf, sem, alive_ref,
                 *, bq, subn, subp, hd, past_len, past_layer, past_transposed):
    bb = pl.program_id(0)
    p = pl.program_id(1)
    qi = pl.program_id(2)
    lane = lax.broadcasted_iota(jnp.int32, (1, LANES), 1)
    heads = LANES // hd
    n_past = (qi * bq if past_len is None else past_len) // subp

    def fetch(j, slot):
        if past_transposed:
            window = (past_layer, bb, pl.ds(p * LANES, LANES), pl.ds(j * subp, subp))
        else:
            window = (bb, pl.ds(j * subp, subp), pl.ds(p * LANES, LANES))
        return (pltpu.make_async_copy(kp_hbm.at[window], kbuf.at[slot], sem.at[0, slot]),
                pltpu.make_async_copy(vp_hbm.at[window], vbuf.at[slot], sem.at[1, slot]))

    def slot_of(j):
        return (n_past - 1 - j) & 1

    @pl.when(n_past > 0)
    def _():
        for cp in fetch(n_past - 1, 0):
            cp.start()

    q = q_ref[...]
    for hh in range(heads):
        in_head = (lane >= hh * hd) & (lane < (hh + 1) * hd)
        qm_ref[hh] = jnp.where(in_head, q, jnp.zeros_like(q))
    acc_ref[...] = jnp.zeros_like(acc_ref)
    carry_ref[...] = jnp.zeros_like(carry_ref)

    groups = bq // subn

    def one_head(hh, k, v, u, r0, r1, mask, transposed):
        if transposed:
            z = _dot(qm_ref[hh, r0:r1, :], k)
        else:
            z = _dot_nt(qm_ref[hh, r0:r1, :], k)
        sp = jnp.maximum(jnp.log(1.0 + jnp.exp(jnp.minimum(z, SOFTPLUS_CLAMP))), z)
        if mask is not None:
            sp = jnp.where(mask, sp, 0.0)
        after = _dot(sp.astype(BF16), u)
        w = jnp.exp((z - sp) - after)
        if mask is not None:
            w = jnp.where(mask, w, 0.0)
        pv = _dot_nt(w.astype(BF16), v) if transposed else _dot(w.astype(BF16), v)
        c = carry_ref[hh, r0:r1, :]
        acc_ref[hh, r0:r1, :] += jnp.exp(-c) * pv
        c_new = c + jnp.sum(sp, axis=-1, keepdims=True)
        carry_ref[hh, r0:r1, :] = c_new
        return c_new

    def set_alive(hh, g, c_group):
        alive_ref[hh * groups + g] = (jnp.min(c_group) < CARRY_DEAD).astype(jnp.int32)

    u_new = un_ref[...]
    for j in range(groups - 1, -1, -1):
        r0 = j * subn
        rows = r0 + lax.broadcasted_iota(jnp.int32, (bq - r0, subn), 0)
        cols = r0 + lax.broadcasted_iota(jnp.int32, (bq - r0, subn), 1)
        k = kn_ref[r0:r0 + subn, :].astype(BF16)
        v = vn_ref[r0:r0 + subn, :].astype(BF16)
        for hh in range(heads):
            c_new = one_head(hh, k, v, u_new, r0, bq, cols < rows, False)
            if j == 0:
                for g in range(groups):
                    set_alive(hh, g, c_new[g * subn:(g + 1) * subn])

    def any_alive():
        total = alive_ref[0]
        for i in range(1, heads * groups):
            total += alive_ref[i]
        return total > 0

    def past_block(j):
        slot = slot_of(j)
        for cp in fetch(j, slot):
            cp.wait()

        @pl.when(j > 0)
        def _():
            for cp in fetch(j - 1, 1 - slot):
                cp.start()

        u_past = up_ref[...]

        def group_heads(g, head_list):
            k = kbuf[slot].astype(BF16)
            v = vbuf[slot].astype(BF16)
            for hh in head_list:
                c_new = one_head(hh, k, v, u_past, g * subn, (g + 1) * subn, None, past_transposed)
                set_alive(hh, g, c_new)

        for g in range(groups):
            n_alive = alive_ref[g]
            for hh in range(1, heads):
                n_alive += alive_ref[hh * groups + g]
            pl.when(n_alive == heads)(functools.partial(group_heads, g, range(heads)))
            for hh in range(heads):
                pl.when((n_alive < heads) & (alive_ref[hh * groups + g] == 1))(
                    functools.partial(group_heads, g, (hh,)))
        return j - 1

    j_end = lax.while_loop(lambda j: (j >= 0) & any_alive(), past_block, n_past - 1)

    @pl.when(j_end >= 0)
    def _():
        for cp in fetch(j_end, slot_of(j_end)):
            cp.wait()

    o = acc_ref[0]
    for hh in range(1, heads):
        in_head = (lane >= hh * hd) & (lane < (hh + 1) * hd)
        o = jnp.where(in_head, acc_ref[hh], o)
    o2 = o * o
    ms = jnp.zeros_like(o)
    for hh in range(heads):
        in_head = (lane >= hh * hd) & (lane < (hh + 1) * hd)
        ssum = jnp.sum(jnp.where(in_head, o2, 0.0), axis=-1, keepdims=True)
        ms = jnp.where(in_head, ssum * (1.0 / hd), ms)
    y_ref[...] = ((o * lax.rsqrt(ms + EPS)) * gain_ref[...]).astype(y_ref.dtype)


def _tril_strict(n):
    return jnp.asarray(np.tril(np.ones((n, n), np.float32), -1), BF16)


def _attention(q, k_new, v_new, k_past, v_past, gain, *, bq, past_len, hd, past_layer=None):
    b, tq, w = q.shape
    past_transposed = past_layer is not None
    subn = min(bq, KEY_SUB)
    subp = KEY_SUB
    assert bq % subn == 0 and (bq if past_len is None else past_len) % subp == 0
    heads = LANES // hd
    qmap = lambda bb, p, qi: (bb, qi, p)
    cmap = lambda bb, p, qi: (0, 0)
    past_buf = (2, LANES, subp) if past_transposed else (2, subp, LANES)
    return pl.pallas_call(
        functools.partial(_attn_kernel, bq=bq, subn=subn, subp=subp, hd=hd, past_len=past_len,
                          past_layer=past_layer, past_transposed=past_transposed),
        grid=(b, w // LANES, tq // bq),
        in_specs=[pl.BlockSpec((None, bq, LANES), qmap),
                  pl.BlockSpec((None, bq, LANES), qmap), pl.BlockSpec((None, bq, LANES), qmap),
                  pl.BlockSpec(memory_space=pl.ANY), pl.BlockSpec(memory_space=pl.ANY),
                  pl.BlockSpec((subn, subn), cmap), pl.BlockSpec((subp, subp), cmap),
                  pl.BlockSpec((1, LANES), cmap)],
        out_specs=pl.BlockSpec((None, bq, LANES), qmap),
        out_shape=jax.ShapeDtypeStruct((b, tq, w), BF16),
        scratch_shapes=[pltpu.VMEM((heads, bq, LANES), BF16), pltpu.VMEM((heads, bq, LANES), F32),
                        pltpu.VMEM((heads, bq, LANES), F32),
                        pltpu.VMEM(past_buf, k_past.dtype), pltpu.VMEM(past_buf, v_past.dtype),
                        pltpu.SemaphoreType.DMA((2, 2)), pltpu.SMEM((heads * (bq // subn),), jnp.int32)],
        compiler_params=_cparams(("parallel", "parallel", "arbitrary")),
        name="stickbreak_attn",
    )(q, k_new, v_new, k_past, v_past, _tril_strict(subn), _tril_strict(subp), gain)


def _hgrn_halvings(c):
    return [c >> (i + 1) for i in range(int(np.log2(c)))]


def _hgrn_masks(c):
    t = np.arange(c)
    masks = [np.eye(c, dtype=bool)]
    for h in _hgrn_halvings(c):
        blk = t // (2 * h)
        second = (t // h) % 2 == 1
        masks.append((blk[:, None] == blk[None, :]) & second[:, None] & (~second)[None, :])
    return jnp.asarray(np.stack(masks).astype(np.float32))


def _boundary_rows(b, h, row):
    c, dk = b.shape
    if 2 * h >= 8:
        n = c // (2 * h)
        ref = b.reshape(n, 2 * h, dk)[:, h - 1:h, :]
        return jnp.broadcast_to(ref, (n, 2 * h, dk)).reshape(c, dk)
    down1 = pltpu.roll(b, 1, 0)
    if h == 1:
        return jnp.where((row & 1) == 1, down1, b)
    m = row & 3
    up1 = pltpu.roll(b, c - 1, 0)
    down2 = pltpu.roll(b, 2, 0)
    return jnp.where(m == 0, up1, jnp.where(m == 1, b, jnp.where(m == 2, down1, down2)))


def _hgrn_kernel(q_ref, k_ref, lf_ref, v_ref, g_ref, s0_ref, tri_ref, msk_ref, gain_ref,
                 y_ref, sout_ref, st_ref, *, c, n_chunks):
    t = pl.program_id(2)
    dk = q_ref.shape[-1]

    @pl.when(t == 0)
    def _():
        st_ref[...] = s0_ref[...].T

    row = lax.broadcasted_iota(jnp.int32, (c, dk), 0)
    tri = tri_ref[...]
    for ci in range(n_chunks):
        sl = slice(ci * c, (ci + 1) * c)
        q = q_ref[sl, :]
        k = k_ref[sl, :]
        lf = lf_ref[sl, :]
        v = v_ref[sl, :].astype(BF16)
        g = g_ref[sl, :]
        hi = lf.astype(BF16)
        r1 = lf - hi.astype(F32)
        mid = r1.astype(BF16)
        lo = (r1 - mid.astype(F32)).astype(BF16)
        b = _dot(tri, hi) + _dot(tri, mid) + _dot(tri, lo)
        st = st_ref[...]
        o = _dot_nt((q * jnp.exp(b)).astype(BF16), st.astype(BF16))
        a = msk_ref[0] * _dot_nt(q.astype(BF16), k.astype(BF16))
        for lv, h in enumerate(_hgrn_halvings(c)):
            gap = b - _boundary_rows(b, h, row)
            el = jnp.exp(jnp.where((row & h) != 0, gap, -gap))
            a += msk_ref[1 + lv] * _dot_nt((q * el).astype(BF16), (k * el).astype(BF16))
        o += _dot(a.astype(BF16), v)
        b_last = b[c - 1:c, :]
        k_end = (k * jnp.exp(b_last - b)).astype(BF16)
        st_ref[...] = st * jnp.exp(b_last) + _dot_tn(v, k_end)
        y = (_rms(o) * gain_ref[...]) * (g * _sigmoid(g))
        y_ref[sl, :] = y.astype(y_ref.dtype)

    @pl.when(t == pl.num_programs(2) - 1)
    def _():
        sout_ref[...] = st_ref[...].T


def _hgrn(q, k, lf, v, g, s0, gain, *, dk):
    b, t, hw = q.shape
    nh = hw // dk
    c = _tile(t, 256)
    tt = _tile(t, 512)
    msk = _hgrn_masks(c)
    tri = jnp.asarray(np.tril(np.ones((c, c), np.float32)), BF16)
    tok = lambda bb, h, ti: (bb, ti, h)
    smap = lambda bb, h, ti: (bb, h, 0, 0)
    return pl.pallas_call(
        functools.partial(_hgrn_kernel, c=c, n_chunks=tt // c),
        grid=(b, nh, t // tt),
        in_specs=[pl.BlockSpec((None, tt, dk), tok)] * 5
        + [pl.BlockSpec((None, None, dk, dk), smap),
           pl.BlockSpec((c, c), lambda bb, h, ti: (0, 0)),
           pl.BlockSpec(msk.shape, lambda bb, h, ti: (0, 0, 0)),
           pl.BlockSpec((1, dk), lambda bb, h, ti: (0, 0))],
        out_specs=[pl.BlockSpec((None, tt, dk), tok), pl.BlockSpec((None, None, dk, dk), smap)],
        out_shape=[jax.ShapeDtypeStruct((b, t, hw), BF16), jax.ShapeDtypeStruct((b, nh, dk, dk), F32)],
        scratch_shapes=[pltpu.VMEM((dk, dk), F32)],
        compiler_params=_cparams(("parallel", "parallel", "arbitrary")),
        name="hgrn2",
    )(q, k, lf, v, g, s0, tri, msk, gain)


def _outproj_kernel(x_ref, ya_ref, yb_ref, wa_ref, wb_ref, o_ref):
    o_ref[...] = x_ref[...] + _dot(ya_ref[...], wa_ref[...]) + _dot(yb_ref[...], wb_ref[...])


def _outproj(x, ya, yb, wa, wb):
    n, d = x.shape
    tm = _tile(n, 512)
    row = lambda i: (i, 0)
    const = lambda i: (0, 0)
    return pl.pallas_call(
        _outproj_kernel,
        grid=(n // tm,),
        in_specs=[pl.BlockSpec((tm, d), row), pl.BlockSpec((tm, ya.shape[1]), row),
                  pl.BlockSpec((tm, yb.shape[1]), row), pl.BlockSpec(wa.shape, const),
                  pl.BlockSpec(wb.shape, const)],
        out_specs=pl.BlockSpec((tm, d), row),
        out_shape=jax.ShapeDtypeStruct((n, d), F32),
        compiler_params=_cparams(("parallel",)),
        name="outproj",
    )(x, ya, yb, wa, wb)


FF_SUB = 256
FF_ROWS = 512


def _swiglu(x, g_ref, wg_ref, wu_ref, wd_ref, acc_ref):
    h = (_rms(x) * g_ref[...]).astype(BF16)
    ff = wg_ref.shape[-1]
    for c0 in range(0, ff, FF_SUB):
        c1 = min(c0 + FF_SUB, ff)
        gate = _dot(h, wg_ref[:, c0:c1])
        up = _dot(h, wu_ref[:, c0:c1])
        part = _dot(((gate * _sigmoid(gate)) * up).astype(BF16), wd_ref[c0:c1, :])
        if c0 == 0:
            acc_ref[...] = part
        else:
            acc_ref[...] += part


def _ffn_kernel(x_ref, g_ref, wg_ref, wu_ref, wd_ref, o_ref, acc_ref):
    x = x_ref[...]
    _swiglu(x, g_ref, wg_ref, wu_ref, wd_ref, acc_ref)
    o_ref[...] = x + acc_ref[...]


def _ffn(x, g, wg, wu, wd):
    n, d = x.shape
    tm = _tile(n, FF_ROWS)
    const = lambda i: (0, 0)
    return pl.pallas_call(
        _ffn_kernel,
        grid=(n // tm,),
        in_specs=[pl.BlockSpec((tm, d), lambda i: (i, 0)), pl.BlockSpec((1, d), const),
                  pl.BlockSpec(wg.shape, const), pl.BlockSpec(wu.shape, const), pl.BlockSpec(wd.shape, const)],
        out_specs=pl.BlockSpec((tm, d), lambda i: (i, 0)),
        out_shape=jax.ShapeDtypeStruct((n, d), F32),
        scratch_shapes=[pltpu.VMEM((tm, d), F32)],
        compiler_params=_cparams(("parallel",)),
        name="ffn_dense",
    )(x, g, wg, wu, wd)


ROUTE_E1, ROUTE_E2, ROUTE_R1, ROUTE_R2, ROUTE_G1, ROUTE_G2 = range(6)


def _router_kernel(x_ref, g_ref, rhi_ref, rlo_ref, tri_ref, route_ref, cnt_ref, base_ref, *, n_experts):
    i = pl.program_id(0)
    lane = lax.broadcasted_iota(jnp.int32, (1, LANES), 1)

    @pl.when(i == 0)
    def _():
        base_ref[...] = jnp.zeros_like(base_ref)

    hf = _rms(x_ref[...]) * g_ref[...]
    h_hi = hf.astype(BF16)
    h_lo = (hf - h_hi.astype(F32)).astype(BF16)
    logits = _dot(h_hi, rhi_ref[...]) + _dot(h_lo, rhi_ref[...]) + _dot(h_hi, rlo_ref[...])
    logits = jnp.where(lane < n_experts, logits, -jnp.inf)
    m1 = jnp.max(logits, axis=-1, keepdims=True)
    i1 = jnp.min(jnp.where(logits == m1, lane, LANES), axis=-1, keepdims=True)
    rest = jnp.where(lane == i1, -jnp.inf, logits)
    m2 = jnp.max(rest, axis=-1, keepdims=True)
    i2 = jnp.min(jnp.where(rest == m2, lane, LANES), axis=-1, keepdims=True)
    e2 = jnp.exp(m2 - m1)
    g1 = 1.0 / (1.0 + e2)
    g2 = e2 / (1.0 + e2)
    oh1 = (lane == i1).astype(F32)
    oh2 = (lane == i2).astype(F32)
    both = oh1 + oh2
    before = _dot(tri_ref[...], both.astype(BF16)) + base_ref[...]
    r1 = jnp.sum(oh1 * before, axis=-1, keepdims=True)
    r2 = jnp.sum(oh2 * before, axis=-1, keepdims=True)
    rec = jnp.zeros(route_ref.shape, F32)
    for slot, val in ((ROUTE_E1, i1.astype(F32)), (ROUTE_E2, i2.astype(F32)), (ROUTE_R1, r1),
                      (ROUTE_R2, r2), (ROUTE_G1, g1), (ROUTE_G2, g2)):
        rec = jnp.where(lane == slot, val, rec)
    route_ref[...] = rec
    base_ref[...] += jnp.sum(both, axis=0, keepdims=True)
    cnt_ref[...] = base_ref[...]


def _router(x, g, router):
    n, d = x.shape
    ne = router.shape[1]
    tm = _tile(n, 1024)
    rpad = jnp.zeros((d, LANES), F32).at[:, :ne].set(router)
    rhi = rpad.astype(BF16)
    rlo = (rpad - rhi.astype(F32)).astype(BF16)
    const = lambda i: (0, 0)
    return pl.pallas_call(
        functools.partial(_router_kernel, n_experts=ne),
        grid=(n // tm,),
        in_specs=[pl.BlockSpec((tm, d), lambda i: (i, 0)), pl.BlockSpec((1, d), const),
                  pl.BlockSpec((d, LANES), const), pl.BlockSpec((d, LANES), const),
                  pl.BlockSpec((tm, tm), const)],
        out_specs=[pl.BlockSpec((tm, LANES), lambda i: (i, 0)), pl.BlockSpec((1, LANES), const)],
        out_shape=[jax.ShapeDtypeStruct((n, LANES), F32), jax.ShapeDtypeStruct((1, LANES), F32)],
        scratch_shapes=[pltpu.VMEM((1, LANES), F32)],
        compiler_params=_cparams(("arbitrary",)),
        name="moe_router",
    )(x, g, rhi, rlo, _tril_strict(tm))


def _row_copy(src_ref, src_row, dst_ref, dst_row, sem):
    return pltpu.make_async_copy(src_ref.at[pl.ds(src_row, 1), :], dst_ref.at[pl.ds(dst_row, 1), :], sem)


def _dispatch_kernel(pad_lo_ref, pad_hi_ref, p1_ref, p2_ref, x_ref, xs_ref, sem, pad_sem, *, tb, n_pad_ranges):
    def issue(t, carry):
        _row_copy(x_ref, t, xs_ref, p1_ref[0, 0, t], sem).start(priority=0)
        _row_copy(x_ref, t, xs_ref, p2_ref[0, 0, t], sem).start(priority=1)
        return carry

    lax.fori_loop(0, tb, issue, 0, unroll=8)

    @pl.when(pl.program_id(0) == 0)
    def _():
        def fill(r, carry):
            _row_copy(x_ref, 0, xs_ref, r, pad_sem).start()
            return carry

        def drain(r, carry):
            _row_copy(x_ref, 0, xs_ref, r, pad_sem).wait()
            return carry

        for e in range(n_pad_ranges):
            lax.fori_loop(pad_lo_ref[e], pad_hi_ref[e], fill, 0)
        for e in range(n_pad_ranges):
            lax.fori_loop(pad_lo_ref[e], pad_hi_ref[e], drain, 0)

    for _ in range(TOP_K):
        pltpu.make_async_copy(x_ref, xs_ref.at[pl.ds(0, tb), :], sem).wait()


def _dispatch(x, pos1, pos2, n_rows, pad_lo, pad_hi):
    n, d = x.shape
    tb = pos1.shape[-1]
    smem = lambda: pl.BlockSpec((1, 1, tb), lambda i, lo, hi: (i, 0, 0), memory_space=pltpu.SMEM)
    grid_spec = pltpu.PrefetchScalarGridSpec(
        num_scalar_prefetch=2,
        grid=(n // tb,),
        in_specs=[smem(), smem(), pl.BlockSpec((tb, d), lambda i, lo, hi: (i, 0))],
        out_specs=pl.BlockSpec(memory_space=pl.ANY),
        scratch_shapes=[pltpu.SemaphoreType.DMA(()), pltpu.SemaphoreType.DMA(())],
    )
    return pl.pallas_call(
        functools.partial(_dispatch_kernel, tb=tb, n_pad_ranges=pad_lo.shape[0]),
        grid_spec=grid_spec,
        out_shape=jax.ShapeDtypeStruct((n_rows, d), F32),
        compiler_params=_cparams(("arbitrary",)),
        name="moe_dispatch",
    )(pad_lo, pad_hi, pos1, pos2, x)


def _experts_kernel(te_ref, nv_ref, x_ref, g_ref, wg_ref, wu_ref, wd_ref, o_ref, acc_ref):
    del te_ref
    valid = pl.program_id(0) < nv_ref[0]

    @pl.when(valid)
    def _():
        _swiglu(x_ref[...], g_ref, wg_ref, wu_ref, wd_ref, acc_ref)
        o_ref[...] = acc_ref[...]

    @pl.when(jnp.logical_not(valid))
    def _():
        o_ref[...] = jnp.zeros_like(o_ref)


def _experts(xs, g, wg, wu, wd, tile_expert, n_valid, tr):
    n_rows, d = xs.shape
    ff = wg.shape[2]
    rowmap = lambda i, te, nv: (jnp.minimum(i, nv[0] - 1), 0)
    wmap = lambda i, te, nv: (te[i], 0, 0)
    grid_spec = pltpu.PrefetchScalarGridSpec(
        num_scalar_prefetch=2,
        grid=(n_rows // tr,),
        in_specs=[pl.BlockSpec((tr, d), rowmap), pl.BlockSpec((1, d), lambda i, te, nv: (0, 0)),
                  pl.BlockSpec((None, d, ff), wmap), pl.BlockSpec((None, d, ff), wmap),
                  pl.BlockSpec((None, ff, d), wmap)],
        out_specs=pl.BlockSpec((tr, d), lambda i, te, nv: (i, 0)),
        scratch_shapes=[pltpu.VMEM((tr, d), F32)],
    )
    return pl.pallas_call(
        _experts_kernel,
        grid_spec=grid_spec,
        out_shape=jax.ShapeDtypeStruct((n_rows, d), F32),
        compiler_params=_cparams(("arbitrary",)),
        name="moe_experts",
    )(tile_expert, n_valid, xs, g, wg, wu, wd)


def _combine_kernel(p1_ref, p2_ref, x_ref, route_ref, ys_ref, o_ref, buf_ref, sem, *, tc):
    def issue(t, carry):
        _row_copy(ys_ref, p1_ref[0, 0, t], buf_ref.at[0], t, sem).start(priority=0)
        _row_copy(ys_ref, p2_ref[0, 0, t], buf_ref.at[1], t, sem).start(priority=1)
        return carry

    lax.fori_loop(0, tc, issue, 0, unroll=8)
    for slot in range(TOP_K):
        pltpu.make_async_copy(ys_ref.at[pl.ds(0, tc), :], buf_ref.at[slot], sem).wait()
    lane = lax.broadcasted_iota(jnp.int32, (1, LANES), 1)
    route = route_ref[...]
    g1 = jnp.sum(jnp.where(lane == ROUTE_G1, route, 0.0), axis=-1, keepdims=True)
    g2 = jnp.sum(jnp.where(lane == ROUTE_G2, route, 0.0), axis=-1, keepdims=True)
    o_ref[...] = x_ref[...] + (g1 * buf_ref[0] + g2 * buf_ref[1])


def _combine(x, route, ys, pos1, pos2):
    n, d = x.shape
    tc = pos1.shape[-1]
    smem = lambda: pl.BlockSpec((1, 1, tc), lambda i: (i, 0, 0), memory_space=pltpu.SMEM)
    return pl.pallas_call(
        functools.partial(_combine_kernel, tc=tc),
        grid=(n // tc,),
        in_specs=[smem(), smem(), pl.BlockSpec((tc, d), lambda i: (i, 0)),
                  pl.BlockSpec((tc, LANES), lambda i: (i, 0)), pl.BlockSpec(memory_space=pl.ANY)],
        out_specs=pl.BlockSpec((tc, d), lambda i: (i, 0)),
        out_shape=jax.ShapeDtypeStruct((n, d), F32),
        scratch_shapes=[pltpu.VMEM((TOP_K, tc, d), F32), pltpu.SemaphoreType.DMA(())],
        compiler_params=_cparams(("arbitrary",)),
        name="moe_combine",
    )(pos1, pos2, x, route, ys)


def _moe(x, g, router, wg, wu, wd):
    n, d = x.shape
    ne = wg.shape[0]
    tr = FF_ROWS if n >= 8192 else 256
    tb = _tile(n, 512)
    route, cnt = _router(x, g, router)
    counts = cnt[0, :ne].astype(jnp.int32)
    padded = ((counts + tr - 1) // tr) * tr
    ends = jnp.cumsum(padded)
    starts = ends - padded
    e1 = route[:, ROUTE_E1].astype(jnp.int32)
    e2 = route[:, ROUTE_E2].astype(jnp.int32)
    pos1 = (starts[e1] + route[:, ROUTE_R1].astype(jnp.int32)).reshape(n // tb, 1, tb)
    pos2 = (starts[e2] + route[:, ROUTE_R2].astype(jnp.int32)).reshape(n // tb, 1, tb)
    max_tiles = -(-(TOP_K * n + ne * (tr - 1)) // tr)
    n_valid = (ends[-1] // tr).reshape(1)
    tile_start = jnp.minimum(jnp.arange(max_tiles, dtype=jnp.int32), n_valid[0] - 1) * tr
    tile_expert = jnp.minimum(jnp.searchsorted(ends, tile_start, side="right"), ne - 1).astype(jnp.int32)
    n_rows = max_tiles * tr
    pad_lo = jnp.concatenate([starts + counts, ends[-1:]]).astype(jnp.int32)
    pad_hi = jnp.concatenate([ends, jnp.full((1,), n_rows, ends.dtype)]).astype(jnp.int32)
    xs = _dispatch(x, pos1, pos2, n_rows, pad_lo, pad_hi)
    ys = _experts(xs, g, wg, wu, wd, tile_expert, n_valid.astype(jnp.int32), tr)
    return _combine(x, route, ys, pos1, pos2)


def kernel(x_prompt, x_sample, cache_k, cache_v, state_hgrn, norm_mix, w_in, sb_q_gain, sb_k_gain,
           hg_lower_bounds, sb_out_gain, hg_out_gain, w_out, norm_ffn, ffn_w_gate, ffn_w_up, ffn_w_down,
           moe_router, moe_w_gate, moe_w_up, moe_w_down):
    depth = w_in.shape[0]
    bp, tp, d = x_prompt.shape
    bs, ts, _ = x_sample.shape
    past = cache_k.shape[2]
    sb_heads, hd = cache_k.shape[3], cache_k.shape[4]
    sw = sb_heads * hd
    dk = hg_out_gain.shape[1]
    hw = hg_lower_bounds.shape[1]
    hg_heads = hw // dk
    sb_scale = hd ** -0.5

    lbs = jnp.cumsum(jax.nn.softmax(hg_lower_bounds.astype(F32), axis=0), axis=0)
    lbs = lbs - lbs[0:1]

    xp = x_prompt.reshape(bp * tp, d)
    xs = x_sample.reshape(bs * ts, d)
    zeros_state = jnp.zeros((bp, hg_heads, dk, dk), F32)
    outs = {k: [] for k in ("sp", "ss")}
    cache_kt = jnp.transpose(cache_k, (0, 1, 3, 4, 2)).reshape(depth, bs, sw, past)
    cache_vt = jnp.transpose(cache_v, (0, 1, 3, 4, 2)).reshape(depth, bs, sw, past)

    def mixer(x, b, t, l, kv_prev, *, cached, past_len, s0, bq):
        qg = (jnp.tile(sb_q_gain[l], sb_heads) * sb_scale)[None, :]
        kg = jnp.tile(sb_k_gain[l], sb_heads)[None, :]
        q, kf, kb, vf, vb, qh, kh, lf, ih, gh = _inproj(
            x, norm_mix[l][None, :], w_in[l].astype(BF16), qg, kg,
            jnp.log(lbs[l])[None, :], jnp.log1p(-lbs[l])[None, :], (1.0 - lbs[l])[None, :], kv_prev,
            sw=sw, hw=hw, hd=hd, seq_len=t)
        r3 = lambda a: a.reshape(b, t, a.shape[-1])
        kb3, vb3 = r3(kb), r3(vb)
        ya = _attention(r3(q), kb3, vb3, cache_kt if cached else kb3, cache_vt if cached else vb3,
                        jnp.tile(sb_out_gain[l], LANES // hd)[None, :],
                        bq=bq, past_len=past_len, hd=hd, past_layer=l if cached else None)
        yb, s_fin = _hgrn(r3(qh), r3(kh), r3(lf), r3(ih), r3(gh), s0, hg_out_gain[l][None, :], dk=dk)
        wo = w_out[l].astype(BF16)
        x1 = _outproj(x, ya.reshape(b * t, sw), yb.reshape(b * t, hw), wo[:sw], wo[sw:])
        return x1, (kf, vf), s_fin

    def channel(x, l):
        j = l // 2
        if l % 2 == 0:
            return _ffn(x, norm_ffn[l][None, :], ffn_w_gate[j].astype(BF16), ffn_w_up[j].astype(BF16),
                        ffn_w_down[j].astype(BF16))
        return _moe(x, norm_ffn[l][None, :], moe_router[j], moe_w_gate[j].astype(BF16),
                    moe_w_up[j].astype(BF16), moe_w_down[j].astype(BF16))

    bq_p = _tile(tp, 512)
    kv_p = kv_s = None
    for l in range(depth):
        xp, kv_p, s_p = mixer(xp, bp, tp, l, kv_p, cached=False, past_len=None, s0=zeros_state, bq=bq_p)
        outs["sp"].append(s_p)
        xs, kv_s, s_s = mixer(xs, bs, ts, l, kv_s, cached=True, past_len=past,
                              s0=state_hgrn[l].astype(F32), bq=ts)
        outs["ss"].append(s_s)
        xp = channel(xp, l)
        xs = channel(xs, l)

    heads5 = lambda a, b, t: jnp.transpose(a.reshape(depth, b, sb_heads, hd, t), (0, 1, 4, 2, 3))
    return (xp.reshape(bp, tp, d), xs.reshape(bs, ts, d), heads5(kv_p[0], bp, tp), heads5(kv_p[1], bp, tp),
            jnp.stack(outs["sp"]), heads5(kv_s[0], bs, ts), heads5(kv_s[1], bs, ts), jnp.stack(outs["ss"]))
```

```python
import functools

import numpy as np
import jax
import jax.numpy as jnp
from jax import lax
from jax.experimental import pallas as pl
from jax.experimental.pallas import tpu as pltpu

EPS = 1e-6
TOP_K = 2
LANES = 128
F32 = jnp.float32
BF16 = jnp.bfloat16
VMEM_LIMIT_BYTES = 56 * 1024 * 1024


def _cparams(semantics):
    return pltpu.CompilerParams(dimension_semantics=semantics, vmem_limit_bytes=VMEM_LIMIT_BYTES)


def _dot(a, b):
    return jnp.dot(a, b, preferred_element_type=F32)


def _dot_nt(a, b):
    return lax.dot_general(a, b, (((1,), (1,)), ((), ())), preferred_element_type=F32)


def _dot_tn(a, b):
    return lax.dot_general(a, b, (((0,), (0,)), ((), ())), preferred_element_type=F32)


def _sigmoid(x):
    return 1.0 / (1.0 + jnp.exp(-x))


def _rms(x):
    return x * lax.rsqrt(jnp.mean(x * x, axis=-1, keepdims=True) + EPS)


def _tile(n, pref):
    t = min(n, pref)
    assert n % t == 0, (n, t)
    return t


def _inproj_kernel(*refs, sw, hw, hd, n_prev, seq_per_tile):
    x_ref, g_ref, w_ref, qg_ref, kg_ref, llb_ref, l1m_ref, oml_ref = refs[:8]
    prev = refs[8:8 + (2 if n_prev else 0)]
    q_ref, kf_ref, kb_ref, vf_ref, vb_ref, qh_ref, kh_ref, lf_ref, ih_ref, gh_ref = refs[8 + len(prev):]
    if n_prev:
        kf_ref[:n_prev] = prev[0][...]
        vf_ref[:n_prev] = prev[1][...]

    def store_time_minor(ref, y):
        t = y.shape[0] // seq_per_tile
        for sq in range(seq_per_tile):
            ref[n_prev, sq] = y[sq * t:(sq + 1) * t, :].T

    h = (_rms(x_ref[...]) * g_ref[...]).astype(BF16)

    def seg(lo, width):
        return _dot(h, w_ref[:, lo:lo + width])

    def headnorm(y, gain):
        lane = lax.broadcasted_iota(jnp.int32, (1, LANES), 1)
        tiles = []
        for c0 in range(0, y.shape[1], LANES):
            y2 = y[:, c0:c0 + LANES] * y[:, c0:c0 + LANES]
            m = jnp.zeros_like(y2)
            for h0 in range(0, LANES, hd):
                in_head = (lane >= h0) & (lane < h0 + hd)
                m = jnp.where(in_head, jnp.sum(jnp.where(in_head, y2, 0.0), axis=-1, keepdims=True), m)
            tiles.append(m * (1.0 / hd))
        m = jnp.concatenate(tiles, axis=1)
        return (y * lax.rsqrt(m + EPS)) * gain

    q_ref[...] = headnorm(seg(0, sw), qg_ref[...]).astype(BF16)
    ka = headnorm(seg(sw, sw), kg_ref[...])
    store_time_minor(kf_ref, ka)
    kb_ref[...] = ka.astype(BF16)
    va = seg(2 * sw, sw)
    store_time_minor(vf_ref, va)
    vb_ref[...] = va.astype(BF16)

    o = 3 * sw
    qb = seg(o, hw)
    qh_ref[...] = qb * _sigmoid(qb)

    fb = seg(o + hw, hw)
    e = jnp.exp(-jnp.abs(fb))
    log_sig = jnp.minimum(fb, 0.0) - jnp.log(1.0 + e)
    c = l1m_ref[...] + log_sig
    a = llb_ref[...]
    lf_ref[...] = jnp.maximum(a, c) + jnp.log(1.0 + jnp.exp(-jnp.abs(a - c)))
    kh_ref[...] = oml_ref[...] * (jnp.where(fb >= 0.0, e, 1.0) / (1.0 + e))

    ih_ref[...] = seg(o + 2 * hw, hw)
    gh_ref[...] = seg(o + 3 * hw, hw)


def _inproj(x, g, w, qg, kg, llb, l1m, oml, kv_prev, *, sw, hw, hd, seq_len):
    n, d = x.shape
    tm = _tile(n, 512)
    n_prev = 0 if kv_prev is None else kv_prev[0].shape[0]
    spt = max(tm // seq_len, 1)
    tt = tm // spt
    nt = seq_len // tt
    assert spt * tt == tm and nt * tt == seq_len
    row = lambda i: (i, 0)
    const = lambda i: (0, 0)
    stacked = lambda i: (0, i // nt, 0, i % nt)
    kv_shape = (n_prev + 1, n // seq_len, sw, seq_len)
    specs = {"row_sw": pl.BlockSpec((tm, sw), row), "row_hw": pl.BlockSpec((tm, hw), row),
             "stack": pl.BlockSpec((n_prev + 1, spt, sw, tt), stacked)}
    outs = [("row_sw", (n, sw), BF16), ("stack", kv_shape, F32), ("row_sw", (n, sw), BF16),
            ("stack", kv_shape, F32), ("row_sw", (n, sw), BF16)] + [("row_hw", (n, hw), F32)] * 5
    prev_specs = [pl.BlockSpec((n_prev, spt, sw, tt), stacked)] * 2 if n_prev else []
    return pl.pallas_call(
        functools.partial(_inproj_kernel, sw=sw, hw=hw, hd=hd, n_prev=n_prev, seq_per_tile=spt),
        grid=(n // tm,),
        in_specs=[pl.BlockSpec((tm, d), row), pl.BlockSpec((1, d), const),
                  pl.BlockSpec(w.shape, const),
                  pl.BlockSpec((1, sw), const), pl.BlockSpec((1, sw), const),
                  pl.BlockSpec((1, hw), const), pl.BlockSpec((1, hw), const), pl.BlockSpec((1, hw), const)]
        + prev_specs,
        out_specs=[specs[kind] for kind, _, _ in outs],
        out_shape=[jax.ShapeDtypeStruct(shape, dt) for _, shape, dt in outs],
        compiler_params=_cparams(("parallel",)),
        name="inproj",
    )(x, g, w, qg, kg, llb, l1m, oml, *(kv_prev or ()))


KEY_SUB = 256
SOFTPLUS_CLAMP = 80.0
CARRY_DEAD = 104.0


def _attn_kernel(q_ref, kn_ref, vn_ref, kp_hbm, vp_hbm, un_ref, up_ref, gain_ref,
                 y_ref, qm_ref, acc_ref, carry_ref, kbuf, vbuf, sem, alive_ref,
                 *, bq, subn, subp, hd, past_len, past_layer, past_transposed):
    bb = pl.program_id(0)
    p = pl.program_id(1)
    qi = pl.program_id(2)
    lane = lax.broadcasted_iota(jnp.int32, (1, LANES), 1)
    heads = LANES // hd
    n_past = (qi * bq if past_len is None else past_len) // subp

    def fetch(j, slot):
        if past_transposed:
            window = (past_layer, bb, pl.ds(p * LANES, LANES), pl.ds(j * subp, subp))
        else:
            window = (bb, pl.ds(j * subp, subp), pl.ds(p * LANES, LANES))
        return (pltpu.make_async_copy(kp_hbm.at[window], kbuf.at[slot], sem.at[0, slot]),
                pltpu.make_async_copy(vp_hbm.at[window], vbuf.at[slot], sem.at[1, slot]))

    def slot_of(j):
        return (n_past - 1 - j) & 1

    @pl.when(n_past > 0)
    def _():
        for cp in fetch(n_past - 1, 0):
            cp.start()

    q = q_ref[...]
    for hh in range(heads):
        in_head = (lane >= hh * hd) & (lane < (hh + 1) * hd)
        qm_ref[hh] = jnp.where(in_head, q, jnp.zeros_like(q))
    acc_ref[...] = jnp.zeros_like(acc_ref)
    carry_ref[...] = jnp.zeros_like(carry_ref)

    groups = bq // subn

    def one_head(hh, k, v, u, r0, r1, mask, transposed):
        if transposed:
            z = _dot(qm_ref[hh, r0:r1, :], k)
        else:
            z = _dot_nt(qm_ref[hh, r0:r1, :], k)
        sp = jnp.maximum(jnp.log(1.0 + jnp.exp(jnp.minimum(z, SOFTPLUS_CLAMP))), z)
        if mask is not None:
            sp = jnp.where(mask, sp, 0.0)
        after = _dot(sp.astype(BF16), u)
        w = jnp.exp((z - sp) - after)
        if mask is not None:
            w = jnp.where(mask, w, 0.0)
        pv = _dot_nt(w.astype(BF16), v) if transposed else _dot(w.astype(BF16), v)
        c = carry_ref[hh, r0:r1, :]
        acc_ref[hh, r0:r1, :] += jnp.exp(-c) * pv
        c_new = c + jnp.sum(sp, axis=-1, keepdims=True)
        carry_ref[hh, r0:r1, :] = c_new
        return c_new

    def set_alive(hh, g, c_group):
        alive_ref[hh * groups + g] = (jnp.min(c_group) < CARRY_DEAD).astype(jnp.int32)

    u_new = un_ref[...]
    for j in range(groups - 1, -1, -1):
        r0 = j * subn
        rows = r0 + lax.broadcasted_iota(jnp.int32, (bq - r0, subn), 0)
        cols = r0 + lax.broadcasted_iota(jnp.int32, (bq - r0, subn), 1)
        k = kn_ref[r0:r0 + subn, :].astype(BF16)
        v = vn_ref[r0:r0 + subn, :].astype(BF16)
        for hh in range(heads):
            c_new = one_head(hh, k, v, u_new, r0, bq, cols < rows, False)
            if j == 0:
                for g in range(groups):
                    set_alive(hh, g, c_new[g * subn:(g + 1) * subn])

    def any_alive():
        total = alive_ref[0]
        for i in range(1, heads * groups):
            total += alive_ref[i]
        return total > 0

    def past_block(j):
        slot = slot_of(j)
        for cp in fetch(j, slot):
            cp.wait()

        @pl.when(j > 0)
        def _():
            for cp in fetch(j - 1, 1 - slot):
                cp.start()

        u_past = up_ref[...]

        def group_heads(g, head_list):
            k = kbuf[slot].astype(BF16)
            v = vbuf[slot].astype(BF16)
            for hh in head_list:
                c_new = one_head(hh, k, v, u_past, g * subn, (g + 1) * subn, None, past_transposed)
                set_alive(hh, g, c_new)

        for g in range(groups):
            n_alive = alive_ref[g]
            for hh in range(1, heads):
                n_alive += alive_ref[hh * groups + g]
            pl.when(n_alive == heads)(functools.partial(group_heads, g, range(heads)))
            for hh in range(heads):
                pl.when((n_alive < heads) & (alive_ref[hh * groups + g] == 1))(
                    functools.partial(group_heads, g, (hh,)))
        return j - 1

    j_end = lax.while_loop(lambda j: (j >= 0) & any_alive(), past_block, n_past - 1)

    @pl.when(j_end >= 0)
    def _():
        for cp in fetch(j_end, slot_of(j_end)):
            cp.wait()

    o = acc_ref[0]
    for hh in range(1, heads):
        in_head = (lane >= hh * hd) & (lane < (hh + 1) * hd)
        o = jnp.where(in_head, acc_ref[hh], o)
    o2 = o * o
    ms = jnp.zeros_like(o)
    for hh in range(heads):
        in_head = (lane >= hh * hd) & (lane < (hh + 1) * hd)
        ssum = jnp.sum(jnp.where(in_head, o2, 0.0), axis=-1, keepdims=True)
        ms = jnp.where(in_head, ssum * (1.0 / hd), ms)
    y_ref[...] = ((o * lax.rsqrt(ms + EPS)) * gain_ref[...]).astype(y_ref.dtype)


def _tril_strict(n):
    return jnp.asarray(np.tril(np.ones((n, n), np.float32), -1), BF16)


def _attention(q, k_new, v_new, k_past, v_past, gain, *, bq, past_len, hd, past_layer=None):
    b, tq, w = q.shape
    past_transposed = past_layer is not None
    subn = min(bq, KEY_SUB)
    subp = KEY_SUB
    assert bq % subn == 0 and (bq if past_len is None else past_len) % subp == 0
    heads = LANES // hd
    qmap = lambda bb, p, qi: (bb, qi, p)
    cmap = lambda bb, p, qi: (0, 0)
    past_buf = (2, LANES, subp) if past_transposed else (2, subp, LANES)
    return pl.pallas_call(
        functools.partial(_attn_kernel, bq=bq, subn=subn, subp=subp, hd=hd, past_len=past_len,
                          past_layer=past_layer, past_transposed=past_transposed),
        grid=(b, w // LANES, tq // bq),
        in_specs=[pl.BlockSpec((None, bq, LANES), qmap),
                  pl.BlockSpec((None, bq, LANES), qmap), pl.BlockSpec((None, bq, LANES), qmap),
                  pl.BlockSpec(memory_space=pl.ANY), pl.BlockSpec(memory_space=pl.ANY),
                  pl.BlockSpec((subn, subn), cmap), pl.BlockSpec((subp, subp), cmap),
                  pl.BlockSpec((1, LANES), cmap)],
        out_specs=pl.BlockSpec((None, bq, LANES), qmap),
        out_shape=jax.ShapeDtypeStruct((b, tq, w), BF16),
        scratch_shapes=[pltpu.VMEM((heads, bq, LANES), BF16), pltpu.VMEM((heads, bq, LANES), F32),
                        pltpu.VMEM((heads, bq, LANES), F32),
                        pltpu.VMEM(past_buf, k_past.dtype), pltpu.VMEM(past_buf, v_past.dtype),
                        pltpu.SemaphoreType.DMA((2, 2)), pltpu.SMEM((heads * (bq // subn),), jnp.int32)],
        compiler_params=_cparams(("parallel", "parallel", "arbitrary")),
        name="stickbreak_attn",
    )(q, k_new, v_new, k_past, v_past, _tril_strict(subn), _tril_strict(subp), gain)


def _hgrn_halvings(c):
    return [c >> (i + 1) for i in range(int(np.log2(c)))]


def _hgrn_masks(c):
    t = np.arange(c)
    masks = [np.eye(c, dtype=bool)]
    for h in _hgrn_halvings(c):
        blk = t // (2 * h)
        second = (t // h) % 2 == 1
        masks.append((blk[:, None] == blk[None, :]) & second[:, None] & (~second)[None, :])
    return jnp.asarray(np.stack(masks).astype(np.float32))


def _boundary_rows(b, h, row):
    c, dk = b.shape
    if 2 * h >= 8:
        n = c // (2 * h)
        ref = b.reshape(n, 2 * h, dk)[:, h - 1:h, :]
        return jnp.broadcast_to(ref, (n, 2 * h, dk)).reshape(c, dk)
    down1 = pltpu.roll(b, 1, 0)
    if h == 1:
        return jnp.where((row & 1) == 1, down1, b)
    m = row & 3
    up1 = pltpu.roll(b, c - 1, 0)
    down2 = pltpu.roll(b, 2, 0)
    return jnp.where(m == 0, up1, jnp.where(m == 1, b, jnp.where(m == 2, down1, down2)))


def _hgrn_kernel(q_ref, k_ref, lf_ref, v_ref, g_ref, s0_ref, tri_ref, msk_ref, gain_ref,
                 y_ref, sout_ref, st_ref, *, c, n_chunks):
    t = pl.program_id(2)
    dk = q_ref.shape[-1]

    @pl.when(t == 0)
    def _():
        st_ref[...] = s0_ref[...].T

    row = lax.broadcasted_iota(jnp.int32, (c, dk), 0)
    tri = tri_ref[...]
    for ci in range(n_chunks):
        sl = slice(ci * c, (ci + 1) * c)
        q = q_ref[sl, :]
        k = k_ref[sl, :]
        lf = lf_ref[sl, :]
        v = v_ref[sl, :].astype(BF16)
        g = g_ref[sl, :]
        hi = lf.astype(BF16)
        r1 = lf - hi.astype(F32)
        mid = r1.astype(BF16)
        lo = (r1 - mid.astype(F32)).astype(BF16)
        b = _dot(tri, hi) + _dot(tri, mid) + _dot(tri, lo)
        st = st_ref[...]
        o = _dot_nt((q * jnp.exp(b)).astype(BF16), st.astype(BF16))
        a = msk_ref[0] * _dot_nt(q.astype(BF16), k.astype(BF16))
        for lv, h in enumerate(_hgrn_halvings(c)):
            gap = b - _boundary_rows(b, h, row)
            el = jnp.exp(jnp.where((row & h) != 0, gap, -gap))
            a += msk_ref[1 + lv] * _dot_nt((q * el).astype(BF16), (k * el).astype(BF16))
        o += _dot(a.astype(BF16), v)
        b_last = b[c - 1:c, :]
        k_end = (k * jnp.exp(b_last - b)).astype(BF16)
        st_ref[...] = st * jnp.exp(b_last) + _dot_tn(v, k_end)
        y = (_rms(o) * gain_ref[...]) * (g * _sigmoid(g))
        y_ref[sl, :] = y.astype(y_ref.dtype)

    @pl.when(t == pl.num_programs(2) - 1)
    def _():
        sout_ref[...] = st_ref[...].T


def _hgrn(q, k, lf, v, g, s0, gain, *, dk):
    b, t, hw = q.shape
    nh = hw // dk
    c = _tile(t, 256)
    tt = _tile(t, 512)
    msk = _hgrn_masks(c)
    tri = jnp.asarray(np.tril(np.ones((c, c), np.float32)), BF16)
    tok = lambda bb, h, ti: (bb, ti, h)
    smap = lambda bb, h, ti: (bb, h, 0, 0)
    return pl.pallas_call(
        functools.partial(_hgrn_kernel, c=c, n_chunks=tt // c),
        grid=(b, nh, t // tt),
        in_specs=[pl.BlockSpec((None, tt, dk), tok)] * 5
        + [pl.BlockSpec((None, None, dk, dk), smap),
           pl.BlockSpec((c, c), lambda bb, h, ti: (0, 0)),
           pl.BlockSpec(msk.shape, lambda bb, h, ti: (0, 0, 0)),
           pl.BlockSpec((1, dk), lambda bb, h, ti: (0, 0))],
        out_specs=[pl.BlockSpec((None, tt, dk), tok), pl.BlockSpec((None, None, dk, dk), smap)],
        out_shape=[jax.ShapeDtypeStruct((b, t, hw), BF16), jax.ShapeDtypeStruct((b, nh, dk, dk), F32)],
        scratch_shapes=[pltpu.VMEM((dk, dk), F32)],
        compiler_params=_cparams(("parallel", "parallel", "arbitrary")),
        name="hgrn2",
    )(q, k, lf, v, g, s0, tri, msk, gain)


def _mixed_residual(x_ref, ya_ref, yb_ref, wa_ref, wb_ref):
    return x_ref[...] + _dot(ya_ref[...], wa_ref[...]) + _dot(yb_ref[...], wb_ref[...])


def _mix_specs(mix, tm, index_row, index_const):
    ya, yb, wa, wb = mix
    return [pl.BlockSpec((tm, ya.shape[1]), index_row), pl.BlockSpec((tm, yb.shape[1]), index_row),
            pl.BlockSpec(wa.shape, index_const), pl.BlockSpec(wb.shape, index_const)]


FF_SUB = 256
FF_ROWS = 512


def _swiglu(x, g_ref, wg_ref, wu_ref, wd_ref, acc_ref):
    h = (_rms(x) * g_ref[...]).astype(BF16)
    ff = wg_ref.shape[-1]
    for c0 in range(0, ff, FF_SUB):
        c1 = min(c0 + FF_SUB, ff)
        gate = _dot(h, wg_ref[:, c0:c1])
        up = _dot(h, wu_ref[:, c0:c1])
        part = _dot(((gate * _sigmoid(gate)) * up).astype(BF16), wd_ref[c0:c1, :])
        if c0 == 0:
            acc_ref[...] = part
        else:
            acc_ref[...] += part


def _ffn_kernel(x_ref, ya_ref, yb_ref, wa_ref, wb_ref, g_ref, wg_ref, wu_ref, wd_ref, o_ref, acc_ref):
    x = _mixed_residual(x_ref, ya_ref, yb_ref, wa_ref, wb_ref)
    _swiglu(x, g_ref, wg_ref, wu_ref, wd_ref, acc_ref)
    o_ref[...] = x + acc_ref[...]


def _ffn(x, mix, g, wg, wu, wd):
    n, d = x.shape
    tm = _tile(n, FF_ROWS)
    row = lambda i: (i, 0)
    const = lambda i: (0, 0)
    return pl.pallas_call(
        _ffn_kernel,
        grid=(n // tm,),
        in_specs=[pl.BlockSpec((tm, d), row)] + _mix_specs(mix, tm, row, const)
        + [pl.BlockSpec((1, d), const),
           pl.BlockSpec(wg.shape, const), pl.BlockSpec(wu.shape, const), pl.BlockSpec(wd.shape, const)],
        out_specs=pl.BlockSpec((tm, d), row),
        out_shape=jax.ShapeDtypeStruct((n, d), F32),
        scratch_shapes=[pltpu.VMEM((tm, d), F32)],
        compiler_params=_cparams(("parallel",)),
        name="ffn_dense",
    )(x, *mix, g, wg, wu, wd)


ROUTE_E1, ROUTE_E2, ROUTE_R1, ROUTE_R2, ROUTE_G1, ROUTE_G2 = range(6)


def _router_kernel(x_ref, ya_ref, yb_ref, wa_ref, wb_ref, g_ref, rhi_ref, rlo_ref, tri_ref,
                   x1_ref, route_ref, cnt_ref, base_ref, *, n_experts):
    i = pl.program_id(0)
    lane = lax.broadcasted_iota(jnp.int32, (1, LANES), 1)

    @pl.when(i == 0)
    def _():
        base_ref[...] = jnp.zeros_like(base_ref)

    x1 = _mixed_residual(x_ref, ya_ref, yb_ref, wa_ref, wb_ref)
    x1_ref[...] = x1
    hf = _rms(x1) * g_ref[...]
    h_hi = hf.astype(BF16)
    h_lo = (hf - h_hi.astype(F32)).astype(BF16)
    logits = _dot(h_hi, rhi_ref[...]) + _dot(h_lo, rhi_ref[...]) + _dot(h_hi, rlo_ref[...])
    logits = jnp.where(lane < n_experts, logits, -jnp.inf)
    m1 = jnp.max(logits, axis=-1, keepdims=True)
    i1 = jnp.min(jnp.where(logits == m1, lane, LANES), axis=-1, keepdims=True)
    rest = jnp.where(lane == i1, -jnp.inf, logits)
    m2 = jnp.max(rest, axis=-1, keepdims=True)
    i2 = jnp.min(jnp.where(rest == m2, lane, LANES), axis=-1, keepdims=True)
    e2 = jnp.exp(m2 - m1)
    g1 = 1.0 / (1.0 + e2)
    g2 = e2 / (1.0 + e2)
    oh1 = (lane == i1).astype(F32)
    oh2 = (lane == i2).astype(F32)
    both = oh1 + oh2
    before = _dot(tri_ref[...], both.astype(BF16)) + base_ref[...]
    r1 = jnp.sum(oh1 * before, axis=-1, keepdims=True)
    r2 = jnp.sum(oh2 * before, axis=-1, keepdims=True)
    rec = jnp.zeros(route_ref.shape, F32)
    for slot, val in ((ROUTE_E1, i1.astype(F32)), (ROUTE_E2, i2.astype(F32)), (ROUTE_R1, r1),
                      (ROUTE_R2, r2), (ROUTE_G1, g1), (ROUTE_G2, g2)):
        rec = jnp.where(lane == slot, val, rec)
    route_ref[...] = rec
    base_ref[...] += jnp.sum(both, axis=0, keepdims=True)
    cnt_ref[...] = base_ref[...]


def _router(x, mix, g, router):
    n, d = x.shape
    ne = router.shape[1]
    tm = _tile(n, 1024)
    rpad = jnp.zeros((d, LANES), F32).at[:, :ne].set(router)
    rhi = rpad.astype(BF16)
    rlo = (rpad - rhi.astype(F32)).astype(BF16)
    row = lambda i: (i, 0)
    const = lambda i: (0, 0)
    return pl.pallas_call(
        functools.partial(_router_kernel, n_experts=ne),
        grid=(n // tm,),
        in_specs=[pl.BlockSpec((tm, d), row)] + _mix_specs(mix, tm, row, const)
        + [pl.BlockSpec((1, d), const),
           pl.BlockSpec((d, LANES), const), pl.BlockSpec((d, LANES), const),
           pl.BlockSpec((tm, tm), const)],
        out_specs=[pl.BlockSpec((tm, d), row), pl.BlockSpec((tm, LANES), row), pl.BlockSpec((1, LANES), const)],
        out_shape=[jax.ShapeDtypeStruct((n, d), F32), jax.ShapeDtypeStruct((n, LANES), F32),
                   jax.ShapeDtypeStruct((1, LANES), F32)],
        scratch_shapes=[pltpu.VMEM((1, LANES), F32)],
        compiler_params=_cparams(("arbitrary",)),
        name="moe_router",
    )(x, *mix, g, rhi, rlo, _tril_strict(tm))


def _row_copy(src_ref, src_row, dst_ref, dst_row, sem):
    return pltpu.make_async_copy(src_ref.at[pl.ds(src_row, 1), :], dst_ref.at[pl.ds(dst_row, 1), :], sem)


def _dispatch_kernel(pad_lo_ref, pad_hi_ref, p1_ref, p2_ref, x_ref, xs_ref, sem, pad_sem, *, tb, n_pad_ranges):
    def issue(t, carry):
        _row_copy(x_ref, t, xs_ref, p1_ref[0, 0, t], sem).start(priority=0)
        _row_copy(x_ref, t, xs_ref, p2_ref[0, 0, t], sem).start(priority=1)
        return carry

    lax.fori_loop(0, tb, issue, 0, unroll=8)

    @pl.when(pl.program_id(0) == 0)
    def _():
        def fill(r, carry):
            _row_copy(x_ref, 0, xs_ref, r, pad_sem).start()
            return carry

        def drain(r, carry):
            _row_copy(x_ref, 0, xs_ref, r, pad_sem).wait()
            return carry

        for e in range(n_pad_ranges):
            lax.fori_loop(pad_lo_ref[e], pad_hi_ref[e], fill, 0)
        for e in range(n_pad_ranges):
            lax.fori_loop(pad_lo_ref[e], pad_hi_ref[e], drain, 0)

    for _ in range(TOP_K):
        pltpu.make_async_copy(x_ref, xs_ref.at[pl.ds(0, tb), :], sem).wait()


def _dispatch(x, pos1, pos2, n_rows, pad_lo, pad_hi):
    n, d = x.shape
    tb = pos1.shape[-1]
    smem = lambda: pl.BlockSpec((1, 1, tb), lambda i, lo, hi: (i, 0, 0), memory_space=pltpu.SMEM)
    grid_spec = pltpu.PrefetchScalarGridSpec(
        num_scalar_prefetch=2,
        grid=(n // tb,),
        in_specs=[smem(), smem(), pl.BlockSpec((tb, d), lambda i, lo, hi: (i, 0))],
        out_specs=pl.BlockSpec(memory_space=pl.ANY),
        scratch_shapes=[pltpu.SemaphoreType.DMA(()), pltpu.SemaphoreType.DMA(())],
    )
    return pl.pallas_call(
        functools.partial(_dispatch_kernel, tb=tb, n_pad_ranges=pad_lo.shape[0]),
        grid_spec=grid_spec,
        out_shape=jax.ShapeDtypeStruct((n_rows, d), F32),
        compiler_params=_cparams(("arbitrary",)),
        name="moe_dispatch",
    )(pad_lo, pad_hi, pos1, pos2, x)


def _experts_kernel(te_ref, nv_ref, x_ref, g_ref, wg_ref, wu_ref, wd_ref, o_ref, acc_ref):
    del te_ref
    valid = pl.program_id(0) < nv_ref[0]

    @pl.when(valid)
    def _():
        _swiglu(x_ref[...], g_ref, wg_ref, wu_ref, wd_ref, acc_ref)
        o_ref[...] = acc_ref[...]

    @pl.when(jnp.logical_not(valid))
    def _():
        o_ref[...] = jnp.zeros_like(o_ref)


def _experts(xs, g, wg, wu, wd, tile_expert, n_valid, tr):
    n_rows, d = xs.shape
    ff = wg.shape[2]
    rowmap = lambda i, te, nv: (jnp.minimum(i, nv[0] - 1), 0)
    wmap = lambda i, te, nv: (te[i], 0, 0)
    grid_spec = pltpu.PrefetchScalarGridSpec(
        num_scalar_prefetch=2,
        grid=(n_rows // tr,),
        in_specs=[pl.BlockSpec((tr, d), rowmap), pl.BlockSpec((1, d), lambda i, te, nv: (0, 0)),
                  pl.BlockSpec((None, d, ff), wmap), pl.BlockSpec((None, d, ff), wmap),
                  pl.BlockSpec((None, ff, d), wmap)],
        out_specs=pl.BlockSpec((tr, d), lambda i, te, nv: (i, 0)),
        scratch_shapes=[pltpu.VMEM((tr, d), F32)],
    )
    return pl.pallas_call(
        _experts_kernel,
        grid_spec=grid_spec,
        out_shape=jax.ShapeDtypeStruct((n_rows, d), F32),
        compiler_params=_cparams(("arbitrary",)),
        name="moe_experts",
    )(tile_expert, n_valid, xs, g, wg, wu, wd)


def _combine_kernel(p1_ref, p2_ref, x_ref, route_ref, ys_ref, o_ref, buf_ref, sem, *, tc):
    def issue(t, carry):
        _row_copy(ys_ref, p1_ref[0, 0, t], buf_ref.at[0], t, sem).start(priority=0)
        _row_copy(ys_ref, p2_ref[0, 0, t], buf_ref.at[1], t, sem).start(priority=1)
        return carry

    lax.fori_loop(0, tc, issue, 0, unroll=8)
    for slot in range(TOP_K):
        pltpu.make_async_copy(ys_ref.at[pl.ds(0, tc), :], buf_ref.at[slot], sem).wait()
    lane = lax.broadcasted_iota(jnp.int32, (1, LANES), 1)
    route = route_ref[...]
    g1 = jnp.sum(jnp.where(lane == ROUTE_G1, route, 0.0), axis=-1, keepdims=True)
    g2 = jnp.sum(jnp.where(lane == ROUTE_G2, route, 0.0), axis=-1, keepdims=True)
    o_ref[...] = x_ref[...] + (g1 * buf_ref[0] + g2 * buf_ref[1])


def _combine(x, route, ys, pos1, pos2):
    n, d = x.shape
    tc = pos1.shape[-1]
    smem = lambda: pl.BlockSpec((1, 1, tc), lambda i: (i, 0, 0), memory_space=pltpu.SMEM)
    return pl.pallas_call(
        functools.partial(_combine_kernel, tc=tc),
        grid=(n // tc,),
        in_specs=[smem(), smem(), pl.BlockSpec((tc, d), lambda i: (i, 0)),
                  pl.BlockSpec((tc, LANES), lambda i: (i, 0)), pl.BlockSpec(memory_space=pl.ANY)],
        out_specs=pl.BlockSpec((tc, d), lambda i: (i, 0)),
        out_shape=jax.ShapeDtypeStruct((n, d), F32),
        scratch_shapes=[pltpu.VMEM((TOP_K, tc, d), F32), pltpu.SemaphoreType.DMA(())],
        compiler_params=_cparams(("arbitrary",)),
        name="moe_combine",
    )(pos1, pos2, x, route, ys)


def _moe(x, mix, g, router, wg, wu, wd):
    n, d = x.shape
    ne = wg.shape[0]
    tr = FF_ROWS if n >= 8192 else 256
    tb = _tile(n, 512)
    x, route, cnt = _router(x, mix, g, router)
    counts = cnt[0, :ne].astype(jnp.int32)
    padded = ((counts + tr - 1) // tr) * tr
    ends = jnp.cumsum(padded)
    starts = ends - padded
    e1 = route[:, ROUTE_E1].astype(jnp.int32)
    e2 = route[:, ROUTE_E2].astype(jnp.int32)
    pos1 = (starts[e1] + route[:, ROUTE_R1].astype(jnp.int32)).reshape(n // tb, 1, tb)
    pos2 = (starts[e2] + route[:, ROUTE_R2].astype(jnp.int32)).reshape(n // tb, 1, tb)
    max_tiles = -(-(TOP_K * n + ne * (tr - 1)) // tr)
    n_valid = (ends[-1] // tr).reshape(1)
    tile_start = jnp.minimum(jnp.arange(max_tiles, dtype=jnp.int32), n_valid[0] - 1) * tr
    tile_expert = jnp.minimum(jnp.searchsorted(ends, tile_start, side="right"), ne - 1).astype(jnp.int32)
    n_rows = max_tiles * tr
    pad_lo = jnp.concatenate([starts + counts, ends[-1:]]).astype(jnp.int32)
    pad_hi = jnp.concatenate([ends, jnp.full((1,), n_rows, ends.dtype)]).astype(jnp.int32)
    xs = _dispatch(x, pos1, pos2, n_rows, pad_lo, pad_hi)
    ys = _experts(xs, g, wg, wu, wd, tile_expert, n_valid.astype(jnp.int32), tr)
    return _combine(x, route, ys, pos1, pos2)


def kernel(x_prompt, x_sample, cache_k, cache_v, state_hgrn, norm_mix, w_in, sb_q_gain, sb_k_gain,
           hg_lower_bounds, sb_out_gain, hg_out_gain, w_out, norm_ffn, ffn_w_gate, ffn_w_up, ffn_w_down,
           moe_router, moe_w_gate, moe_w_up, moe_w_down):
    depth = w_in.shape[0]
    bp, tp, d = x_prompt.shape
    bs, ts, _ = x_sample.shape
    past = cache_k.shape[2]
    sb_heads, hd = cache_k.shape[3], cache_k.shape[4]
    sw = sb_heads * hd
    dk = hg_out_gain.shape[1]
    hw = hg_lower_bounds.shape[1]
    hg_heads = hw // dk
    sb_scale = hd ** -0.5

    lbs = jnp.cumsum(jax.nn.softmax(hg_lower_bounds.astype(F32), axis=0), axis=0)
    lbs = lbs - lbs[0:1]

    xp = x_prompt.reshape(bp * tp, d)
    xs = x_sample.reshape(bs * ts, d)
    zeros_state = jnp.zeros((bp, hg_heads, dk, dk), F32)
    outs = {k: [] for k in ("sp", "ss")}
    cache_kt = jnp.transpose(cache_k, (0, 1, 3, 4, 2)).reshape(depth, bs, sw, past)
    cache_vt = jnp.transpose(cache_v, (0, 1, 3, 4, 2)).reshape(depth, bs, sw, past)

    def mixer(x, b, t, l, kv_prev, *, cached, past_len, s0, bq):
        qg = (jnp.tile(sb_q_gain[l], sb_heads) * sb_scale)[None, :]
        kg = jnp.tile(sb_k_gain[l], sb_heads)[None, :]
        q, kf, kb, vf, vb, qh, kh, lf, ih, gh = _inproj(
            x, norm_mix[l][None, :], w_in[l].astype(BF16), qg, kg,
            jnp.log(lbs[l])[None, :], jnp.log1p(-lbs[l])[None, :], (1.0 - lbs[l])[None, :], kv_prev,
            sw=sw, hw=hw, hd=hd, seq_len=t)
        r3 = lambda a: a.reshape(b, t, a.shape[-1])
        kb3, vb3 = r3(kb), r3(vb)
        ya = _attention(r3(q), kb3, vb3, cache_kt if cached else kb3, cache_vt if cached else vb3,
                        jnp.tile(sb_out_gain[l], LANES // hd)[None, :],
                        bq=bq, past_len=past_len, hd=hd, past_layer=l if cached else None)
        yb, s_fin = _hgrn(r3(qh), r3(kh), r3(lf), r3(ih), r3(gh), s0, hg_out_gain[l][None, :], dk=dk)
        wo = w_out[l].astype(BF16)
        return (ya.reshape(b * t, sw), yb.reshape(b * t, hw), wo[:sw], wo[sw:]), (kf, vf), s_fin

    def channel(x, mix, l):
        j = l // 2
        if l % 2 == 0:
            return _ffn(x, mix, norm_ffn[l][None, :], ffn_w_gate[j].astype(BF16), ffn_w_up[j].astype(BF16),
                        ffn_w_down[j].astype(BF16))
        return _moe(x, mix, norm_ffn[l][None, :], moe_router[j], moe_w_gate[j].astype(BF16),
                    moe_w_up[j].astype(BF16), moe_w_down[j].astype(BF16))

    bq_p = _tile(tp, 512)
    kv_p = kv_s = None
    for l in range(depth):
        mix_p, kv_p, s_p = mixer(xp, bp, tp, l, kv_p, cached=False, past_len=None, s0=zeros_state, bq=bq_p)
        outs["sp"].append(s_p)
        mix_s, kv_s, s_s = mixer(xs, bs, ts, l, kv_s, cached=True, past_len=past,
                                 s0=state_hgrn[l].astype(F32), bq=ts)
        outs["ss"].append(s_s)
        xp = channel(xp, mix_p, l)
        xs = channel(xs, mix_s, l)

    heads5 = lambda a, b, t: jnp.transpose(a.reshape(depth, b, sb_heads, hd, t), (0, 1, 4, 2, 3))
    return (xp.reshape(bp, tp, d), xs.reshape(bs, ts, d), heads5(kv_p[0], bp, tp), heads5(kv_p[1], bp, tp),
            jnp.stack(outs["sp"]), heads5(kv_s[0], bs, ts), heads5(kv_s[1], bs, ts), jnp.stack(outs["ss"]))
```

```python
import functools

import numpy as np
import jax
import jax.numpy as jnp
from jax import lax
from jax.experimental import pallas as pl
from jax.experimental.pallas import tpu as pltpu

EPS = 1e-6
TOP_K = 2
LANES = 128
F32 = jnp.float32
BF16 = jnp.bfloat16
VMEM_LIMIT_BYTES = 56 * 1024 * 1024


def _cparams(semantics):
    return pltpu.CompilerParams(dimension_semantics=semantics, vmem_limit_bytes=VMEM_LIMIT_BYTES)


def _dot(a, b):
    return jnp.dot(a, b, preferred_element_type=F32)


def _dot_nt(a, b):
    return lax.dot_general(a, b, (((1,), (1,)), ((), ())), preferred_element_type=F32)


def _dot_tn(a, b):
    return lax.dot_general(a, b, (((0,), (0,)), ((), ())), preferred_element_type=F32)


def _sigmoid(x):
    return 1.0 / (1.0 + jnp.exp(-x))


def _rms(x):
    return x * lax.rsqrt(jnp.mean(x * x, axis=-1, keepdims=True) + EPS)


def _tile(n, pref):
    t = min(n, pref)
    assert n % t == 0, (n, t)
    return t


def _inproj_kernel(*refs, sw, hw, hd, n_prev, seq_per_tile):
    x_ref, g_ref, w_ref, qg_ref, kg_ref, llb_ref, l1m_ref, oml_ref = refs[:8]
    prev = refs[8:8 + (2 if n_prev else 0)]
    q_ref, kf_ref, kb_ref, vf_ref, vb_ref, qh_ref, kh_ref, lf_ref, ih_ref, gh_ref = refs[8 + len(prev):]
    if n_prev:
        kf_ref[:n_prev] = prev[0][...]
        vf_ref[:n_prev] = prev[1][...]

    def store_time_minor(ref, y):
        t = y.shape[0] // seq_per_tile
        for sq in range(seq_per_tile):
            ref[n_prev, sq] = y[sq * t:(sq + 1) * t, :].T

    h = (_rms(x_ref[...]) * g_ref[...]).astype(BF16)

    def seg(lo, width):
        return _dot(h, w_ref[:, lo:lo + width])

    def headnorm(y, gain):
        lane = lax.broadcasted_iota(jnp.int32, (1, LANES), 1)
        tiles = []
        for c0 in range(0, y.shape[1], LANES):
            y2 = y[:, c0:c0 + LANES] * y[:, c0:c0 + LANES]
            m = jnp.zeros_like(y2)
            for h0 in range(0, LANES, hd):
                in_head = (lane >= h0) & (lane < h0 + hd)
                m = jnp.where(in_head, jnp.sum(jnp.where(in_head, y2, 0.0), axis=-1, keepdims=True), m)
            tiles.append(m * (1.0 / hd))
        m = jnp.concatenate(tiles, axis=1)
        return (y * lax.rsqrt(m + EPS)) * gain

    q_ref[...] = headnorm(seg(0, sw), qg_ref[...]).astype(BF16)
    ka = headnorm(seg(sw, sw), kg_ref[...])
    store_time_minor(kf_ref, ka)
    kb_ref[...] = ka.astype(BF16)
    va = seg(2 * sw, sw)
    store_time_minor(vf_ref, va)
    vb_ref[...] = va.astype(BF16)

    o = 3 * sw
    qb = seg(o, hw)
    qh_ref[...] = qb * _sigmoid(qb)

    fb = seg(o + hw, hw)
    e = jnp.exp(-jnp.abs(fb))
    log_sig = jnp.minimum(fb, 0.0) - jnp.log(1.0 + e)
    c = l1m_ref[...] + log_sig
    a = llb_ref[...]
    lf_ref[...] = jnp.maximum(a, c) + jnp.log(1.0 + jnp.exp(-jnp.abs(a - c)))
    kh_ref[...] = oml_ref[...] * (jnp.where(fb >= 0.0, e, 1.0) / (1.0 + e))

    ih_ref[...] = seg(o + 2 * hw, hw)
    gh_ref[...] = seg(o + 3 * hw, hw)


def _inproj(x, g, w, qg, kg, llb, l1m, oml, kv_prev, *, sw, hw, hd, seq_len):
    n, d = x.shape
    tm = _tile(n, 512)
    n_prev = 0 if kv_prev is None else kv_prev[0].shape[0]
    spt = max(tm // seq_len, 1)
    tt = tm // spt
    nt = seq_len // tt
    assert spt * tt == tm and nt * tt == seq_len
    row = lambda i: (i, 0)
    const = lambda i: (0, 0)
    stacked = lambda i: (0, i // nt, 0, i % nt)
    kv_shape = (n_prev + 1, n // seq_len, sw, seq_len)
    specs = {"row_sw": pl.BlockSpec((tm, sw), row), "row_hw": pl.BlockSpec((tm, hw), row),
             "stack": pl.BlockSpec((n_prev + 1, spt, sw, tt), stacked)}
    outs = [("row_sw", (n, sw), BF16), ("stack", kv_shape, F32), ("row_sw", (n, sw), BF16),
            ("stack", kv_shape, F32), ("row_sw", (n, sw), BF16)] + [("row_hw", (n, hw), F32)] * 5
    prev_specs = [pl.BlockSpec((n_prev, spt, sw, tt), stacked)] * 2 if n_prev else []
    return pl.pallas_call(
        functools.partial(_inproj_kernel, sw=sw, hw=hw, hd=hd, n_prev=n_prev, seq_per_tile=spt),
        grid=(n // tm,),
        in_specs=[pl.BlockSpec((tm, d), row), pl.BlockSpec((1, d), const),
                  pl.BlockSpec(w.shape, const),
                  pl.BlockSpec((1, sw), const), pl.BlockSpec((1, sw), const),
                  pl.BlockSpec((1, hw), const), pl.BlockSpec((1, hw), const), pl.BlockSpec((1, hw), const)]
        + prev_specs,
        out_specs=[specs[kind] for kind, _, _ in outs],
        out_shape=[jax.ShapeDtypeStruct(shape, dt) for _, shape, dt in outs],
        compiler_params=_cparams(("parallel",)),
        name="inproj",
    )(x, g, w, qg, kg, llb, l1m, oml, *(kv_prev or ()))


KEY_SUB = 256
SOFTPLUS_CLAMP = 80.0
CARRY_DEAD = 104.0


def _attn_kernel(q_ref, kn_ref, vn_ref, kp_hbm, vp_hbm, un_ref, up_ref, gain_ref,
                 y_ref, qm_ref, acc_ref, carry_ref, kbuf, vbuf, sem, alive_ref,
                 *, bq, subn, subp, hd, past_len, past_layer, past_transposed):
    bb = pl.program_id(0)
    p = pl.program_id(1)
    qi = pl.program_id(2)
    lane = lax.broadcasted_iota(jnp.int32, (1, LANES), 1)
    heads = LANES // hd
    n_past = (qi * bq if past_len is None else past_len) // subp

    def fetch(j, slot):
        if past_transposed:
            window = (past_layer, bb, pl.ds(p * LANES, LANES), pl.ds(j * subp, subp))
        else:
            window = (bb, pl.ds(j * subp, subp), pl.ds(p * LANES, LANES))
        return (pltpu.make_async_copy(kp_hbm.at[window], kbuf.at[slot], sem.at[0, slot]),
                pltpu.make_async_copy(vp_hbm.at[window], vbuf.at[slot], sem.at[1, slot]))

    def slot_of(j):
        return (n_past - 1 - j) & 1

    @pl.when(n_past > 0)
    def _():
        for cp in fetch(n_past - 1, 0):
            cp.start()

    q = q_ref[...]
    for hh in range(heads):
        in_head = (lane >= hh * hd) & (lane < (hh + 1) * hd)
        qm_ref[hh] = jnp.where(in_head, q, jnp.zeros_like(q))
    acc_ref[...] = jnp.zeros_like(acc_ref)
    carry_ref[...] = jnp.zeros_like(carry_ref)

    groups = bq // subn

    def one_head(hh, k, v, u, r0, r1, mask, transposed):
        if transposed:
            z = _dot(qm_ref[hh, r0:r1, :], k)
        else:
            z = _dot_nt(qm_ref[hh, r0:r1, :], k)
        sp = jnp.maximum(jnp.log(1.0 + jnp.exp(jnp.minimum(z, SOFTPLUS_CLAMP))), z)
        if mask is not None:
            sp = jnp.where(mask, sp, 0.0)
        after = _dot(sp.astype(BF16), u)
        w = jnp.exp((z - sp) - after)
        if mask is not None:
            w = jnp.where(mask, w, 0.0)
        pv = _dot_nt(w.astype(BF16), v) if transposed else _dot(w.astype(BF16), v)
        c = carry_ref[hh, r0:r1, :]
        acc_ref[hh, r0:r1, :] += jnp.exp(-c) * pv
        c_new = c + jnp.sum(sp, axis=-1, keepdims=True)
        carry_ref[hh, r0:r1, :] = c_new
        return c_new

    def set_alive(hh, g, c_group):
        alive_ref[hh * groups + g] = (jnp.min(c_group) < CARRY_DEAD).astype(jnp.int32)

    def diagonal_block(flag_groups):
        u_new = un_ref[...]
        for j in range(groups - 1, -1, -1):
            r0 = j * subn
            rows = r0 + lax.broadcasted_iota(jnp.int32, (bq - r0, subn), 0)
            cols = r0 + lax.broadcasted_iota(jnp.int32, (bq - r0, subn), 1)
            k = kn_ref[r0:r0 + subn, :].astype(BF16)
            v = vn_ref[r0:r0 + subn, :].astype(BF16)
            for hh in range(heads):
                c_new = one_head(hh, k, v, u_new, r0, bq, cols < rows, False)
                if j == 0:
                    for g in flag_groups:
                        set_alive(hh, g, c_new[g * subn:(g + 1) * subn])

    def group_heads(slot, g, head_list):
        k = kbuf[slot].astype(BF16)
        v = vbuf[slot].astype(BF16)
        u_past = up_ref[...]
        for hh in head_list:
            c_new = one_head(hh, k, v, u_past, g * subn, (g + 1) * subn, None, past_transposed)
            set_alive(hh, g, c_new)

    def alive_groups(slot, group_list):
        for g in group_list:
            n_alive = alive_ref[g]
            for hh in range(1, heads):
                n_alive += alive_ref[hh * groups + g]
            pl.when(n_alive == heads)(functools.partial(group_heads, slot, g, range(heads)))
            for hh in range(heads):
                pl.when((n_alive < heads) & (alive_ref[hh * groups + g] == 1))(
                    functools.partial(group_heads, slot, g, (hh,)))

    @pl.when(n_past == 0)
    def _():
        diagonal_block(range(groups))

    @pl.when(n_past > 0)
    def _():
        diagonal_block(range(1, groups))
        for cp in fetch(n_past - 1, 0):
            cp.wait()
        for cp in fetch(jnp.maximum(n_past - 2, 0), 1):
            cp.start()
        group_heads(0, 0, range(heads))
        alive_groups(0, range(1, groups))

    def any_alive():
        total = alive_ref[0]
        for i in range(1, heads * groups):
            total += alive_ref[i]
        return total > 0

    def past_block(j):
        slot = slot_of(j)
        for cp in fetch(j, slot):
            cp.wait()

        @pl.when(j > 0)
        def _():
            for cp in fetch(j - 1, 1 - slot):
                cp.start()

        alive_groups(slot, range(groups))
        return j - 1

    j_end = lax.while_loop(lambda j: (j >= 0) & any_alive(), past_block, n_past - 2)

    @pl.when(j_end >= 0)
    def _():
        for cp in fetch(j_end, slot_of(j_end)):
            cp.wait()

    @pl.when(n_past == 1)
    def _():
        for cp in fetch(0, 1):
            cp.wait()

    o = acc_ref[0]
    for hh in range(1, heads):
        in_head = (lane >= hh * hd) & (lane < (hh + 1) * hd)
        o = jnp.where(in_head, acc_ref[hh], o)
    o2 = o * o
    ms = jnp.zeros_like(o)
    for hh in range(heads):
        in_head = (lane >= hh * hd) & (lane < (hh + 1) * hd)
        ssum = jnp.sum(jnp.where(in_head, o2, 0.0), axis=-1, keepdims=True)
        ms = jnp.where(in_head, ssum * (1.0 / hd), ms)
    y_ref[...] = ((o * lax.rsqrt(ms + EPS)) * gain_ref[...]).astype(y_ref.dtype)


def _tril_strict(n):
    return jnp.asarray(np.tril(np.ones((n, n), np.float32), -1), BF16)


def _attention(q, k_new, v_new, k_past, v_past, gain, *, bq, past_len, hd, past_layer=None):
    b, tq, w = q.shape
    past_transposed = past_layer is not None
    subn = min(bq, KEY_SUB)
    subp = KEY_SUB
    assert bq % subn == 0 and (bq if past_len is None else past_len) % subp == 0
    heads = LANES // hd
    qmap = lambda bb, p, qi: (bb, qi, p)
    cmap = lambda bb, p, qi: (0, 0)
    past_buf = (2, LANES, subp) if past_transposed else (2, subp, LANES)
    return pl.pallas_call(
        functools.partial(_attn_kernel, bq=bq, subn=subn, subp=subp, hd=hd, past_len=past_len,
                          past_layer=past_layer, past_transposed=past_transposed),
        grid=(b, w // LANES, tq // bq),
        in_specs=[pl.BlockSpec((None, bq, LANES), qmap),
                  pl.BlockSpec((None, bq, LANES), qmap), pl.BlockSpec((None, bq, LANES), qmap),
                  pl.BlockSpec(memory_space=pl.ANY), pl.BlockSpec(memory_space=pl.ANY),
                  pl.BlockSpec((subn, subn), cmap), pl.BlockSpec((subp, subp), cmap),
                  pl.BlockSpec((1, LANES), cmap)],
        out_specs=pl.BlockSpec((None, bq, LANES), qmap),
        out_shape=jax.ShapeDtypeStruct((b, tq, w), BF16),
        scratch_shapes=[pltpu.VMEM((heads, bq, LANES), BF16), pltpu.VMEM((heads, bq, LANES), F32),
                        pltpu.VMEM((heads, bq, LANES), F32),
                        pltpu.VMEM(past_buf, k_past.dtype), pltpu.VMEM(past_buf, v_past.dtype),
                        pltpu.SemaphoreType.DMA((2, 2)), pltpu.SMEM((heads * (bq // subn),), jnp.int32)],
        compiler_params=_cparams(("parallel", "parallel", "arbitrary")),
        name="stickbreak_attn",
    )(q, k_new, v_new, k_past, v_past, _tril_strict(subn), _tril_strict(subp), gain)


def _hgrn_halvings(c):
    return [c >> (i + 1) for i in range(int(np.log2(c)))]


def _hgrn_masks(c):
    t = np.arange(c)
    masks = [np.eye(c, dtype=bool)]
    for h in _hgrn_halvings(c):
        blk = t // (2 * h)
        second = (t // h) % 2 == 1
        masks.append((blk[:, None] == blk[None, :]) & second[:, None] & (~second)[None, :])
    return jnp.asarray(np.stack(masks).astype(np.float32))


def _boundary_rows(b, h, row):
    c, dk = b.shape
    if 2 * h >= 8:
        n = c // (2 * h)
        ref = b.reshape(n, 2 * h, dk)[:, h - 1:h, :]
        return jnp.broadcast_to(ref, (n, 2 * h, dk)).reshape(c, dk)
    down1 = pltpu.roll(b, 1, 0)
    if h == 1:
        return jnp.where((row & 1) == 1, down1, b)
    m = row & 3
    up1 = pltpu.roll(b, c - 1, 0)
    down2 = pltpu.roll(b, 2, 0)
    return jnp.where(m == 0, up1, jnp.where(m == 1, b, jnp.where(m == 2, down1, down2)))


def _hgrn_kernel(q_ref, k_ref, lf_ref, v_ref, g_ref, s0_ref, tri_ref, msk_ref, gain_ref,
                 y_ref, sout_ref, st_ref, *, c, n_chunks):
    t = pl.program_id(2)
    dk = q_ref.shape[-1]

    @pl.when(t == 0)
    def _():
        st_ref[...] = s0_ref[...].T

    row = lax.broadcasted_iota(jnp.int32, (c, dk), 0)
    tri = tri_ref[...]
    for ci in range(n_chunks):
        sl = slice(ci * c, (ci + 1) * c)
        q = q_ref[sl, :]
        k = k_ref[sl, :]
        lf = lf_ref[sl, :]
        v = v_ref[sl, :].astype(BF16)
        g = g_ref[sl, :]
        hi = lf.astype(BF16)
        r1 = lf - hi.astype(F32)
        mid = r1.astype(BF16)
        lo = (r1 - mid.astype(F32)).astype(BF16)
        b = _dot(tri, hi) + _dot(tri, mid) + _dot(tri, lo)
        st = st_ref[...]
        o = _dot_nt((q * jnp.exp(b)).astype(BF16), st.astype(BF16))
        a = msk_ref[0] * _dot_nt(q.astype(BF16), k.astype(BF16))
        for lv, h in enumerate(_hgrn_halvings(c)):
            gap = b - _boundary_rows(b, h, row)
            el = jnp.exp(jnp.where((row & h) != 0, gap, -gap))
            a += msk_ref[1 + lv] * _dot_nt((q * el).astype(BF16), (k * el).astype(BF16))
        o += _dot(a.astype(BF16), v)
        b_last = b[c - 1:c, :]
        k_end = (k * jnp.exp(b_last - b)).astype(BF16)
        st_ref[...] = st * jnp.exp(b_last) + _dot_tn(v, k_end)
        y = (_rms(o) * gain_ref[...]) * (g * _sigmoid(g))
        y_ref[sl, :] = y.astype(y_ref.dtype)

    @pl.when(t == pl.num_programs(2) - 1)
    def _():
        sout_ref[...] = st_ref[...].T


def _hgrn(q, k, lf, v, g, s0, gain, *, dk):
    b, t, hw = q.shape
    nh = hw // dk
    c = _tile(t, 256)
    tt = _tile(t, 512)
    msk = _hgrn_masks(c)
    tri = jnp.asarray(np.tril(np.ones((c, c), np.float32)), BF16)
    tok = lambda bb, h, ti: (bb, ti, h)
    smap = lambda bb, h, ti: (bb, h, 0, 0)
    return pl.pallas_call(
        functools.partial(_hgrn_kernel, c=c, n_chunks=tt // c),
        grid=(b, nh, t // tt),
        in_specs=[pl.BlockSpec((None, tt, dk), tok)] * 5
        + [pl.BlockSpec((None, None, dk, dk), smap),
           pl.BlockSpec((c, c), lambda bb, h, ti: (0, 0)),
           pl.BlockSpec(msk.shape, lambda bb, h, ti: (0, 0, 0)),
           pl.BlockSpec((1, dk), lambda bb, h, ti: (0, 0))],
        out_specs=[pl.BlockSpec((None, tt, dk), tok), pl.BlockSpec((None, None, dk, dk), smap)],
        out_shape=[jax.ShapeDtypeStruct((b, t, hw), BF16), jax.ShapeDtypeStruct((b, nh, dk, dk), F32)],
        scratch_shapes=[pltpu.VMEM((dk, dk), F32)],
        compiler_params=_cparams(("parallel", "parallel", "arbitrary")),
        name="hgrn2",
    )(q, k, lf, v, g, s0, tri, msk, gain)


def _mixed_residual(x_ref, ya_ref, yb_ref, wa_ref, wb_ref):
    return x_ref[...] + _dot(ya_ref[...], wa_ref[...]) + _dot(yb_ref[...], wb_ref[...])


def _mix_specs(mix, tm, index_row, index_const):
    ya, yb, wa, wb = mix
    return [pl.BlockSpec((tm, ya.shape[1]), index_row), pl.BlockSpec((tm, yb.shape[1]), index_row),
            pl.BlockSpec(wa.shape, index_const), pl.BlockSpec(wb.shape, index_const)]


FF_SUB = 256
FF_ROWS = 512


def _swiglu(x, g_ref, wg_ref, wu_ref, wd_ref, acc_ref):
    h = (_rms(x) * g_ref[...]).astype(BF16)
    ff = wg_ref.shape[-1]
    for c0 in range(0, ff, FF_SUB):
        c1 = min(c0 + FF_SUB, ff)
        gate = _dot(h, wg_ref[:, c0:c1])
        up = _dot(h, wu_ref[:, c0:c1])
        part = _dot(((gate * _sigmoid(gate)) * up).astype(BF16), wd_ref[c0:c1, :])
        if c0 == 0:
            acc_ref[...] = part
        else:
            acc_ref[...] += part


def _ffn_kernel(x_ref, ya_ref, yb_ref, wa_ref, wb_ref, g_ref, wg_ref, wu_ref, wd_ref, o_ref, acc_ref):
    x = _mixed_residual(x_ref, ya_ref, yb_ref, wa_ref, wb_ref)
    _swiglu(x, g_ref, wg_ref, wu_ref, wd_ref, acc_ref)
    o_ref[...] = x + acc_ref[...]


def _ffn(x, mix, g, wg, wu, wd):
    n, d = x.shape
    tm = _tile(n, FF_ROWS)
    row = lambda i: (i, 0)
    const = lambda i: (0, 0)
    return pl.pallas_call(
        _ffn_kernel,
        grid=(n // tm,),
        in_specs=[pl.BlockSpec((tm, d), row)] + _mix_specs(mix, tm, row, const)
        + [pl.BlockSpec((1, d), const),
           pl.BlockSpec(wg.shape, const), pl.BlockSpec(wu.shape, const), pl.BlockSpec(wd.shape, const)],
        out_specs=pl.BlockSpec((tm, d), row),
        out_shape=jax.ShapeDtypeStruct((n, d), F32),
        scratch_shapes=[pltpu.VMEM((tm, d), F32)],
        compiler_params=_cparams(("parallel",)),
        name="ffn_dense",
    )(x, *mix, g, wg, wu, wd)


ROUTE_E1, ROUTE_E2, ROUTE_R1, ROUTE_R2, ROUTE_G1, ROUTE_G2 = range(6)


def _router_kernel(x_ref, ya_ref, yb_ref, wa_ref, wb_ref, g_ref, rhi_ref, rlo_ref, tri_ref,
                   x1_ref, route_ref, cnt_ref, base_ref, *, n_experts):
    i = pl.program_id(0)
    lane = lax.broadcasted_iota(jnp.int32, (1, LANES), 1)

    @pl.when(i == 0)
    def _():
        base_ref[...] = jnp.zeros_like(base_ref)

    x1 = _mixed_residual(x_ref, ya_ref, yb_ref, wa_ref, wb_ref)
    x1_ref[...] = x1
    hf = _rms(x1) * g_ref[...]
    h_hi = hf.astype(BF16)
    h_lo = (hf - h_hi.astype(F32)).astype(BF16)
    logits = _dot(h_hi, rhi_ref[...]) + _dot(h_lo, rhi_ref[...]) + _dot(h_hi, rlo_ref[...])
    logits = jnp.where(lane < n_experts, logits, -jnp.inf)
    m1 = jnp.max(logits, axis=-1, keepdims=True)
    i1 = jnp.min(jnp.where(logits == m1, lane, LANES), axis=-1, keepdims=True)
    rest = jnp.where(lane == i1, -jnp.inf, logits)
    m2 = jnp.max(rest, axis=-1, keepdims=True)
    i2 = jnp.min(jnp.where(rest == m2, lane, LANES), axis=-1, keepdims=True)
    e2 = jnp.exp(m2 - m1)
    g1 = 1.0 / (1.0 + e2)
    g2 = e2 / (1.0 + e2)
    oh1 = (lane == i1).astype(F32)
    oh2 = (lane == i2).astype(F32)
    both = oh1 + oh2
    before = _dot(tri_ref[...], both.astype(BF16)) + base_ref[...]
    r1 = jnp.sum(oh1 * before, axis=-1, keepdims=True)
    r2 = jnp.sum(oh2 * before, axis=-1, keepdims=True)
    rec = jnp.zeros(route_ref.shape, F32)
    for slot, val in ((ROUTE_E1, i1.astype(F32)), (ROUTE_E2, i2.astype(F32)), (ROUTE_R1, r1),
                      (ROUTE_R2, r2), (ROUTE_G1, g1), (ROUTE_G2, g2)):
        rec = jnp.where(lane == slot, val, rec)
    route_ref[...] = rec
    base_ref[...] += jnp.sum(both, axis=0, keepdims=True)
    cnt_ref[...] = base_ref[...]


def _router(x, mix, g, router):
    n, d = x.shape
    ne = router.shape[1]
    tm = _tile(n, 1024)
    rpad = jnp.zeros((d, LANES), F32).at[:, :ne].set(router)
    rhi = rpad.astype(BF16)
    rlo = (rpad - rhi.astype(F32)).astype(BF16)
    row = lambda i: (i, 0)
    const = lambda i: (0, 0)
    return pl.pallas_call(
        functools.partial(_router_kernel, n_experts=ne),
        grid=(n // tm,),
        in_specs=[pl.BlockSpec((tm, d), row)] + _mix_specs(mix, tm, row, const)
        + [pl.BlockSpec((1, d), const),
           pl.BlockSpec((d, LANES), const), pl.BlockSpec((d, LANES), const),
           pl.BlockSpec((tm, tm), const)],
        out_specs=[pl.BlockSpec((tm, d), row), pl.BlockSpec((tm, LANES), row), pl.BlockSpec((1, LANES), const)],
        out_shape=[jax.ShapeDtypeStruct((n, d), F32), jax.ShapeDtypeStruct((n, LANES), F32),
                   jax.ShapeDtypeStruct((1, LANES), F32)],
        scratch_shapes=[pltpu.VMEM((1, LANES), F32)],
        compiler_params=_cparams(("arbitrary",)),
        name="moe_router",
    )(x, *mix, g, rhi, rlo, _tril_strict(tm))


def _row_copy(src_ref, src_row, dst_ref, dst_row, sem):
    return pltpu.make_async_copy(src_ref.at[pl.ds(src_row, 1), :], dst_ref.at[pl.ds(dst_row, 1), :], sem)


def _dispatch_kernel(pad_lo_ref, pad_hi_ref, p1_ref, p2_ref, x_ref, xs_ref, sem, pad_sem, *, tb, n_pad_ranges):
    def issue(t, carry):
        _row_copy(x_ref, t, xs_ref, p1_ref[0, 0, t], sem).start(priority=0)
        _row_copy(x_ref, t, xs_ref, p2_ref[0, 0, t], sem).start(priority=1)
        return carry

    lax.fori_loop(0, tb, issue, 0, unroll=8)

    @pl.when(pl.program_id(0) == 0)
    def _():
        def fill(r, carry):
            _row_copy(x_ref, 0, xs_ref, r, pad_sem).start()
            return carry

        def drain(r, carry):
            _row_copy(x_ref, 0, xs_ref, r, pad_sem).wait()
            return carry

        for e in range(n_pad_ranges):
            lax.fori_loop(pad_lo_ref[e], pad_hi_ref[e], fill, 0)
        for e in range(n_pad_ranges):
            lax.fori_loop(pad_lo_ref[e], pad_hi_ref[e], drain, 0)

    for _ in range(TOP_K):
        pltpu.make_async_copy(x_ref, xs_ref.at[pl.ds(0, tb), :], sem).wait()


def _dispatch(x, pos1, pos2, n_rows, pad_lo, pad_hi):
    n, d = x.shape
    tb = pos1.shape[-1]
    smem = lambda: pl.BlockSpec((1, 1, tb), lambda i, lo, hi: (i, 0, 0), memory_space=pltpu.SMEM)
    grid_spec = pltpu.PrefetchScalarGridSpec(
        num_scalar_prefetch=2,
        grid=(n // tb,),
        in_specs=[smem(), smem(), pl.BlockSpec((tb, d), lambda i, lo, hi: (i, 0))],
        out_specs=pl.BlockSpec(memory_space=pl.ANY),
        scratch_shapes=[pltpu.SemaphoreType.DMA(()), pltpu.SemaphoreType.DMA(())],
    )
    return pl.pallas_call(
        functools.partial(_dispatch_kernel, tb=tb, n_pad_ranges=pad_lo.shape[0]),
        grid_spec=grid_spec,
        out_shape=jax.ShapeDtypeStruct((n_rows, d), F32),
        compiler_params=_cparams(("arbitrary",)),
        name="moe_dispatch",
    )(pad_lo, pad_hi, pos1, pos2, x)


def _experts_kernel(te_ref, nv_ref, x_ref, g_ref, wg_ref, wu_ref, wd_ref, o_ref, acc_ref):
    del te_ref
    valid = pl.program_id(0) < nv_ref[0]

    @pl.when(valid)
    def _():
        _swiglu(x_ref[...], g_ref, wg_ref, wu_ref, wd_ref, acc_ref)
        o_ref[...] = acc_ref[...]

    @pl.when(jnp.logical_not(valid))
    def _():
        o_ref[...] = jnp.zeros_like(o_ref)


def _experts(xs, g, wg, wu, wd, tile_expert, n_valid, tr):
    n_rows, d = xs.shape
    ff = wg.shape[2]
    rowmap = lambda i, te, nv: (jnp.minimum(i, nv[0] - 1), 0)
    wmap = lambda i, te, nv: (te[i], 0, 0)
    grid_spec = pltpu.PrefetchScalarGridSpec(
        num_scalar_prefetch=2,
        grid=(n_rows // tr,),
        in_specs=[pl.BlockSpec((tr, d), rowmap), pl.BlockSpec((1, d), lambda i, te, nv: (0, 0)),
                  pl.BlockSpec((None, d, ff), wmap), pl.BlockSpec((None, d, ff), wmap),
                  pl.BlockSpec((None, ff, d), wmap)],
        out_specs=pl.BlockSpec((tr, d), lambda i, te, nv: (i, 0)),
        scratch_shapes=[pltpu.VMEM((tr, d), F32)],
    )
    return pl.pallas_call(
        _experts_kernel,
        grid_spec=grid_spec,
        out_shape=jax.ShapeDtypeStruct((n_rows, d), F32),
        compiler_params=_cparams(("arbitrary",)),
        name="moe_experts",
    )(tile_expert, n_valid, xs, g, wg, wu, wd)


def _combine_kernel(p1_ref, p2_ref, x_ref, route_ref, ys_ref, o_ref, buf_ref, sem, *, tc):
    def issue(t, carry):
        _row_copy(ys_ref, p1_ref[0, 0, t], buf_ref.at[0], t, sem).start(priority=0)
        _row_copy(ys_ref, p2_ref[0, 0, t], buf_ref.at[1], t, sem).start(priority=1)
        return carry

    lax.fori_loop(0, tc, issue, 0, unroll=8)
    for slot in range(TOP_K):
        pltpu.make_async_copy(ys_ref.at[pl.ds(0, tc), :], buf_ref.at[slot], sem).wait()
    lane = lax.broadcasted_iota(jnp.int32, (1, LANES), 1)
    route = route_ref[...]
    g1 = jnp.sum(jnp.where(lane == ROUTE_G1, route, 0.0), axis=-1, keepdims=True)
    g2 = jnp.sum(jnp.where(lane == ROUTE_G2, route, 0.0), axis=-1, keepdims=True)
    o_ref[...] = x_ref[...] + (g1 * buf_ref[0] + g2 * buf_ref[1])


def _combine(x, route, ys, pos1, pos2):
    n, d = x.shape
    tc = pos1.shape[-1]
    smem = lambda: pl.BlockSpec((1, 1, tc), lambda i: (i, 0, 0), memory_space=pltpu.SMEM)
    return pl.pallas_call(
        functools.partial(_combine_kernel, tc=tc),
        grid=(n // tc,),
        in_specs=[smem(), smem(), pl.BlockSpec((tc, d), lambda i: (i, 0)),
                  pl.BlockSpec((tc, LANES), lambda i: (i, 0)), pl.BlockSpec(memory_space=pl.ANY)],
        out_specs=pl.BlockSpec((tc, d), lambda i: (i, 0)),
        out_shape=jax.ShapeDtypeStruct((n, d), F32),
        scratch_shapes=[pltpu.VMEM((TOP_K, tc, d), F32), pltpu.SemaphoreType.DMA(())],
        compiler_params=_cparams(("arbitrary",)),
        name="moe_combine",
    )(pos1, pos2, x, route, ys)


def _moe(x, mix, g, router, wg, wu, wd):
    n, d = x.shape
    ne = wg.shape[0]
    tr = FF_ROWS if n >= 8192 else 256
    tb = _tile(n, 512)
    x, route, cnt = _router(x, mix, g, router)
    counts = cnt[0, :ne].astype(jnp.int32)
    padded = ((counts + tr - 1) // tr) * tr
    ends = jnp.cumsum(padded)
    starts = ends - padded
    e1 = route[:, ROUTE_E1].astype(jnp.int32)
    e2 = route[:, ROUTE_E2].astype(jnp.int32)
    pos1 = (starts[e1] + route[:, ROUTE_R1].astype(jnp.int32)).reshape(n // tb, 1, tb)
    pos2 = (starts[e2] + route[:, ROUTE_R2].astype(jnp.int32)).reshape(n // tb, 1, tb)
    max_tiles = -(-(TOP_K * n + ne * (tr - 1)) // tr)
    n_valid = (ends[-1] // tr).reshape(1)
    tile_start = jnp.minimum(jnp.arange(max_tiles, dtype=jnp.int32), n_valid[0] - 1) * tr
    tile_expert = jnp.minimum(jnp.searchsorted(ends, tile_start, side="right"), ne - 1).astype(jnp.int32)
    n_rows = max_tiles * tr
    pad_lo = jnp.concatenate([starts + counts, ends[-1:]]).astype(jnp.int32)
    pad_hi = jnp.concatenate([ends, jnp.full((1,), n_rows, ends.dtype)]).astype(jnp.int32)
    xs = _dispatch(x, pos1, pos2, n_rows, pad_lo, pad_hi)
    ys = _experts(xs, g, wg, wu, wd, tile_expert, n_valid.astype(jnp.int32), tr)
    return _combine(x, route, ys, pos1, pos2)


def kernel(x_prompt, x_sample, cache_k, cache_v, state_hgrn, norm_mix, w_in, sb_q_gain, sb_k_gain,
           hg_lower_bounds, sb_out_gain, hg_out_gain, w_out, norm_ffn, ffn_w_gate, ffn_w_up, ffn_w_down,
           moe_router, moe_w_gate, moe_w_up, moe_w_down):
    depth = w_in.shape[0]
    bp, tp, d = x_prompt.shape
    bs, ts, _ = x_sample.shape
    past = cache_k.shape[2]
    sb_heads, hd = cache_k.shape[3], cache_k.shape[4]
    sw = sb_heads * hd
    dk = hg_out_gain.shape[1]
    hw = hg_lower_bounds.shape[1]
    hg_heads = hw // dk
    sb_scale = hd ** -0.5

    lbs = jnp.cumsum(jax.nn.softmax(hg_lower_bounds.astype(F32), axis=0), axis=0)
    lbs = lbs - lbs[0:1]

    xp = x_prompt.reshape(bp * tp, d)
    xs = x_sample.reshape(bs * ts, d)
    zeros_state = jnp.zeros((bp, hg_heads, dk, dk), F32)
    outs = {k: [] for k in ("sp", "ss")}
    cache_kt = jnp.transpose(cache_k, (0, 1, 3, 4, 2)).reshape(depth, bs, sw, past)
    cache_vt = jnp.transpose(cache_v, (0, 1, 3, 4, 2)).reshape(depth, bs, sw, past)

    def mixer(x, b, t, l, kv_prev, *, cached, past_len, s0, bq):
        qg = (jnp.tile(sb_q_gain[l], sb_heads) * sb_scale)[None, :]
        kg = jnp.tile(sb_k_gain[l], sb_heads)[None, :]
        q, kf, kb, vf, vb, qh, kh, lf, ih, gh = _inproj(
            x, norm_mix[l][None, :], w_in[l].astype(BF16), qg, kg,
            jnp.log(lbs[l])[None, :], jnp.log1p(-lbs[l])[None, :], (1.0 - lbs[l])[None, :], kv_prev,
            sw=sw, hw=hw, hd=hd, seq_len=t)
        r3 = lambda a: a.reshape(b, t, a.shape[-1])
        kb3, vb3 = r3(kb), r3(vb)
        ya = _attention(r3(q), kb3, vb3, cache_kt if cached else kb3, cache_vt if cached else vb3,
                        jnp.tile(sb_out_gain[l], LANES // hd)[None, :],
                        bq=bq, past_len=past_len, hd=hd, past_layer=l if cached else None)
        yb, s_fin = _hgrn(r3(qh), r3(kh), r3(lf), r3(ih), r3(gh), s0, hg_out_gain[l][None, :], dk=dk)
        wo = w_out[l].astype(BF16)
        return (ya.reshape(b * t, sw), yb.reshape(b * t, hw), wo[:sw], wo[sw:]), (kf, vf), s_fin

    def channel(x, mix, l):
        j = l // 2
        if l % 2 == 0:
            return _ffn(x, mix, norm_ffn[l][None, :], ffn_w_gate[j].astype(BF16), ffn_w_up[j].astype(BF16),
                        ffn_w_down[j].astype(BF16))
        return _moe(x, mix, norm_ffn[l][None, :], moe_router[j], moe_w_gate[j].astype(BF16),
                    moe_w_up[j].astype(BF16), moe_w_down[j].astype(BF16))

    bq_p = _tile(tp, 512)
    kv_p = kv_s = None
    for l in range(depth):
        mix_p, kv_p, s_p = mixer(xp, bp, tp, l, kv_p, cached=False, past_len=None, s0=zeros_state, bq=bq_p)
        outs["sp"].append(s_p)
        mix_s, kv_s, s_s = mixer(xs, bs, ts, l, kv_s, cached=True, past_len=past,
                                 s0=state_hgrn[l].astype(F32), bq=ts)
        outs["ss"].append(s_s)
        xp = channel(xp, mix_p, l)
        xs = channel(xs, mix_s, l)

    heads5 = lambda a, b, t: jnp.transpose(a.reshape(depth, b, sb_heads, hd, t), (0, 1, 4, 2, 3))
    return (xp.reshape(bp, tp, d), xs.reshape(bs, ts, d), heads5(kv_p[0], bp, tp), heads5(kv_p[1], bp, tp),
            jnp.stack(outs["sp"]), heads5(kv_s[0], bs, ts), heads5(kv_s[1], bs, ts), jnp.stack(outs["ss"]))
```

```python
import functools

import numpy as np
import jax
import jax.numpy as jnp
from jax import lax
from jax.experimental import pallas as pl
from jax.experimental.pallas import tpu as pltpu

EPS = 1e-6
TOP_K = 2
LANES = 128
F32 = jnp.float32
BF16 = jnp.bfloat16
VMEM_LIMIT_BYTES = 56 * 1024 * 1024


def _cparams(semantics):
    return pltpu.CompilerParams(dimension_semantics=semantics, vmem_limit_bytes=VMEM_LIMIT_BYTES)


def _dot(a, b):
    return jnp.dot(a, b, preferred_element_type=F32)


def _dot_nt(a, b):
    return lax.dot_general(a, b, (((1,), (1,)), ((), ())), preferred_element_type=F32)


def _dot_tn(a, b):
    return lax.dot_general(a, b, (((0,), (0,)), ((), ())), preferred_element_type=F32)


def _sigmoid(x):
    return 1.0 / (1.0 + jnp.exp(-x))


def _rms(x):
    return x * lax.rsqrt(jnp.mean(x * x, axis=-1, keepdims=True) + EPS)


def _tile(n, pref):
    t = min(n, pref)
    assert n % t == 0, (n, t)
    return t


def _inproj_kernel(*refs, sw, hw, hd, n_prev, seq_per_tile):
    x_ref, g_ref, w_ref, qg_ref, kg_ref, llb_ref, l1m_ref, oml_ref = refs[:8]
    prev = refs[8:8 + (2 if n_prev else 0)]
    q_ref, kf_ref, kb_ref, vf_ref, vb_ref, qh_ref, kh_ref, lf_ref, ih_ref, gh_ref = refs[8 + len(prev):]
    if n_prev:
        kf_ref[:n_prev] = prev[0][...]
        vf_ref[:n_prev] = prev[1][...]

    def store_time_minor(ref, y):
        t = y.shape[0] // seq_per_tile
        for sq in range(seq_per_tile):
            ref[n_prev, sq] = y[sq * t:(sq + 1) * t, :].T

    h = (_rms(x_ref[...]) * g_ref[...]).astype(BF16)

    def seg(lo, width):
        return _dot(h, w_ref[:, lo:lo + width])

    def headnorm(y, gain):
        lane = lax.broadcasted_iota(jnp.int32, (1, LANES), 1)
        tiles = []
        for c0 in range(0, y.shape[1], LANES):
            y2 = y[:, c0:c0 + LANES] * y[:, c0:c0 + LANES]
            m = jnp.zeros_like(y2)
            for h0 in range(0, LANES, hd):
                in_head = (lane >= h0) & (lane < h0 + hd)
                m = jnp.where(in_head, jnp.sum(jnp.where(in_head, y2, 0.0), axis=-1, keepdims=True), m)
            tiles.append(m * (1.0 / hd))
        m = jnp.concatenate(tiles, axis=1)
        return (y * lax.rsqrt(m + EPS)) * gain

    q_ref[...] = headnorm(seg(0, sw), qg_ref[...]).astype(BF16)
    ka = headnorm(seg(sw, sw), kg_ref[...])
    store_time_minor(kf_ref, ka)
    kb_ref[...] = ka.astype(BF16)
    va = seg(2 * sw, sw)
    store_time_minor(vf_ref, va)
    vb_ref[...] = va.astype(BF16)

    o = 3 * sw
    qb = seg(o, hw)
    qh_ref[...] = qb * _sigmoid(qb)

    fb = seg(o + hw, hw)
    e = jnp.exp(-jnp.abs(fb))
    log_sig = jnp.minimum(fb, 0.0) - jnp.log(1.0 + e)
    c = l1m_ref[...] + log_sig
    a = llb_ref[...]
    lf_ref[...] = jnp.maximum(a, c) + jnp.log(1.0 + jnp.exp(-jnp.abs(a - c)))
    kh_ref[...] = oml_ref[...] * (jnp.where(fb >= 0.0, e, 1.0) / (1.0 + e))

    ih_ref[...] = seg(o + 2 * hw, hw)
    gh_ref[...] = seg(o + 3 * hw, hw)


def _inproj(x, g, w, qg, kg, llb, l1m, oml, kv_prev, *, sw, hw, hd, seq_len):
    n, d = x.shape
    tm = _tile(n, 512)
    n_prev = 0 if kv_prev is None else kv_prev[0].shape[0]
    spt = max(tm // seq_len, 1)
    tt = tm // spt
    nt = seq_len // tt
    assert spt * tt == tm and nt * tt == seq_len
    row = lambda i: (i, 0)
    const = lambda i: (0, 0)
    stacked = lambda i: (0, i // nt, 0, i % nt)
    kv_shape = (n_prev + 1, n // seq_len, sw, seq_len)
    specs = {"row_sw": pl.BlockSpec((tm, sw), row), "row_hw": pl.BlockSpec((tm, hw), row),
             "stack": pl.BlockSpec((n_prev + 1, spt, sw, tt), stacked)}
    outs = [("row_sw", (n, sw), BF16), ("stack", kv_shape, F32), ("row_sw", (n, sw), BF16),
            ("stack", kv_shape, F32), ("row_sw", (n, sw), BF16)] + [("row_hw", (n, hw), F32)] * 5
    prev_specs = [pl.BlockSpec((n_prev, spt, sw, tt), stacked)] * 2 if n_prev else []
    return pl.pallas_call(
        functools.partial(_inproj_kernel, sw=sw, hw=hw, hd=hd, n_prev=n_prev, seq_per_tile=spt),
        grid=(n // tm,),
        in_specs=[pl.BlockSpec((tm, d), row), pl.BlockSpec((1, d), const),
                  pl.BlockSpec(w.shape, const),
                  pl.BlockSpec((1, sw), const), pl.BlockSpec((1, sw), const),
                  pl.BlockSpec((1, hw), const), pl.BlockSpec((1, hw), const), pl.BlockSpec((1, hw), const)]
        + prev_specs,
        out_specs=[specs[kind] for kind, _, _ in outs],
        out_shape=[jax.ShapeDtypeStruct(shape, dt) for _, shape, dt in outs],
        compiler_params=_cparams(("parallel",)),
        name="inproj",
    )(x, g, w, qg, kg, llb, l1m, oml, *(kv_prev or ()))


KEY_SUB = 256
SOFTPLUS_CLAMP = 80.0
CARRY_DEAD = 104.0


def _attn_kernel(q_ref, kn_ref, vn_ref, kp_hbm, vp_hbm, un_ref, up_ref, gain_ref,
                 y_ref, qm_ref, acc_ref, carry_ref, kbuf, vbuf, sem, alive_ref,
                 *, bq, subn, subp, hd, past_len, past_layer, past_transposed):
    bb = pl.program_id(0)
    p = pl.program_id(1)
    qi = pl.program_id(2)
    lane = lax.broadcasted_iota(jnp.int32, (1, LANES), 1)
    heads = LANES // hd
    n_past = (qi * bq if past_len is None else past_len) // subp

    def fetch(j, slot):
        if past_transposed:
            window = (past_layer, bb, pl.ds(p * LANES, LANES), pl.ds(j * subp, subp))
        else:
            window = (bb, pl.ds(j * subp, subp), pl.ds(p * LANES, LANES))
        return (pltpu.make_async_copy(kp_hbm.at[window], kbuf.at[slot], sem.at[0, slot]),
                pltpu.make_async_copy(vp_hbm.at[window], vbuf.at[slot], sem.at[1, slot]))

    def slot_of(j):
        return (n_past - 1 - j) & 1

    @pl.when(n_past > 0)
    def _():
        for cp in fetch(n_past - 1, 0):
            cp.start()

    q = q_ref[...]
    for hh in range(heads):
        in_head = (lane >= hh * hd) & (lane < (hh + 1) * hd)
        qm_ref[hh] = jnp.where(in_head, q, jnp.zeros_like(q))
    acc_ref[...] = jnp.zeros_like(acc_ref)
    carry_ref[...] = jnp.zeros_like(carry_ref)

    groups = bq // subn

    def one_head(hh, k, v, u, r0, r1, mask, transposed):
        if transposed:
            z = _dot(qm_ref[hh, r0:r1, :], k)
        else:
            z = _dot_nt(qm_ref[hh, r0:r1, :], k)
        sp = jnp.maximum(jnp.log(1.0 + jnp.exp(jnp.minimum(z, SOFTPLUS_CLAMP))), z)
        if mask is not None:
            sp = jnp.where(mask, sp, 0.0)
        after = _dot(sp.astype(BF16), u)
        w = jnp.exp((z - sp) - after)
        if mask is not None:
            w = jnp.where(mask, w, 0.0)
        pv = _dot_nt(w.astype(BF16), v) if transposed else _dot(w.astype(BF16), v)
        c = carry_ref[hh, r0:r1, :]
        acc_ref[hh, r0:r1, :] += jnp.exp(-c) * pv
        c_new = c + jnp.sum(sp, axis=-1, keepdims=True)
        carry_ref[hh, r0:r1, :] = c_new
        return c_new

    def set_alive(hh, g, c_group):
        alive_ref[hh * groups + g] = (jnp.min(c_group) < CARRY_DEAD).astype(jnp.int32)

    def diagonal_block(flag_groups):
        u_new = un_ref[...]
        for j in range(groups - 1, -1, -1):
            r0 = j * subn
            rows = r0 + lax.broadcasted_iota(jnp.int32, (bq - r0, subn), 0)
            cols = r0 + lax.broadcasted_iota(jnp.int32, (bq - r0, subn), 1)
            k = kn_ref[r0:r0 + subn, :].astype(BF16)
            v = vn_ref[r0:r0 + subn, :].astype(BF16)
            for hh in range(heads):
                c_new = one_head(hh, k, v, u_new, r0, bq, cols < rows, False)
                if j == 0:
                    for g in flag_groups:
                        set_alive(hh, g, c_new[g * subn:(g + 1) * subn])

    def group_heads(slot, g, head_list):
        k = kbuf[slot].astype(BF16)
        v = vbuf[slot].astype(BF16)
        u_past = up_ref[...]
        for hh in head_list:
            c_new = one_head(hh, k, v, u_past, g * subn, (g + 1) * subn, None, past_transposed)
            set_alive(hh, g, c_new)

    def alive_groups(slot, group_list):
        for g in group_list:
            n_alive = alive_ref[g]
            for hh in range(1, heads):
                n_alive += alive_ref[hh * groups + g]
            pl.when(n_alive == heads)(functools.partial(group_heads, slot, g, range(heads)))
            for hh in range(heads):
                pl.when((n_alive < heads) & (alive_ref[hh * groups + g] == 1))(
                    functools.partial(group_heads, slot, g, (hh,)))

    @pl.when(n_past == 0)
    def _():
        diagonal_block(range(groups))

    @pl.when(n_past > 0)
    def _():
        diagonal_block(range(1, groups))
        for cp in fetch(n_past - 1, 0):
            cp.wait()
        for cp in fetch(jnp.maximum(n_past - 2, 0), 1):
            cp.start()
        group_heads(0, 0, range(heads))
        alive_groups(0, range(1, groups))

    def any_alive():
        total = alive_ref[0]
        for i in range(1, heads * groups):
            total += alive_ref[i]
        return total > 0

    def past_block(j):
        slot = slot_of(j)
        for cp in fetch(j, slot):
            cp.wait()

        @pl.when(j > 0)
        def _():
            for cp in fetch(j - 1, 1 - slot):
                cp.start()

        alive_groups(slot, range(groups))
        return j - 1

    j_end = lax.while_loop(lambda j: (j >= 0) & any_alive(), past_block, n_past - 2)

    @pl.when(j_end >= 0)
    def _():
        for cp in fetch(j_end, slot_of(j_end)):
            cp.wait()

    @pl.when(n_past == 1)
    def _():
        for cp in fetch(0, 1):
            cp.wait()

    o = acc_ref[0]
    for hh in range(1, heads):
        in_head = (lane >= hh * hd) & (lane < (hh + 1) * hd)
        o = jnp.where(in_head, acc_ref[hh], o)
    o2 = o * o
    ms = jnp.zeros_like(o)
    for hh in range(heads):
        in_head = (lane >= hh * hd) & (lane < (hh + 1) * hd)
        ssum = jnp.sum(jnp.where(in_head, o2, 0.0), axis=-1, keepdims=True)
        ms = jnp.where(in_head, ssum * (1.0 / hd), ms)
    y_ref[...] = ((o * lax.rsqrt(ms + EPS)) * gain_ref[...]).astype(y_ref.dtype)


def _tril_strict(n):
    return jnp.asarray(np.tril(np.ones((n, n), np.float32), -1), BF16)


def _attention(q, k_new, v_new, k_past, v_past, gain, *, bq, past_len, hd, past_layer=None):
    b, tq, w = q.shape
    past_transposed = past_layer is not None
    subn = min(bq, KEY_SUB)
    subp = KEY_SUB
    assert bq % subn == 0 and (bq if past_len is None else past_len) % subp == 0
    heads = LANES // hd
    qmap = lambda bb, p, qi: (bb, qi, p)
    cmap = lambda bb, p, qi: (0, 0)
    past_buf = (2, LANES, subp) if past_transposed else (2, subp, LANES)
    return pl.pallas_call(
        functools.partial(_attn_kernel, bq=bq, subn=subn, subp=subp, hd=hd, past_len=past_len,
                          past_layer=past_layer, past_transposed=past_transposed),
        grid=(b, w // LANES, tq // bq),
        in_specs=[pl.BlockSpec((None, bq, LANES), qmap),
                  pl.BlockSpec((None, bq, LANES), qmap), pl.BlockSpec((None, bq, LANES), qmap),
                  pl.BlockSpec(memory_space=pl.ANY), pl.BlockSpec(memory_space=pl.ANY),
                  pl.BlockSpec((subn, subn), cmap), pl.BlockSpec((subp, subp), cmap),
                  pl.BlockSpec((1, LANES), cmap)],
        out_specs=pl.BlockSpec((None, bq, LANES), qmap),
        out_shape=jax.ShapeDtypeStruct((b, tq, w), BF16),
        scratch_shapes=[pltpu.VMEM((heads, bq, LANES), BF16), pltpu.VMEM((heads, bq, LANES), F32),
                        pltpu.VMEM((heads, bq, LANES), F32),
                        pltpu.VMEM(past_buf, k_past.dtype), pltpu.VMEM(past_buf, v_past.dtype),
                        pltpu.SemaphoreType.DMA((2, 2)), pltpu.SMEM((heads * (bq // subn),), jnp.int32)],
        compiler_params=_cparams(("parallel", "parallel", "arbitrary")),
        name="stickbreak_attn",
    )(q, k_new, v_new, k_past, v_past, _tril_strict(subn), _tril_strict(subp), gain)


def _hgrn_halvings(c):
    return [c >> (i + 1) for i in range(int(np.log2(c)))]


def _hgrn_masks(tl):
    t = np.arange(tl)
    masks = [np.eye(tl, dtype=bool)]
    for h in _hgrn_halvings(tl):
        blk = t // (2 * h)
        second = (t // h) % 2 == 1
        masks.append((blk[:, None] == blk[None, :]) & second[:, None] & (~second)[None, :])
    return jnp.asarray(np.stack(masks).astype(np.float32))


def _boundary_rows(b, h, row):
    c, dk = b.shape
    if 2 * h >= 8:
        n = c // (2 * h)
        ref = b.reshape(n, 2 * h, dk)[:, h - 1:h, :]
        return jnp.broadcast_to(ref, (n, 2 * h, dk)).reshape(c, dk)
    down1 = pltpu.roll(b, 1, 0)
    if h == 1:
        return jnp.where((row & 1) == 1, down1, b)
    m = row & 3
    up1 = pltpu.roll(b, c - 1, 0)
    down2 = pltpu.roll(b, 2, 0)
    return jnp.where(m == 0, up1, jnp.where(m == 1, b, jnp.where(m == 2, down1, down2)))


def _hgrn_kernel(q_ref, k_ref, lf_ref, v_ref, g_ref, s0_ref, tri_ref, msk_ref, gain_ref,
                 y_ref, sout_ref, st_ref, *, c, n_chunks):
    t = pl.program_id(2)
    dk = q_ref.shape[-1]

    @pl.when(t == 0)
    def _():
        st_ref[...] = s0_ref[...].T

    row = lax.broadcasted_iota(jnp.int32, (c, dk), 0)
    tri = tri_ref[...]
    tl = msk_ref.shape[-1]
    for ci in range(n_chunks):
        sl = slice(ci * c, (ci + 1) * c)
        q = q_ref[sl, :]
        k = k_ref[sl, :]
        lf = lf_ref[sl, :]
        v = v_ref[sl, :].astype(BF16)
        g = g_ref[sl, :]
        hi = lf.astype(BF16)
        r1 = lf - hi.astype(F32)
        mid = r1.astype(BF16)
        lo = (r1 - mid.astype(F32)).astype(BF16)
        b = _dot(tri, hi) + _dot(tri, mid) + _dot(tri, lo)
        st = st_ref[...]
        o = _dot_nt((q * jnp.exp(b)).astype(BF16), st.astype(BF16))
        tiles = [slice(ti * tl, (ti + 1) * tl) for ti in range(c // tl)]
        qb, kb = q.astype(BF16), k.astype(BF16)
        a_diag = [msk_ref[0] * _dot_nt(qb[rs], kb[rs]) for rs in tiles]
        a_cross = None
        small = 0
        for h in _hgrn_halvings(c):
            gap = b - _boundary_rows(b, h, row)
            el = jnp.exp(jnp.where((row & h) != 0, gap, -gap))
            qe, ke = (q * el).astype(BF16), (k * el).astype(BF16)
            if h >= tl:
                a_cross = _dot_nt(qe[tiles[1]], ke[tiles[0]])
            else:
                small += 1
                for ti, rs in enumerate(tiles):
                    a_diag[ti] += msk_ref[small] * _dot_nt(qe[rs], ke[rs])
        b_last = b[c - 1:c, :]
        k_end = (k * jnp.exp(b_last - b)).astype(BF16)
        st_ref[...] = st * jnp.exp(b_last) + _dot_tn(v, k_end)
        for ti, rs in enumerate(tiles):
            o_t = o[rs] + _dot(a_diag[ti].astype(BF16), v[rs])
            if ti == 1:
                o_t += _dot(a_cross.astype(BF16), v[tiles[0]])
            y = (_rms(o_t) * gain_ref[...]) * (g[rs] * _sigmoid(g[rs]))
            y_ref[ci * c + ti * tl:ci * c + (ti + 1) * tl, :] = y.astype(y_ref.dtype)

    @pl.when(t == pl.num_programs(2) - 1)
    def _():
        sout_ref[...] = st_ref[...].T


def _hgrn(q, k, lf, v, g, s0, gain, *, dk):
    b, t, hw = q.shape
    nh = hw // dk
    c = _tile(t, 2 * LANES)
    tt = _tile(t, 512)
    msk = _hgrn_masks(min(c, LANES))
    tri = jnp.asarray(np.tril(np.ones((c, c), np.float32)), BF16)
    tok = lambda bb, h, ti: (bb, ti, h)
    smap = lambda bb, h, ti: (bb, h, 0, 0)
    return pl.pallas_call(
        functools.partial(_hgrn_kernel, c=c, n_chunks=tt // c),
        grid=(b, nh, t // tt),
        in_specs=[pl.BlockSpec((None, tt, dk), tok)] * 5
        + [pl.BlockSpec((None, None, dk, dk), smap),
           pl.BlockSpec((c, c), lambda bb, h, ti: (0, 0)),
           pl.BlockSpec(msk.shape, lambda bb, h, ti: (0, 0, 0)),
           pl.BlockSpec((1, dk), lambda bb, h, ti: (0, 0))],
        out_specs=[pl.BlockSpec((None, tt, dk), tok), pl.BlockSpec((None, None, dk, dk), smap)],
        out_shape=[jax.ShapeDtypeStruct((b, t, hw), BF16), jax.ShapeDtypeStruct((b, nh, dk, dk), F32)],
        scratch_shapes=[pltpu.VMEM((dk, dk), F32)],
        compiler_params=_cparams(("parallel", "parallel", "arbitrary")),
        name="hgrn2",
    )(q, k, lf, v, g, s0, tri, msk, gain)


def _mixed_residual(x_ref, ya_ref, yb_ref, wa_ref, wb_ref):
    return x_ref[...] + _dot(ya_ref[...], wa_ref[...]) + _dot(yb_ref[...], wb_ref[...])


def _mix_specs(mix, tm, index_row, index_const):
    ya, yb, wa, wb = mix
    return [pl.BlockSpec((tm, ya.shape[1]), index_row), pl.BlockSpec((tm, yb.shape[1]), index_row),
            pl.BlockSpec(wa.shape, index_const), pl.BlockSpec(wb.shape, index_const)]


FF_SUB = 256
FF_ROWS = 512


def _swiglu(x, g_ref, wg_ref, wu_ref, wd_ref, acc_ref):
    h = (_rms(x) * g_ref[...]).astype(BF16)
    ff = wg_ref.shape[-1]
    for c0 in range(0, ff, FF_SUB):
        c1 = min(c0 + FF_SUB, ff)
        gate = _dot(h, wg_ref[:, c0:c1])
        up = _dot(h, wu_ref[:, c0:c1])
        part = _dot(((gate * _sigmoid(gate)) * up).astype(BF16), wd_ref[c0:c1, :])
        if c0 == 0:
            acc_ref[...] = part
        else:
            acc_ref[...] += part


def _ffn_kernel(x_ref, ya_ref, yb_ref, wa_ref, wb_ref, g_ref, wg_ref, wu_ref, wd_ref, o_ref, acc_ref):
    x = _mixed_residual(x_ref, ya_ref, yb_ref, wa_ref, wb_ref)
    _swiglu(x, g_ref, wg_ref, wu_ref, wd_ref, acc_ref)
    o_ref[...] = x + acc_ref[...]


def _ffn(x, mix, g, wg, wu, wd):
    n, d = x.shape
    tm = _tile(n, FF_ROWS)
    row = lambda i: (i, 0)
    const = lambda i: (0, 0)
    return pl.pallas_call(
        _ffn_kernel,
        grid=(n // tm,),
        in_specs=[pl.BlockSpec((tm, d), row)] + _mix_specs(mix, tm, row, const)
        + [pl.BlockSpec((1, d), const),
           pl.BlockSpec(wg.shape, const), pl.BlockSpec(wu.shape, const), pl.BlockSpec(wd.shape, const)],
        out_specs=pl.BlockSpec((tm, d), row),
        out_shape=jax.ShapeDtypeStruct((n, d), F32),
        scratch_shapes=[pltpu.VMEM((tm, d), F32)],
        compiler_params=_cparams(("parallel",)),
        name="ffn_dense",
    )(x, *mix, g, wg, wu, wd)


ROUTE_E1, ROUTE_E2, ROUTE_R1, ROUTE_R2, ROUTE_G1, ROUTE_G2 = range(6)


def _router_kernel(x_ref, ya_ref, yb_ref, wa_ref, wb_ref, g_ref, rhi_ref, tri_ref,
                   x1_ref, route_ref, cnt_ref, base_ref, *, n_experts):
    i = pl.program_id(0)
    lane = lax.broadcasted_iota(jnp.int32, (1, LANES), 1)

    @pl.when(i == 0)
    def _():
        base_ref[...] = jnp.zeros_like(base_ref)

    x1 = _mixed_residual(x_ref, ya_ref, yb_ref, wa_ref, wb_ref)
    x1_ref[...] = x1
    hf = _rms(x1) * g_ref[...]
    h_hi = hf.astype(BF16)
    h_lo = (hf - h_hi.astype(F32)).astype(BF16)
    hi_parts = _dot(h_hi, rhi_ref[...])
    logits = hi_parts[:, :LANES] + _dot(h_lo, rhi_ref[:, :LANES]) + hi_parts[:, LANES:]
    logits = jnp.where(lane < n_experts, logits, -jnp.inf)
    m1 = jnp.max(logits, axis=-1, keepdims=True)
    i1 = jnp.min(jnp.where(logits == m1, lane, LANES), axis=-1, keepdims=True)
    rest = jnp.where(lane == i1, -jnp.inf, logits)
    m2 = jnp.max(rest, axis=-1, keepdims=True)
    i2 = jnp.min(jnp.where(rest == m2, lane, LANES), axis=-1, keepdims=True)
    e2 = jnp.exp(m2 - m1)
    g1 = 1.0 / (1.0 + e2)
    g2 = e2 / (1.0 + e2)
    oh1 = (lane == i1).astype(F32)
    oh2 = (lane == i2).astype(F32)
    both = oh1 + oh2
    sub = tri_ref.shape[0]
    offset = base_ref[...]
    pieces = []
    for r0 in range(0, both.shape[0], sub):
        blk = both[r0:r0 + sub]
        pieces.append(_dot(tri_ref[...], blk.astype(BF16)) + offset)
        offset = offset + jnp.sum(blk, axis=0, keepdims=True)
    before = jnp.concatenate(pieces, axis=0)
    r1 = jnp.sum(oh1 * before, axis=-1, keepdims=True)
    r2 = jnp.sum(oh2 * before, axis=-1, keepdims=True)
    rec = jnp.zeros(route_ref.shape, F32)
    for slot, val in ((ROUTE_E1, i1.astype(F32)), (ROUTE_E2, i2.astype(F32)), (ROUTE_R1, r1),
                      (ROUTE_R2, r2), (ROUTE_G1, g1), (ROUTE_G2, g2)):
        rec = jnp.where(lane == slot, val, rec)
    route_ref[...] = rec
    base_ref[...] = offset
    cnt_ref[...] = offset


def _router(x, mix, g, router):
    n, d = x.shape
    ne = router.shape[1]
    tm = _tile(n, 1024)
    rpad = jnp.zeros((d, LANES), F32).at[:, :ne].set(router)
    rank_sub = _tile(tm, KEY_SUB)
    rhi = rpad.astype(BF16)
    rhi = jnp.concatenate([rhi, (rpad - rhi.astype(F32)).astype(BF16)], axis=1)
    row = lambda i: (i, 0)
    const = lambda i: (0, 0)
    return pl.pallas_call(
        functools.partial(_router_kernel, n_experts=ne),
        grid=(n // tm,),
        in_specs=[pl.BlockSpec((tm, d), row)] + _mix_specs(mix, tm, row, const)
        + [pl.BlockSpec((1, d), const),
           pl.BlockSpec((d, 2 * LANES), const), pl.BlockSpec((rank_sub, rank_sub), const)],
        out_specs=[pl.BlockSpec((tm, d), row), pl.BlockSpec((tm, LANES), row), pl.BlockSpec((1, LANES), const)],
        out_shape=[jax.ShapeDtypeStruct((n, d), F32), jax.ShapeDtypeStruct((n, LANES), F32),
                   jax.ShapeDtypeStruct((1, LANES), F32)],
        scratch_shapes=[pltpu.VMEM((1, LANES), F32)],
        compiler_params=_cparams(("arbitrary",)),
        name="moe_router",
    )(x, *mix, g, rhi, _tril_strict(rank_sub))


def _row_copy(src_ref, src_row, dst_ref, dst_row, sem):
    return pltpu.make_async_copy(src_ref.at[pl.ds(src_row, 1), :], dst_ref.at[pl.ds(dst_row, 1), :], sem)


def _dispatch_kernel(pad_lo_ref, pad_hi_ref, p1_ref, p2_ref, x_ref, xs_ref, sem, pad_sem, *, tb, n_pad_ranges):
    def issue(t, carry):
        _row_copy(x_ref, t, xs_ref, p1_ref[0, 0, t], sem).start(priority=0)
        _row_copy(x_ref, t, xs_ref, p2_ref[0, 0, t], sem).start(priority=1)
        return carry

    lax.fori_loop(0, tb, issue, 0, unroll=8)

    @pl.when(pl.program_id(0) == 0)
    def _():
        def fill(r, carry):
            _row_copy(x_ref, 0, xs_ref, r, pad_sem).start()
            return carry

        def drain(r, carry):
            _row_copy(x_ref, 0, xs_ref, r, pad_sem).wait()
            return carry

        for e in range(n_pad_ranges):
            lax.fori_loop(pad_lo_ref[e], pad_hi_ref[e], fill, 0)
        for e in range(n_pad_ranges):
            lax.fori_loop(pad_lo_ref[e], pad_hi_ref[e], drain, 0)

    for _ in range(TOP_K):
        pltpu.make_async_copy(x_ref, xs_ref.at[pl.ds(0, tb), :], sem).wait()


def _dispatch(x, pos1, pos2, n_rows, pad_lo, pad_hi):
    n, d = x.shape
    tb = pos1.shape[-1]
    smem = lambda: pl.BlockSpec((1, 1, tb), lambda i, lo, hi: (i, 0, 0), memory_space=pltpu.SMEM)
    grid_spec = pltpu.PrefetchScalarGridSpec(
        num_scalar_prefetch=2,
        grid=(n // tb,),
        in_specs=[smem(), smem(), pl.BlockSpec((tb, d), lambda i, lo, hi: (i, 0))],
        out_specs=pl.BlockSpec(memory_space=pl.ANY),
        scratch_shapes=[pltpu.SemaphoreType.DMA(()), pltpu.SemaphoreType.DMA(())],
    )
    return pl.pallas_call(
        functools.partial(_dispatch_kernel, tb=tb, n_pad_ranges=pad_lo.shape[0]),
        grid_spec=grid_spec,
        out_shape=jax.ShapeDtypeStruct((n_rows, d), F32),
        compiler_params=_cparams(("arbitrary",)),
        name="moe_dispatch",
    )(pad_lo, pad_hi, pos1, pos2, x)


def _experts_kernel(te_ref, nv_ref, x_ref, g_ref, wg_ref, wu_ref, wd_ref, o_ref, acc_ref):
    del te_ref
    valid = pl.program_id(0) < nv_ref[0]

    @pl.when(valid)
    def _():
        _swiglu(x_ref[...], g_ref, wg_ref, wu_ref, wd_ref, acc_ref)
        o_ref[...] = acc_ref[...]

    @pl.when(jnp.logical_not(valid))
    def _():
        o_ref[...] = jnp.zeros_like(o_ref)


def _experts(xs, g, wg, wu, wd, tile_expert, n_valid, tr):
    n_rows, d = xs.shape
    ff = wg.shape[2]
    rowmap = lambda i, te, nv: (jnp.minimum(i, nv[0] - 1), 0)
    wmap = lambda i, te, nv: (te[i], 0, 0)
    grid_spec = pltpu.PrefetchScalarGridSpec(
        num_scalar_prefetch=2,
        grid=(n_rows // tr,),
        in_specs=[pl.BlockSpec((tr, d), rowmap), pl.BlockSpec((1, d), lambda i, te, nv: (0, 0)),
                  pl.BlockSpec((None, d, ff), wmap), pl.BlockSpec((None, d, ff), wmap),
                  pl.BlockSpec((None, ff, d), wmap)],
        out_specs=pl.BlockSpec((tr, d), lambda i, te, nv: (i, 0)),
        scratch_shapes=[pltpu.VMEM((tr, d), F32)],
    )
    return pl.pallas_call(
        _experts_kernel,
        grid_spec=grid_spec,
        out_shape=jax.ShapeDtypeStruct((n_rows, d), F32),
        compiler_params=_cparams(("arbitrary",)),
        name="moe_experts",
    )(tile_expert, n_valid, xs, g, wg, wu, wd)


def _combine_kernel(p1_ref, p2_ref, x_ref, route_ref, ys_ref, o_ref, buf_ref, sem, *, tc):
    def issue(t, carry):
        _row_copy(ys_ref, p1_ref[0, 0, t], buf_ref.at[0], t, sem).start(priority=0)
        _row_copy(ys_ref, p2_ref[0, 0, t], buf_ref.at[1], t, sem).start(priority=1)
        return carry

    lax.fori_loop(0, tc, issue, 0, unroll=8)
    for slot in range(TOP_K):
        pltpu.make_async_copy(ys_ref.at[pl.ds(0, tc), :], buf_ref.at[slot], sem).wait()
    lane = lax.broadcasted_iota(jnp.int32, (1, LANES), 1)
    route = route_ref[...]
    g1 = jnp.sum(jnp.where(lane == ROUTE_G1, route, 0.0), axis=-1, keepdims=True)
    g2 = jnp.sum(jnp.where(lane == ROUTE_G2, route, 0.0), axis=-1, keepdims=True)
    o_ref[...] = x_ref[...] + (g1 * buf_ref[0] + g2 * buf_ref[1])


def _combine(x, route, ys, pos1, pos2):
    n, d = x.shape
    tc = pos1.shape[-1]
    smem = lambda: pl.BlockSpec((1, 1, tc), lambda i: (i, 0, 0), memory_space=pltpu.SMEM)
    return pl.pallas_call(
        functools.partial(_combine_kernel, tc=tc),
        grid=(n // tc,),
        in_specs=[smem(), smem(), pl.BlockSpec((tc, d), lambda i: (i, 0)),
                  pl.BlockSpec((tc, LANES), lambda i: (i, 0)), pl.BlockSpec(memory_space=pl.ANY)],
        out_specs=pl.BlockSpec((tc, d), lambda i: (i, 0)),
        out_shape=jax.ShapeDtypeStruct((n, d), F32),
        scratch_shapes=[pltpu.VMEM((TOP_K, tc, d), F32), pltpu.SemaphoreType.DMA(())],
        compiler_params=_cparams(("arbitrary",)),
        name="moe_combine",
    )(pos1, pos2, x, route, ys)


def _moe(x, mix, g, router, wg, wu, wd):
    n, d = x.shape
    ne = wg.shape[0]
    tr = FF_ROWS if n >= 8192 else 256
    tb = _tile(n, 512)
    x, route, cnt = _router(x, mix, g, router)
    counts = cnt[0, :ne].astype(jnp.int32)
    padded = ((counts + tr - 1) // tr) * tr
    ends = jnp.cumsum(padded)
    starts = ends - padded
    e1 = route[:, ROUTE_E1].astype(jnp.int32)
    e2 = route[:, ROUTE_E2].astype(jnp.int32)
    pos1 = (starts[e1] + route[:, ROUTE_R1].astype(jnp.int32)).reshape(n // tb, 1, tb)
    pos2 = (starts[e2] + route[:, ROUTE_R2].astype(jnp.int32)).reshape(n // tb, 1, tb)
    max_tiles = -(-(TOP_K * n + ne * (tr - 1)) // tr)
    n_valid = (ends[-1] // tr).reshape(1)
    tile_start = jnp.minimum(jnp.arange(max_tiles, dtype=jnp.int32), n_valid[0] - 1) * tr
    tile_expert = jnp.minimum(jnp.searchsorted(ends, tile_start, side="right"), ne - 1).astype(jnp.int32)
    n_rows = max_tiles * tr
    pad_lo = jnp.concatenate([starts + counts, ends[-1:]]).astype(jnp.int32)
    pad_hi = jnp.concatenate([ends, jnp.full((1,), n_rows, ends.dtype)]).astype(jnp.int32)
    xs = _dispatch(x, pos1, pos2, n_rows, pad_lo, pad_hi)
    ys = _experts(xs, g, wg, wu, wd, tile_expert, n_valid.astype(jnp.int32), tr)
    return _combine(x, route, ys, pos1, pos2)


def kernel(x_prompt, x_sample, cache_k, cache_v, state_hgrn, norm_mix, w_in, sb_q_gain, sb_k_gain,
           hg_lower_bounds, sb_out_gain, hg_out_gain, w_out, norm_ffn, ffn_w_gate, ffn_w_up, ffn_w_down,
           moe_router, moe_w_gate, moe_w_up, moe_w_down):
    depth = w_in.shape[0]
    bp, tp, d = x_prompt.shape
    bs, ts, _ = x_sample.shape
    past = cache_k.shape[2]
    sb_heads, hd = cache_k.shape[3], cache_k.shape[4]
    sw = sb_heads * hd
    dk = hg_out_gain.shape[1]
    hw = hg_lower_bounds.shape[1]
    hg_heads = hw // dk
    sb_scale = hd ** -0.5

    lbs = jnp.cumsum(jax.nn.softmax(hg_lower_bounds.astype(F32), axis=0), axis=0)
    lbs = lbs - lbs[0:1]

    xp = x_prompt.reshape(bp * tp, d)
    xs = x_sample.reshape(bs * ts, d)
    zeros_state = jnp.zeros((bp, hg_heads, dk, dk), F32)
    outs = {k: [] for k in ("sp", "ss")}
    cache_kt = jnp.transpose(cache_k, (0, 1, 3, 4, 2)).reshape(depth, bs, sw, past)
    cache_vt = jnp.transpose(cache_v, (0, 1, 3, 4, 2)).reshape(depth, bs, sw, past)

    def mixer(x, b, t, l, kv_prev, *, cached, past_len, s0, bq):
        qg = (jnp.tile(sb_q_gain[l], sb_heads) * sb_scale)[None, :]
        kg = jnp.tile(sb_k_gain[l], sb_heads)[None, :]
        q, kf, kb, vf, vb, qh, kh, lf, ih, gh = _inproj(
            x, norm_mix[l][None, :], w_in[l].astype(BF16), qg, kg,
            jnp.log(lbs[l])[None, :], jnp.log1p(-lbs[l])[None, :], (1.0 - lbs[l])[None, :], kv_prev,
            sw=sw, hw=hw, hd=hd, seq_len=t)
        r3 = lambda a: a.reshape(b, t, a.shape[-1])
        kb3, vb3 = r3(kb), r3(vb)
        ya = _attention(r3(q), kb3, vb3, cache_kt if cached else kb3, cache_vt if cached else vb3,
                        jnp.tile(sb_out_gain[l], LANES // hd)[None, :],
                        bq=bq, past_len=past_len, hd=hd, past_layer=l if cached else None)
        yb, s_fin = _hgrn(r3(qh), r3(kh), r3(lf), r3(ih), r3(gh), s0, hg_out_gain[l][None, :], dk=dk)
        wo = w_out[l].astype(BF16)
        return (ya.reshape(b * t, sw), yb.reshape(b * t, hw), wo[:sw], wo[sw:]), (kf, vf), s_fin

    def channel(x, mix, l):
        j = l // 2
        if l % 2 == 0:
            return _ffn(x, mix, norm_ffn[l][None, :], ffn_w_gate[j].astype(BF16), ffn_w_up[j].astype(BF16),
                        ffn_w_down[j].astype(BF16))
        return _moe(x, mix, norm_ffn[l][None, :], moe_router[j], moe_w_gate[j].astype(BF16),
                    moe_w_up[j].astype(BF16), moe_w_down[j].astype(BF16))

    bq_p = _tile(tp, 512)
    kv_p = kv_s = None
    for l in range(depth):
        mix_p, kv_p, s_p = mixer(xp, bp, tp, l, kv_p, cached=False, past_len=None, s0=zeros_state, bq=bq_p)
        outs["sp"].append(s_p)
        mix_s, kv_s, s_s = mixer(xs, bs, ts, l, kv_s, cached=True, past_len=past,
                                 s0=state_hgrn[l].astype(F32), bq=ts)
        outs["ss"].append(s_s)
        xp = channel(xp, mix_p, l)
        xs = channel(xs, mix_s, l)

    heads5 = lambda a, b, t: jnp.transpose(a.reshape(depth, b, sb_heads, hd, t), (0, 1, 4, 2, 3))
    return (xp.reshape(bp, tp, d), xs.reshape(bs, ts, d), heads5(kv_p[0], bp, tp), heads5(kv_p[1], bp, tp),
            jnp.stack(outs["sp"]), heads5(kv_s[0], bs, ts), heads5(kv_s[1], bs, ts), jnp.stack(outs["ss"]))
```

```python
import functools

import numpy as np
import jax
import jax.numpy as jnp
from jax import lax
from jax.experimental import pallas as pl
from jax.experimental.pallas import tpu as pltpu

EPS = 1e-6
LOG2_E = 1.4426950408889634
TOP_K = 2
LANES = 128
SUBLANES = 8
F32 = jnp.float32
BF16 = jnp.bfloat16
VMEM_LIMIT_BYTES = 56 * 1024 * 1024


def _cparams(semantics):
    return pltpu.CompilerParams(dimension_semantics=semantics, vmem_limit_bytes=VMEM_LIMIT_BYTES)


def _dot(a, b):
    return jnp.dot(a, b, preferred_element_type=F32)


def _dot_nt(a, b):
    return lax.dot_general(a, b, (((1,), (1,)), ((), ())), preferred_element_type=F32)


def _dot_tn(a, b):
    return lax.dot_general(a, b, (((0,), (0,)), ((), ())), preferred_element_type=F32)


def _sigmoid(x):
    return 1.0 / (1.0 + jnp.exp(-x))


def _rms(x):
    return x * lax.rsqrt(jnp.mean(x * x, axis=-1, keepdims=True) + EPS)


def _tile(n, pref):
    t = min(n, pref)
    assert n % t == 0, (n, t)
    return t


def _inproj_kernel(*refs, sw, hw, hd, n_prev, seq_per_tile):
    x_ref, g_ref, w_ref, qg_ref, kg_ref, llb_ref, l1m_ref, oml_ref = refs[:8]
    prev = refs[8:8 + (2 if n_prev else 0)]
    q_ref, kf_ref, kb_ref, vf_ref, vb_ref, qh_ref, kh_ref, lf_ref, ih_ref, gh_ref = refs[8 + len(prev):]
    if n_prev:
        kf_ref[:n_prev] = prev[0][...]
        vf_ref[:n_prev] = prev[1][...]

    def store_time_minor(ref, y):
        t = y.shape[0] // seq_per_tile
        for sq in range(seq_per_tile):
            ref[n_prev, sq] = y[sq * t:(sq + 1) * t, :].T

    h = (_rms(x_ref[...]) * g_ref[...]).astype(BF16)

    def seg(lo, width):
        return _dot(h, w_ref[:, lo:lo + width])

    def headnorm(y, gain):
        lane = lax.broadcasted_iota(jnp.int32, (1, LANES), 1)
        tiles = []
        for c0 in range(0, y.shape[1], LANES):
            y2 = y[:, c0:c0 + LANES] * y[:, c0:c0 + LANES]
            m = jnp.zeros_like(y2)
            for h0 in range(0, LANES, hd):
                in_head = (lane >= h0) & (lane < h0 + hd)
                m = jnp.where(in_head, jnp.sum(jnp.where(in_head, y2, 0.0), axis=-1, keepdims=True), m)
            tiles.append(m * (1.0 / hd))
        m = jnp.concatenate(tiles, axis=1)
        return (y * lax.rsqrt(m + EPS)) * gain

    q_ref[...] = headnorm(seg(0, sw), qg_ref[...]).astype(BF16)
    ka = headnorm(seg(sw, sw), kg_ref[...])
    store_time_minor(kf_ref, ka)
    kb_ref[...] = ka.astype(BF16)
    va = seg(2 * sw, sw)
    store_time_minor(vf_ref, va)
    vb_ref[...] = va.astype(BF16)

    o = 3 * sw
    qb = seg(o, hw)
    qh_ref[...] = qb * _sigmoid(qb)

    fb = seg(o + hw, hw)
    e = jnp.exp(-jnp.abs(fb))
    log_sig = jnp.minimum(fb, 0.0) - jnp.log(1.0 + e)
    c = l1m_ref[...] + log_sig
    a = llb_ref[...]
    lf_ref[...] = jnp.maximum(a, c) + jnp.log(1.0 + jnp.exp(-jnp.abs(a - c)))
    kh_ref[...] = oml_ref[...] * (jnp.where(fb >= 0.0, e, 1.0) / (1.0 + e))

    ih_ref[...] = seg(o + 2 * hw, hw)
    gh_ref[...] = seg(o + 3 * hw, hw)


def _inproj(x, g, w, qg, kg, llb, l1m, oml, kv_prev, *, sw, hw, hd, seq_len):
    n, d = x.shape
    tm = _tile(n, 512)
    n_prev = 0 if kv_prev is None else kv_prev[0].shape[0]
    spt = max(tm // seq_len, 1)
    tt = tm // spt
    nt = seq_len // tt
    assert spt * tt == tm and nt * tt == seq_len
    row = lambda i: (i, 0)
    const = lambda i: (0, 0)
    stacked = lambda i: (0, i // nt, 0, i % nt)
    kv_shape = (n_prev + 1, n // seq_len, sw, seq_len)
    specs = {"row_sw": pl.BlockSpec((tm, sw), row), "row_hw": pl.BlockSpec((tm, hw), row),
             "stack": pl.BlockSpec((n_prev + 1, spt, sw, tt), stacked)}
    outs = [("row_sw", (n, sw), BF16), ("stack", kv_shape, F32), ("row_sw", (n, sw), BF16),
            ("stack", kv_shape, F32), ("row_sw", (n, sw), BF16)] + [("row_hw", (n, hw), F32)] * 5
    prev_specs = [pl.BlockSpec((n_prev, spt, sw, tt), stacked)] * 2 if n_prev else []
    return pl.pallas_call(
        functools.partial(_inproj_kernel, sw=sw, hw=hw, hd=hd, n_prev=n_prev, seq_per_tile=spt),
        grid=(n // tm,),
        in_specs=[pl.BlockSpec((tm, d), row), pl.BlockSpec((1, d), const),
                  pl.BlockSpec(w.shape, const),
                  pl.BlockSpec((1, sw), const), pl.BlockSpec((1, sw), const),
                  pl.BlockSpec((1, hw), const), pl.BlockSpec((1, hw), const), pl.BlockSpec((1, hw), const)]
        + prev_specs,
        out_specs=[specs[kind] for kind, _, _ in outs],
        out_shape=[jax.ShapeDtypeStruct(shape, dt) for _, shape, dt in outs],
        compiler_params=_cparams(("parallel",)),
        name="inproj",
    )(x, g, w, qg, kg, llb, l1m, oml, *(kv_prev or ()))


KEY_SUB = 256
SOFTPLUS_CLAMP = 80.0
CARRY_DEAD = 104.0


def _attn_kernel(q_ref, kn_ref, vn_ref, kp_hbm, vp_hbm, un_ref, up_ref, gain_ref,
                 y_ref, qm_ref, acc_ref, carry_ref, kbuf, vbuf, sem, alive_ref,
                 *, bq, subn, subp, hd, past_len, past_layer, past_transposed):
    bb = pl.program_id(0)
    p = pl.program_id(1)
    qi = pl.program_id(2)
    lane = lax.broadcasted_iota(jnp.int32, (1, LANES), 1)
    heads = LANES // hd
    n_past = (qi * bq if past_len is None else past_len) // subp

    def fetch(j, slot):
        if past_transposed:
            window = (past_layer, bb, pl.ds(p * LANES, LANES), pl.ds(j * subp, subp))
        else:
            window = (bb, pl.ds(j * subp, subp), pl.ds(p * LANES, LANES))
        return (pltpu.make_async_copy(kp_hbm.at[window], kbuf.at[slot], sem.at[0, slot]),
                pltpu.make_async_copy(vp_hbm.at[window], vbuf.at[slot], sem.at[1, slot]))

    def slot_of(j):
        return (n_past - 1 - j) & 1

    @pl.when(n_past > 0)
    def _():
        for cp in fetch(n_past - 1, 0):
            cp.start()

    q = q_ref[...]
    for hh in range(heads):
        in_head = (lane >= hh * hd) & (lane < (hh + 1) * hd)
        qm_ref[hh] = jnp.where(in_head, q, jnp.zeros_like(q))
    acc_ref[...] = jnp.zeros_like(acc_ref)
    carry_ref[...] = jnp.zeros_like(carry_ref)

    groups = bq // subn

    def one_head(hh, k, v, u, r0, r1, mask, transposed):
        if transposed:
            z = _dot(qm_ref[hh, r0:r1, :], k)
        else:
            z = _dot_nt(qm_ref[hh, r0:r1, :], k)
        sp = jnp.maximum(jnp.log(1.0 + jnp.exp(jnp.minimum(z, SOFTPLUS_CLAMP))), z)
        if mask is not None:
            sp = jnp.where(mask, sp, 0.0)
        after = _dot(sp.astype(BF16), u)
        w = jnp.exp((z - sp) - after)
        if mask is not None:
            w = jnp.where(mask, w, 0.0)
        pv = _dot_nt(w.astype(BF16), v) if transposed else _dot(w.astype(BF16), v)
        c = carry_ref[hh, r0:r1, :]
        acc_ref[hh, r0:r1, :] += jnp.exp(-c) * pv
        c_new = c + jnp.sum(sp, axis=-1, keepdims=True)
        carry_ref[hh, r0:r1, :] = c_new
        return c_new

    def set_alive(hh, g, c_group):
        alive_ref[hh * groups + g] = (jnp.min(c_group) < CARRY_DEAD).astype(jnp.int32)

    def diagonal_block(flag_groups):
        u_new = un_ref[...]
        for j in range(groups - 1, -1, -1):
            r0 = j * subn
            rows = r0 + lax.broadcasted_iota(jnp.int32, (bq - r0, subn), 0)
            cols = r0 + lax.broadcasted_iota(jnp.int32, (bq - r0, subn), 1)
            k = kn_ref[r0:r0 + subn, :].astype(BF16)
            v = vn_ref[r0:r0 + subn, :].astype(BF16)
            for hh in range(heads):
                c_new = one_head(hh, k, v, u_new, r0, bq, cols < rows, False)
                if j == 0:
                    for g in flag_groups:
                        set_alive(hh, g, c_new[g * subn:(g + 1) * subn])

    def group_heads(slot, g, head_list):
        k = kbuf[slot].astype(BF16)
        v = vbuf[slot].astype(BF16)
        u_past = up_ref[...]
        for hh in head_list:
            c_new = one_head(hh, k, v, u_past, g * subn, (g + 1) * subn, None, past_transposed)
            set_alive(hh, g, c_new)

    def alive_groups(slot, group_list):
        for g in group_list:
            n_alive = alive_ref[g]
            for hh in range(1, heads):
                n_alive += alive_ref[hh * groups + g]
            pl.when(n_alive == heads)(functools.partial(group_heads, slot, g, range(heads)))
            for hh in range(heads):
                pl.when((n_alive < heads) & (alive_ref[hh * groups + g] == 1))(
                    functools.partial(group_heads, slot, g, (hh,)))

    @pl.when(n_past == 0)
    def _():
        diagonal_block(range(groups))

    @pl.when(n_past > 0)
    def _():
        diagonal_block(range(1, groups))
        for cp in fetch(n_past - 1, 0):
            cp.wait()
        for cp in fetch(jnp.maximum(n_past - 2, 0), 1):
            cp.start()
        group_heads(0, 0, range(heads))
        alive_groups(0, range(1, groups))

    def any_alive():
        total = alive_ref[0]
        for i in range(1, heads * groups):
            total += alive_ref[i]
        return total > 0

    def past_block(j):
        slot = slot_of(j)
        for cp in fetch(j, slot):
            cp.wait()

        @pl.when(j > 0)
        def _():
            for cp in fetch(j - 1, 1 - slot):
                cp.start()

        alive_groups(slot, range(groups))
        return j - 1

    j_end = lax.while_loop(lambda j: (j >= 0) & any_alive(), past_block, n_past - 2)

    @pl.when(j_end >= 0)
    def _():
        for cp in fetch(j_end, slot_of(j_end)):
            cp.wait()

    @pl.when(n_past == 1)
    def _():
        for cp in fetch(0, 1):
            cp.wait()

    o = acc_ref[0]
    for hh in range(1, heads):
        in_head = (lane >= hh * hd) & (lane < (hh + 1) * hd)
        o = jnp.where(in_head, acc_ref[hh], o)
    o2 = o * o
    ms = jnp.zeros_like(o)
    for hh in range(heads):
        in_head = (lane >= hh * hd) & (lane < (hh + 1) * hd)
        ssum = jnp.sum(jnp.where(in_head, o2, 0.0), axis=-1, keepdims=True)
        ms = jnp.where(in_head, ssum * (1.0 / hd), ms)
    y_ref[...] = ((o * lax.rsqrt(ms + EPS)) * gain_ref[...]).astype(y_ref.dtype)


def _tril_strict(n):
    return jnp.asarray(np.tril(np.ones((n, n), np.float32), -1), BF16)


def _attention(q, k_new, v_new, k_past, v_past, gain, *, bq, past_len, hd, past_layer=None):
    b, tq, w = q.shape
    past_transposed = past_layer is not None
    subn = min(bq, KEY_SUB)
    subp = KEY_SUB
    assert bq % subn == 0 and (bq if past_len is None else past_len) % subp == 0
    heads = LANES // hd
    qmap = lambda bb, p, qi: (bb, qi, p)
    cmap = lambda bb, p, qi: (0, 0)
    past_buf = (2, LANES, subp) if past_transposed else (2, subp, LANES)
    return pl.pallas_call(
        functools.partial(_attn_kernel, bq=bq, subn=subn, subp=subp, hd=hd, past_len=past_len,
                          past_layer=past_layer, past_transposed=past_transposed),
        grid=(b, w // LANES, tq // bq),
        in_specs=[pl.BlockSpec((None, bq, LANES), qmap),
                  pl.BlockSpec((None, bq, LANES), qmap), pl.BlockSpec((None, bq, LANES), qmap),
                  pl.BlockSpec(memory_space=pl.ANY), pl.BlockSpec(memory_space=pl.ANY),
                  pl.BlockSpec((subn, subn), cmap), pl.BlockSpec((subp, subp), cmap),
                  pl.BlockSpec((1, LANES), cmap)],
        out_specs=pl.BlockSpec((None, bq, LANES), qmap),
        out_shape=jax.ShapeDtypeStruct((b, tq, w), BF16),
        scratch_shapes=[pltpu.VMEM((heads, bq, LANES), BF16), pltpu.VMEM((heads, bq, LANES), F32),
                        pltpu.VMEM((heads, bq, LANES), F32),
                        pltpu.VMEM(past_buf, k_past.dtype), pltpu.VMEM(past_buf, v_past.dtype),
                        pltpu.SemaphoreType.DMA((2, 2)), pltpu.SMEM((heads * (bq // subn),), jnp.int32)],
        compiler_params=_cparams(("parallel", "parallel", "arbitrary")),
        name="stickbreak_attn",
    )(q, k_new, v_new, k_past, v_past, _tril_strict(subn), _tril_strict(subp), gain)


def _hgrn_halvings(c):
    return [c >> (i + 1) for i in range(int(np.log2(c)))]


def _hgrn_masks(tl):
    t = np.arange(tl)
    masks = [np.eye(tl, dtype=bool)]
    for h in _hgrn_halvings(tl):
        blk = t // (2 * h)
        second = (t // h) % 2 == 1
        masks.append((blk[:, None] == blk[None, :]) & second[:, None] & (~second)[None, :])
    return jnp.asarray(np.stack(masks).astype(np.float32))


def _boundary_rows(b, h, row):
    c, dk = b.shape
    if 2 * h >= 8:
        n = c // (2 * h)
        ref = b.reshape(n, 2 * h, dk)[:, h - 1:h, :]
        return jnp.broadcast_to(ref, (n, 2 * h, dk)).reshape(c, dk)
    down1 = pltpu.roll(b, 1, 0)
    if h == 1:
        return jnp.where((row & 1) == 1, down1, b)
    m = row & 3
    up1 = pltpu.roll(b, c - 1, 0)
    down2 = pltpu.roll(b, 2, 0)
    return jnp.where(m == 0, up1, jnp.where(m == 1, b, jnp.where(m == 2, down1, down2)))


def _hgrn_kernel(q_ref, k_ref, lf_ref, v_ref, g_ref, s0_ref, tri_ref, msk_ref, gain_ref,
                 y_ref, sout_ref, st_ref, *, c, n_chunks):
    t = pl.program_id(2)
    dk = q_ref.shape[-1]

    @pl.when(t == 0)
    def _():
        st_ref[...] = s0_ref[...].T

    row = lax.broadcasted_iota(jnp.int32, (c, dk), 0)
    tri = tri_ref[...]
    tl = msk_ref.shape[-1]
    for ci in range(n_chunks):
        sl = slice(ci * c, (ci + 1) * c)
        q = q_ref[sl, :]
        k = k_ref[sl, :]
        lf = lf_ref[sl, :]
        v = v_ref[sl, :].astype(BF16)
        g = g_ref[sl, :]
        hi = lf.astype(BF16)
        r1 = lf - hi.astype(F32)
        mid = r1.astype(BF16)
        lo = (r1 - mid.astype(F32)).astype(BF16)
        b = _dot(tri, hi) + _dot(tri, mid) + _dot(tri, lo)
        st = st_ref[...]
        b2 = b * LOG2_E
        o = _dot_nt((q * jnp.exp2(b2)).astype(BF16), st.astype(BF16))
        tiles = [slice(ti * tl, (ti + 1) * tl) for ti in range(c // tl)]
        qb, kb = q.astype(BF16), k.astype(BF16)
        a_diag = [msk_ref[0] * _dot_nt(qb[rs], kb[rs]) for rs in tiles]
        a_cross = None
        small = 0
        for h in _hgrn_halvings(c):
            el = jnp.exp2(-jnp.abs(b2 - _boundary_rows(b2, h, row)))
            qe, ke = (q * el).astype(BF16), (k * el).astype(BF16)
            if h >= tl:
                a_cross = _dot_nt(qe[tiles[1]], ke[tiles[0]])
            else:
                small += 1
                for ti, rs in enumerate(tiles):
                    a_diag[ti] += msk_ref[small] * _dot_nt(qe[rs], ke[rs])
        b2_last = b2[c - 1:c, :]
        k_end = (k * jnp.exp2(b2_last - b2)).astype(BF16)
        st_ref[...] = st * jnp.exp2(b2_last) + _dot_tn(v, k_end)
        for ti, rs in enumerate(tiles):
            o_t = o[rs] + _dot(a_diag[ti].astype(BF16), v[rs])
            if ti == 1:
                o_t += _dot(a_cross.astype(BF16), v[tiles[0]])
            y = (_rms(o_t) * gain_ref[...]) * (g[rs] * _sigmoid(g[rs]))
            y_ref[ci * c + ti * tl:ci * c + (ti + 1) * tl, :] = y.astype(y_ref.dtype)

    @pl.when(t == pl.num_programs(2) - 1)
    def _():
        sout_ref[...] = st_ref[...].T


def _hgrn(q, k, lf, v, g, s0, gain, *, dk):
    b, t, hw = q.shape
    nh = hw // dk
    c = _tile(t, 2 * LANES)
    tt = _tile(t, 512)
    msk = _hgrn_masks(min(c, LANES))
    tri = jnp.asarray(np.tril(np.ones((c, c), np.float32)), BF16)
    tok = lambda bb, h, ti: (bb, ti, h)
    smap = lambda bb, h, ti: (bb, h, 0, 0)
    return pl.pallas_call(
        functools.partial(_hgrn_kernel, c=c, n_chunks=tt // c),
        grid=(b, nh, t // tt),
        in_specs=[pl.BlockSpec((None, tt, dk), tok)] * 5
        + [pl.BlockSpec((None, None, dk, dk), smap),
           pl.BlockSpec((c, c), lambda bb, h, ti: (0, 0)),
           pl.BlockSpec(msk.shape, lambda bb, h, ti: (0, 0, 0)),
           pl.BlockSpec((1, dk), lambda bb, h, ti: (0, 0))],
        out_specs=[pl.BlockSpec((None, tt, dk), tok), pl.BlockSpec((None, None, dk, dk), smap)],
        out_shape=[jax.ShapeDtypeStruct((b, t, hw), BF16), jax.ShapeDtypeStruct((b, nh, dk, dk), F32)],
        scratch_shapes=[pltpu.VMEM((dk, dk), F32)],
        compiler_params=_cparams(("parallel", "parallel", "arbitrary")),
        name="hgrn2",
    )(q, k, lf, v, g, s0, tri, msk, gain)


def _mixed_residual(x_ref, ya_ref, yb_ref, wa_ref, wb_ref):
    return x_ref[...] + _dot(ya_ref[...], wa_ref[...]) + _dot(yb_ref[...], wb_ref[...])


def _mix_specs(mix, tm, index_row, index_const):
    ya, yb, wa, wb = mix
    return [pl.BlockSpec((tm, ya.shape[1]), index_row), pl.BlockSpec((tm, yb.shape[1]), index_row),
            pl.BlockSpec(wa.shape, index_const), pl.BlockSpec(wb.shape, index_const)]


FF_SUB = 256
FF_ROWS = 512


def _swiglu(x, g_ref, wg_ref, wu_ref, wd_ref, acc_ref):
    h = (_rms(x) * g_ref[...]).astype(BF16)
    ff = wg_ref.shape[-1]
    for c0 in range(0, ff, FF_SUB):
        c1 = min(c0 + FF_SUB, ff)
        gate = _dot(h, wg_ref[:, c0:c1])
        up = _dot(h, wu_ref[:, c0:c1])
        part = _dot(((gate * _sigmoid(gate)) * up).astype(BF16), wd_ref[c0:c1, :])
        if c0 == 0:
            acc_ref[...] = part
        else:
            acc_ref[...] += part


def _ffn_kernel(x_ref, ya_ref, yb_ref, wa_ref, wb_ref, g_ref, wg_ref, wu_ref, wd_ref, o_ref, acc_ref):
    x = _mixed_residual(x_ref, ya_ref, yb_ref, wa_ref, wb_ref)
    _swiglu(x, g_ref, wg_ref, wu_ref, wd_ref, acc_ref)
    o_ref[...] = x + acc_ref[...]


def _ffn(x, mix, g, wg, wu, wd):
    n, d = x.shape
    tm = _tile(n, FF_ROWS)
    row = lambda i: (i, 0)
    const = lambda i: (0, 0)
    return pl.pallas_call(
        _ffn_kernel,
        grid=(n // tm,),
        in_specs=[pl.BlockSpec((tm, d), row)] + _mix_specs(mix, tm, row, const)
        + [pl.BlockSpec((1, d), const),
           pl.BlockSpec(wg.shape, const), pl.BlockSpec(wu.shape, const), pl.BlockSpec(wd.shape, const)],
        out_specs=pl.BlockSpec((tm, d), row),
        out_shape=jax.ShapeDtypeStruct((n, d), F32),
        scratch_shapes=[pltpu.VMEM((tm, d), F32)],
        compiler_params=_cparams(("parallel",)),
        name="ffn_dense",
    )(x, *mix, g, wg, wu, wd)


ROUTE_E1, ROUTE_E2, ROUTE_R1, ROUTE_R2, ROUTE_G1, ROUTE_G2 = range(6)


def _router_kernel(x_ref, ya_ref, yb_ref, wa_ref, wb_ref, g_ref, rhi_ref, tri_ref,
                   x1_ref, route_ref, cnt_ref, base_ref, *, n_experts):
    i = pl.program_id(0)
    lane = lax.broadcasted_iota(jnp.int32, (1, LANES), 1)

    @pl.when(i == 0)
    def _():
        base_ref[...] = jnp.zeros_like(base_ref)

    x1 = _mixed_residual(x_ref, ya_ref, yb_ref, wa_ref, wb_ref)
    x1_ref[...] = x1
    hf = _rms(x1) * g_ref[...]
    h_hi = hf.astype(BF16)
    h_lo = (hf - h_hi.astype(F32)).astype(BF16)
    hi_parts = _dot(h_hi, rhi_ref[...])
    logits = hi_parts[:, :LANES] + _dot(h_lo, rhi_ref[:, :LANES]) + hi_parts[:, LANES:]
    logits = jnp.where(lane < n_experts, logits, -jnp.inf)
    m1 = jnp.max(logits, axis=-1, keepdims=True)
    i1 = jnp.min(jnp.where(logits == m1, lane, LANES), axis=-1, keepdims=True)
    rest = jnp.where(lane == i1, -jnp.inf, logits)
    m2 = jnp.max(rest, axis=-1, keepdims=True)
    i2 = jnp.min(jnp.where(rest == m2, lane, LANES), axis=-1, keepdims=True)
    e2 = jnp.exp(m2 - m1)
    g1 = 1.0 / (1.0 + e2)
    g2 = e2 / (1.0 + e2)
    oh1 = (lane == i1).astype(F32)
    oh2 = (lane == i2).astype(F32)
    both = oh1 + oh2
    sub = tri_ref.shape[0]
    offset = base_ref[...]
    pieces = []
    for r0 in range(0, both.shape[0], sub):
        blk = both[r0:r0 + sub]
        pieces.append(_dot(tri_ref[...], blk.astype(BF16)) + offset)
        offset = offset + jnp.sum(blk, axis=0, keepdims=True)
    before = jnp.concatenate(pieces, axis=0)
    r1 = jnp.sum(oh1 * before, axis=-1, keepdims=True)
    r2 = jnp.sum(oh2 * before, axis=-1, keepdims=True)
    rec = jnp.zeros(route_ref.shape, F32)
    for slot, val in ((ROUTE_E1, i1.astype(F32)), (ROUTE_E2, i2.astype(F32)), (ROUTE_R1, r1),
                      (ROUTE_R2, r2), (ROUTE_G1, g1), (ROUTE_G2, g2)):
        rec = jnp.where(lane == slot, val, rec)
    route_ref[...] = rec
    base_ref[...] = offset
    cnt_ref[...] = offset


def _router(x, mix, g, router):
    n, d = x.shape
    ne = router.shape[1]
    tm = _tile(n, 1024)
    rpad = jnp.zeros((d, LANES), F32).at[:, :ne].set(router)
    rank_sub = _tile(tm, KEY_SUB)
    rhi = rpad.astype(BF16)
    rhi = jnp.concatenate([rhi, (rpad - rhi.astype(F32)).astype(BF16)], axis=1)
    row = lambda i: (i, 0)
    const = lambda i: (0, 0)
    return pl.pallas_call(
        functools.partial(_router_kernel, n_experts=ne),
        grid=(n // tm,),
        in_specs=[pl.BlockSpec((tm, d), row)] + _mix_specs(mix, tm, row, const)
        + [pl.BlockSpec((1, d), const),
           pl.BlockSpec((d, 2 * LANES), const), pl.BlockSpec((rank_sub, rank_sub), const)],
        out_specs=[pl.BlockSpec((tm, d), row), pl.BlockSpec((tm, LANES), row), pl.BlockSpec((1, LANES), const)],
        out_shape=[jax.ShapeDtypeStruct((n, d), F32), jax.ShapeDtypeStruct((n, LANES), F32),
                   jax.ShapeDtypeStruct((1, LANES), F32)],
        scratch_shapes=[pltpu.VMEM((1, LANES), F32)],
        compiler_params=_cparams(("arbitrary",)),
        name="moe_router",
    )(x, *mix, g, rhi, _tril_strict(rank_sub))


def _row_copy(src_ref, src_row, dst_ref, dst_row, sem):
    return pltpu.make_async_copy(src_ref.at[pl.ds(src_row, 1), :], dst_ref.at[pl.ds(dst_row, 1), :], sem)


def _dispatch_kernel(pad_lo_ref, pad_hi_ref, p1_ref, p2_ref, x_ref, xs_ref, sem, pad_sem, *, tb, tr, n_pad_ranges):
    def issue(t, carry):
        _row_copy(x_ref, t, xs_ref, p1_ref[0, 0, t], sem).start(priority=0)
        _row_copy(x_ref, t, xs_ref, p2_ref[0, 0, t], sem).start(priority=1)
        return carry

    lax.fori_loop(0, tb, issue, 0, unroll=8)

    @pl.when(pl.program_id(0) == 0)
    def _():
        runs = []

        def run(take, off, rows):
            if rows >= SUBLANES:
                off = pl.multiple_of(off, SUBLANES)
            cp = pltpu.make_async_copy(x_ref.at[pl.ds(0, rows), :], xs_ref.at[pl.ds(off, rows), :], pad_sem)
            pl.when(take)(cp.start)
            runs.append((take, cp))

        for e in range(n_pad_ranges - 1):
            lo = pad_lo_ref[e]
            hi = pad_hi_ref[e]
            mid = jnp.minimum((lo + SUBLANES - 1) // SUBLANES * SUBLANES, hi)
            for r in range(SUBLANES - 1):
                run(lo + r < mid, lo + r, 1)
            n = hi - mid
            off = mid
            for sh in range(tr.bit_length() - 2, SUBLANES.bit_length() - 2, -1):
                bit = 1 << sh
                take = (n & bit) != 0
                run(take, off, bit)
                off = off + jnp.where(take, bit, 0)

        tail_lo = pad_lo_ref[n_pad_ranges - 1]
        n_tail = (pad_hi_ref[n_pad_ranges - 1] - tail_lo) // tr

        def tail_copy(i):
            start = pl.multiple_of(tail_lo + i * tr, SUBLANES)
            return pltpu.make_async_copy(x_ref.at[pl.ds(0, tr), :], xs_ref.at[pl.ds(start, tr), :], pad_sem)

        lax.fori_loop(0, n_tail, lambda i, c: (tail_copy(i).start(), c)[1], 0)
        for take, cp in runs:
            pl.when(take)(cp.wait)
        lax.fori_loop(0, n_tail, lambda i, c: (tail_copy(i).wait(), c)[1], 0)

    for _ in range(TOP_K):
        pltpu.make_async_copy(x_ref, xs_ref.at[pl.ds(0, tb), :], sem).wait()


def _dispatch(x, pos1, pos2, n_rows, pad_lo, pad_hi, tr):
    n, d = x.shape
    tb = pos1.shape[-1]
    assert tr <= tb and tr & (tr - 1) == 0
    smem = lambda: pl.BlockSpec((1, 1, tb), lambda i, lo, hi: (i, 0, 0), memory_space=pltpu.SMEM)
    grid_spec = pltpu.PrefetchScalarGridSpec(
        num_scalar_prefetch=2,
        grid=(n // tb,),
        in_specs=[smem(), smem(), pl.BlockSpec((tb, d), lambda i, lo, hi: (i, 0))],
        out_specs=pl.BlockSpec(memory_space=pl.ANY),
        scratch_shapes=[pltpu.SemaphoreType.DMA(()), pltpu.SemaphoreType.DMA(())],
    )
    return pl.pallas_call(
        functools.partial(_dispatch_kernel, tb=tb, tr=tr, n_pad_ranges=pad_lo.shape[0]),
        grid_spec=grid_spec,
        out_shape=jax.ShapeDtypeStruct((n_rows, d), F32),
        compiler_params=_cparams(("arbitrary",)),
        name="moe_dispatch",
    )(pad_lo, pad_hi, pos1, pos2, x)


def _experts_kernel(te_ref, nv_ref, x_ref, g_ref, wg_ref, wu_ref, wd_ref, o_ref, acc_ref):
    del te_ref
    valid = pl.program_id(0) < nv_ref[0]

    @pl.when(valid)
    def _():
        _swiglu(x_ref[...], g_ref, wg_ref, wu_ref, wd_ref, acc_ref)
        o_ref[...] = acc_ref[...]

    @pl.when(jnp.logical_not(valid))
    def _():
        o_ref[...] = jnp.zeros_like(o_ref)


def _experts(xs, g, wg, wu, wd, tile_expert, n_valid, tr):
    n_rows, d = xs.shape
    ff = wg.shape[2]
    rowmap = lambda i, te, nv: (jnp.minimum(i, nv[0] - 1), 0)
    wmap = lambda i, te, nv: (te[i], 0, 0)
    grid_spec = pltpu.PrefetchScalarGridSpec(
        num_scalar_prefetch=2,
        grid=(n_rows // tr,),
        in_specs=[pl.BlockSpec((tr, d), rowmap), pl.BlockSpec((1, d), lambda i, te, nv: (0, 0)),
                  pl.BlockSpec((None, d, ff), wmap), pl.BlockSpec((None, d, ff), wmap),
                  pl.BlockSpec((None, ff, d), wmap)],
        out_specs=pl.BlockSpec((tr, d), lambda i, te, nv: (i, 0)),
        scratch_shapes=[pltpu.VMEM((tr, d), F32)],
    )
    return pl.pallas_call(
        _experts_kernel,
        grid_spec=grid_spec,
        out_shape=jax.ShapeDtypeStruct((n_rows, d), F32),
        compiler_params=_cparams(("arbitrary",)),
        name="moe_experts",
    )(tile_expert, n_valid, xs, g, wg, wu, wd)


def _combine_kernel(p1_ref, p2_ref, x_ref, route_ref, ys_ref, o_ref, buf_ref, sem, *, tc):
    def issue(t, carry):
        _row_copy(ys_ref, p1_ref[0, 0, t], buf_ref.at[0], t, sem).start(priority=0)
        _row_copy(ys_ref, p2_ref[0, 0, t], buf_ref.at[1], t, sem).start(priority=1)
        return carry

    lax.fori_loop(0, tc, issue, 0, unroll=8)
    for slot in range(TOP_K):
        pltpu.make_async_copy(ys_ref.at[pl.ds(0, tc), :], buf_ref.at[slot], sem).wait()
    lane = lax.broadcasted_iota(jnp.int32, (1, LANES), 1)
    route = route_ref[...]
    g1 = jnp.sum(jnp.where(lane == ROUTE_G1, route, 0.0), axis=-1, keepdims=True)
    g2 = jnp.sum(jnp.where(lane == ROUTE_G2, route, 0.0), axis=-1, keepdims=True)
    o_ref[...] = x_ref[...] + (g1 * buf_ref[0] + g2 * buf_ref[1])


def _combine(x, route, ys, pos1, pos2):
    n, d = x.shape
    tc = pos1.shape[-1]
    smem = lambda: pl.BlockSpec((1, 1, tc), lambda i: (i, 0, 0), memory_space=pltpu.SMEM)
    return pl.pallas_call(
        functools.partial(_combine_kernel, tc=tc),
        grid=(n // tc,),
        in_specs=[smem(), smem(), pl.BlockSpec((tc, d), lambda i: (i, 0)),
                  pl.BlockSpec((tc, LANES), lambda i: (i, 0)), pl.BlockSpec(memory_space=pl.ANY)],
        out_specs=pl.BlockSpec((tc, d), lambda i: (i, 0)),
        out_shape=jax.ShapeDtypeStruct((n, d), F32),
        scratch_shapes=[pltpu.VMEM((TOP_K, tc, d), F32), pltpu.SemaphoreType.DMA(())],
        compiler_params=_cparams(("arbitrary",)),
        name="moe_combine",
    )(pos1, pos2, x, route, ys)


def _moe(x, mix, g, router, wg, wu, wd):
    n, d = x.shape
    ne = wg.shape[0]
    tb = _tile(n, 512)
    tr = FF_ROWS if n >= 8192 else min(256, tb)
    x, route, cnt = _router(x, mix, g, router)
    counts = cnt[0, :ne].astype(jnp.int32)
    padded = ((counts + tr - 1) // tr) * tr
    ends = jnp.cumsum(padded)
    starts = ends - padded
    e1 = route[:, ROUTE_E1].astype(jnp.int32)
    e2 = route[:, ROUTE_E2].astype(jnp.int32)
    pos1 = (starts[e1] + route[:, ROUTE_R1].astype(jnp.int32)).reshape(n // tb, 1, tb)
    pos2 = (starts[e2] + route[:, ROUTE_R2].astype(jnp.int32)).reshape(n // tb, 1, tb)
    max_tiles = -(-(TOP_K * n + ne * (tr - 1)) // tr)
    n_valid = (ends[-1] // tr).reshape(1)
    tile_start = jnp.minimum(jnp.arange(max_tiles, dtype=jnp.int32), n_valid[0] - 1) * tr
    tile_expert = jnp.minimum(jnp.searchsorted(ends, tile_start, side="right"), ne - 1).astype(jnp.int32)
    n_rows = max_tiles * tr
    pad_lo = jnp.concatenate([starts + counts, ends[-1:]]).astype(jnp.int32)
    pad_hi = jnp.concatenate([ends, jnp.full((1,), n_rows, ends.dtype)]).astype(jnp.int32)
    xs = _dispatch(x, pos1, pos2, n_rows, pad_lo, pad_hi, tr)
    ys = _experts(xs, g, wg, wu, wd, tile_expert, n_valid.astype(jnp.int32), tr)
    return _combine(x, route, ys, pos1, pos2)


def kernel(x_prompt, x_sample, cache_k, cache_v, state_hgrn, norm_mix, w_in, sb_q_gain, sb_k_gain,
           hg_lower_bounds, sb_out_gain, hg_out_gain, w_out, norm_ffn, ffn_w_gate, ffn_w_up, ffn_w_down,
           moe_router, moe_w_gate, moe_w_up, moe_w_down):
    depth = w_in.shape[0]
    bp, tp, d = x_prompt.shape
    bs, ts, _ = x_sample.shape
    past = cache_k.shape[2]
    sb_heads, hd = cache_k.shape[3], cache_k.shape[4]
    sw = sb_heads * hd
    dk = hg_out_gain.shape[1]
    hw = hg_lower_bounds.shape[1]
    hg_heads = hw // dk
    sb_scale = hd ** -0.5

    lbs = jnp.cumsum(jax.nn.softmax(hg_lower_bounds.astype(F32), axis=0), axis=0)
    lbs = lbs - lbs[0:1]

    xp = x_prompt.reshape(bp * tp, d)
    xs = x_sample.reshape(bs * ts, d)
    zeros_state = jnp.zeros((bp, hg_heads, dk, dk), F32)
    outs = {k: [] for k in ("sp", "ss")}
    cache_kt = jnp.transpose(cache_k, (0, 1, 3, 4, 2)).reshape(depth, bs, sw, past)
    cache_vt = jnp.transpose(cache_v, (0, 1, 3, 4, 2)).reshape(depth, bs, sw, past)

    def mixer(x, b, t, l, kv_prev, *, cached, past_len, s0, bq):
        qg = (jnp.tile(sb_q_gain[l], sb_heads) * sb_scale)[None, :]
        kg = jnp.tile(sb_k_gain[l], sb_heads)[None, :]
        q, kf, kb, vf, vb, qh, kh, lf, ih, gh = _inproj(
            x, norm_mix[l][None, :], w_in[l].astype(BF16), qg, kg,
            jnp.log(lbs[l])[None, :], jnp.log1p(-lbs[l])[None, :], (1.0 - lbs[l])[None, :], kv_prev,
            sw=sw, hw=hw, hd=hd, seq_len=t)
        r3 = lambda a: a.reshape(b, t, a.shape[-1])
        kb3, vb3 = r3(kb), r3(vb)
        ya = _attention(r3(q), kb3, vb3, cache_kt if cached else kb3, cache_vt if cached else vb3,
                        jnp.tile(sb_out_gain[l], LANES // hd)[None, :],
                        bq=bq, past_len=past_len, hd=hd, past_layer=l if cached else None)
        yb, s_fin = _hgrn(r3(qh), r3(kh), r3(lf), r3(ih), r3(gh), s0, hg_out_gain[l][None, :], dk=dk)
        wo = w_out[l].astype(BF16)
        return (ya.reshape(b * t, sw), yb.reshape(b * t, hw), wo[:sw], wo[sw:]), (kf, vf), s_fin

    def channel(x, mix, l):
        j = l // 2
        if l % 2 == 0:
            return _ffn(x, mix, norm_ffn[l][None, :], ffn_w_gate[j].astype(BF16), ffn_w_up[j].astype(BF16),
                        ffn_w_down[j].astype(BF16))
        return _moe(x, mix, norm_ffn[l][None, :], moe_router[j], moe_w_gate[j].astype(BF16),
                    moe_w_up[j].astype(BF16), moe_w_down[j].astype(BF16))

    bq_p = _tile(tp, 512)
    kv_p = kv_s = None
    for l in range(depth):
        mix_p, kv_p, s_p = mixer(xp, bp, tp, l, kv_p, cached=False, past_len=None, s0=zeros_state, bq=bq_p)
        outs["sp"].append(s_p)
        mix_s, kv_s, s_s = mixer(xs, bs, ts, l, kv_s, cached=True, past_len=past,
                                 s0=state_hgrn[l].astype(F32), bq=ts)
        outs["ss"].append(s_s)
        xp = channel(xp, mix_p, l)
        xs = channel(xs, mix_s, l)

    heads5 = lambda a, b, t: jnp.transpose(a.reshape(depth, b, sb_heads, hd, t), (0, 1, 4, 2, 3))
    return (xp.reshape(bp, tp, d), xs.reshape(bs, ts, d), heads5(kv_p[0], bp, tp), heads5(kv_p[1], bp, tp),
            jnp.stack(outs["sp"]), heads5(kv_s[0], bs, ts), heads5(kv_s[1], bs, ts), jnp.stack(outs["ss"]))
```

```python
import functools

import numpy as np
import jax
import jax.numpy as jnp
from jax import lax
from jax.experimental import pallas as pl
from jax.experimental.pallas import tpu as pltpu

EPS = 1e-6
LOG2_E = 1.4426950408889634
TOP_K = 2
LANES = 128
SUBLANES = 8
F32 = jnp.float32
BF16 = jnp.bfloat16
VMEM_LIMIT_BYTES = 56 * 1024 * 1024


def _cparams(semantics):
    return pltpu.CompilerParams(dimension_semantics=semantics, vmem_limit_bytes=VMEM_LIMIT_BYTES)


def _dot(a, b):
    return jnp.dot(a, b, preferred_element_type=F32)


def _dot_nt(a, b):
    return lax.dot_general(a, b, (((1,), (1,)), ((), ())), preferred_element_type=F32)


def _dot_tn(a, b):
    return lax.dot_general(a, b, (((0,), (0,)), ((), ())), preferred_element_type=F32)


def _sigmoid(x):
    return 1.0 / (1.0 + jnp.exp(-x))


def _rms(x):
    return x * lax.rsqrt(jnp.mean(x * x, axis=-1, keepdims=True) + EPS)


def _tile(n, pref):
    t = min(n, pref)
    assert n % t == 0, (n, t)
    return t


def _inproj_kernel(*refs, sw, hw, hd, n_prev, seq_per_tile):
    x_ref, g_ref, w_ref, qg_ref, kg_ref, llb_ref, l1m_ref, oml_ref = refs[:8]
    prev = refs[8:8 + (2 if n_prev else 0)]
    q_ref, kf_ref, kb_ref, vf_ref, vb_ref, qh_ref, kh_ref, lf_ref, ih_ref, gh_ref = refs[8 + len(prev):]
    if n_prev:
        kf_ref[:n_prev] = prev[0][...]
        vf_ref[:n_prev] = prev[1][...]

    def store_time_minor(ref, y):
        t = y.shape[0] // seq_per_tile
        for sq in range(seq_per_tile):
            ref[n_prev, sq] = y[sq * t:(sq + 1) * t, :].T

    h = (_rms(x_ref[...]) * g_ref[...]).astype(BF16)

    def seg(lo, width):
        return _dot(h, w_ref[:, lo:lo + width])

    def headnorm(y, gain):
        lane = lax.broadcasted_iota(jnp.int32, (1, LANES), 1)
        tiles = []
        for c0 in range(0, y.shape[1], LANES):
            y2 = y[:, c0:c0 + LANES] * y[:, c0:c0 + LANES]
            m = jnp.zeros_like(y2)
            for h0 in range(0, LANES, hd):
                in_head = (lane >= h0) & (lane < h0 + hd)
                m = jnp.where(in_head, jnp.sum(jnp.where(in_head, y2, 0.0), axis=-1, keepdims=True), m)
            tiles.append(m * (1.0 / hd))
        m = jnp.concatenate(tiles, axis=1)
        return (y * lax.rsqrt(m + EPS)) * gain

    q_ref[...] = headnorm(seg(0, sw), qg_ref[...]).astype(BF16)
    ka = headnorm(seg(sw, sw), kg_ref[...])
    store_time_minor(kf_ref, ka)
    kb_ref[...] = ka.astype(BF16)
    va = seg(2 * sw, sw)
    store_time_minor(vf_ref, va)
    vb_ref[...] = va.astype(BF16)

    o = 3 * sw
    qb = seg(o, hw)
    qh_ref[...] = qb * _sigmoid(qb)

    fb = seg(o + hw, hw)
    e = jnp.exp(-jnp.abs(fb))
    log_sig = jnp.minimum(fb, 0.0) - jnp.log(1.0 + e)
    c = l1m_ref[...] + log_sig
    a = llb_ref[...]
    lf_ref[...] = jnp.maximum(a, c) + jnp.log(1.0 + jnp.exp(-jnp.abs(a - c)))
    kh_ref[...] = oml_ref[...] * (jnp.where(fb >= 0.0, e, 1.0) / (1.0 + e))

    ih_ref[...] = seg(o + 2 * hw, hw)
    gh_ref[...] = seg(o + 3 * hw, hw)


def _inproj(x, g, w, qg, kg, llb, l1m, oml, kv_prev, *, sw, hw, hd, seq_len):
    n, d = x.shape
    tm = _tile(n, 512)
    n_prev = 0 if kv_prev is None else kv_prev[0].shape[0]
    spt = max(tm // seq_len, 1)
    tt = tm // spt
    nt = seq_len // tt
    assert spt * tt == tm and nt * tt == seq_len
    row = lambda i: (i, 0)
    const = lambda i: (0, 0)
    stacked = lambda i: (0, i // nt, 0, i % nt)
    kv_shape = (n_prev + 1, n // seq_len, sw, seq_len)
    specs = {"row_sw": pl.BlockSpec((tm, sw), row), "row_hw": pl.BlockSpec((tm, hw), row),
             "stack": pl.BlockSpec((n_prev + 1, spt, sw, tt), stacked)}
    outs = [("row_sw", (n, sw), BF16), ("stack", kv_shape, F32), ("row_sw", (n, sw), BF16),
            ("stack", kv_shape, F32), ("row_sw", (n, sw), BF16)] + [("row_hw", (n, hw), F32)] * 5
    prev_specs = [pl.BlockSpec((n_prev, spt, sw, tt), stacked)] * 2 if n_prev else []
    return pl.pallas_call(
        functools.partial(_inproj_kernel, sw=sw, hw=hw, hd=hd, n_prev=n_prev, seq_per_tile=spt),
        grid=(n // tm,),
        in_specs=[pl.BlockSpec((tm, d), row), pl.BlockSpec((1, d), const),
                  pl.BlockSpec(w.shape, const),
                  pl.BlockSpec((1, sw), const), pl.BlockSpec((1, sw), const),
                  pl.BlockSpec((1, hw), const), pl.BlockSpec((1, hw), const), pl.BlockSpec((1, hw), const)]
        + prev_specs,
        out_specs=[specs[kind] for kind, _, _ in outs],
        out_shape=[jax.ShapeDtypeStruct(shape, dt) for _, shape, dt in outs],
        compiler_params=_cparams(("parallel",)),
        name="inproj",
    )(x, g, w, qg, kg, llb, l1m, oml, *(kv_prev or ()))


KEY_SUB = 256
SOFTPLUS_CLAMP = 80.0
CARRY_DEAD = 104.0


def _attn_kernel(q_ref, kn_ref, vn_ref, kp_hbm, vp_hbm, un_ref, up_ref, gain_ref,
                 y_ref, qm_ref, acc_ref, carry_ref, kbuf, vbuf, sem, alive_ref,
                 *, bq, subn, subp, hd, past_len, past_layer, past_transposed):
    bb = pl.program_id(0)
    p = pl.program_id(1)
    qi = pl.program_id(2)
    lane = lax.broadcasted_iota(jnp.int32, (1, LANES), 1)
    heads = LANES // hd
    n_past = (qi * bq if past_len is None else past_len) // subp

    def fetch(j, slot):
        if past_transposed:
            window = (past_layer, bb, pl.ds(p * LANES, LANES), pl.ds(j * subp, subp))
        else:
            window = (bb, pl.ds(j * subp, subp), pl.ds(p * LANES, LANES))
        return (pltpu.make_async_copy(kp_hbm.at[window], kbuf.at[slot], sem.at[0, slot]),
                pltpu.make_async_copy(vp_hbm.at[window], vbuf.at[slot], sem.at[1, slot]))

    def slot_of(j):
        return (n_past - 1 - j) & 1

    @pl.when(n_past > 0)
    def _():
        for cp in fetch(n_past - 1, 0):
            cp.start()

    q = q_ref[...]
    for hh in range(heads):
        in_head = (lane >= hh * hd) & (lane < (hh + 1) * hd)
        qm_ref[hh] = jnp.where(in_head, q, jnp.zeros_like(q))
    acc_ref[...] = jnp.zeros_like(acc_ref)
    carry_ref[...] = jnp.zeros_like(carry_ref)

    groups = bq // subn

    def one_head(hh, k, v, u, r0, r1, mask, transposed):
        if transposed:
            z = _dot(qm_ref[hh, r0:r1, :], k)
        else:
            z = _dot_nt(qm_ref[hh, r0:r1, :], k)
        sp = jnp.maximum(jnp.log(1.0 + jnp.exp(jnp.minimum(z, SOFTPLUS_CLAMP))), z)
        if mask is not None:
            sp = jnp.where(mask, sp, 0.0)
        after = _dot(sp.astype(BF16), u)
        w = jnp.exp((z - sp) - after)
        if mask is not None:
            w = jnp.where(mask, w, 0.0)
        pv = _dot_nt(w.astype(BF16), v) if transposed else _dot(w.astype(BF16), v)
        c = carry_ref[hh, r0:r1, :]
        acc_ref[hh, r0:r1, :] += jnp.exp(-c) * pv
        c_new = c + jnp.sum(sp, axis=-1, keepdims=True)
        carry_ref[hh, r0:r1, :] = c_new
        return c_new

    def set_alive(hh, g, c_group):
        alive_ref[hh * groups + g] = (jnp.min(c_group) < CARRY_DEAD).astype(jnp.int32)

    def diagonal_block(flag_groups):
        u_new = un_ref[...]
        for j in range(groups - 1, -1, -1):
            r0 = j * subn
            rows = r0 + lax.broadcasted_iota(jnp.int32, (bq - r0, subn), 0)
            cols = r0 + lax.broadcasted_iota(jnp.int32, (bq - r0, subn), 1)
            k = kn_ref[r0:r0 + subn, :].astype(BF16)
            v = vn_ref[r0:r0 + subn, :].astype(BF16)
            for hh in range(heads):
                c_new = one_head(hh, k, v, u_new, r0, bq, cols < rows, False)
                if j == 0:
                    for g in flag_groups:
                        set_alive(hh, g, c_new[g * subn:(g + 1) * subn])

    def group_heads(slot, g, head_list):
        k = kbuf[slot].astype(BF16)
        v = vbuf[slot].astype(BF16)
        u_past = up_ref[...]
        for hh in head_list:
            c_new = one_head(hh, k, v, u_past, g * subn, (g + 1) * subn, None, past_transposed)
            set_alive(hh, g, c_new)

    def alive_groups(slot, group_list):
        for g in group_list:
            n_alive = alive_ref[g]
            for hh in range(1, heads):
                n_alive += alive_ref[hh * groups + g]
            pl.when(n_alive == heads)(functools.partial(group_heads, slot, g, range(heads)))
            for hh in range(heads):
                pl.when((n_alive < heads) & (alive_ref[hh * groups + g] == 1))(
                    functools.partial(group_heads, slot, g, (hh,)))

    @pl.when(n_past == 0)
    def _():
        diagonal_block(range(groups))

    @pl.when(n_past > 0)
    def _():
        diagonal_block(range(1, groups))
        for cp in fetch(n_past - 1, 0):
            cp.wait()
        for cp in fetch(jnp.maximum(n_past - 2, 0), 1):
            cp.start()
        group_heads(0, 0, range(heads))
        alive_groups(0, range(1, groups))

    def any_alive():
        total = alive_ref[0]
        for i in range(1, heads * groups):
            total += alive_ref[i]
        return total > 0

    def past_block(j):
        slot = slot_of(j)
        for cp in fetch(j, slot):
            cp.wait()

        @pl.when(j > 0)
        def _():
            for cp in fetch(j - 1, 1 - slot):
                cp.start()

        alive_groups(slot, range(groups))
        return j - 1

    j_end = lax.while_loop(lambda j: (j >= 0) & any_alive(), past_block, n_past - 2)

    @pl.when(j_end >= 0)
    def _():
        for cp in fetch(j_end, slot_of(j_end)):
            cp.wait()

    @pl.when(n_past == 1)
    def _():
        for cp in fetch(0, 1):
            cp.wait()

    o = acc_ref[0]
    for hh in range(1, heads):
        in_head = (lane >= hh * hd) & (lane < (hh + 1) * hd)
        o = jnp.where(in_head, acc_ref[hh], o)
    o2 = o * o
    ms = jnp.zeros_like(o)
    for hh in range(heads):
        in_head = (lane >= hh * hd) & (lane < (hh + 1) * hd)
        ssum = jnp.sum(jnp.where(in_head, o2, 0.0), axis=-1, keepdims=True)
        ms = jnp.where(in_head, ssum * (1.0 / hd), ms)
    y_ref[...] = ((o * lax.rsqrt(ms + EPS)) * gain_ref[...]).astype(y_ref.dtype)


def _tril_strict(n):
    return jnp.asarray(np.tril(np.ones((n, n), np.float32), -1), BF16)


def _attention(q, k_new, v_new, k_past, v_past, gain, *, bq, past_len, hd, past_layer=None):
    b, tq, w = q.shape
    past_transposed = past_layer is not None
    subn = min(bq, KEY_SUB)
    subp = KEY_SUB
    assert bq % subn == 0 and (bq if past_len is None else past_len) % subp == 0
    heads = LANES // hd
    qmap = lambda bb, p, qi: (bb, qi, p)
    cmap = lambda bb, p, qi: (0, 0)
    past_buf = (2, LANES, subp) if past_transposed else (2, subp, LANES)
    return pl.pallas_call(
        functools.partial(_attn_kernel, bq=bq, subn=subn, subp=subp, hd=hd, past_len=past_len,
                          past_layer=past_layer, past_transposed=past_transposed),
        grid=(b, w // LANES, tq // bq),
        in_specs=[pl.BlockSpec((None, bq, LANES), qmap),
                  pl.BlockSpec((None, bq, LANES), qmap), pl.BlockSpec((None, bq, LANES), qmap),
                  pl.BlockSpec(memory_space=pl.ANY), pl.BlockSpec(memory_space=pl.ANY),
                  pl.BlockSpec((subn, subn), cmap), pl.BlockSpec((subp, subp), cmap),
                  pl.BlockSpec((1, LANES), cmap)],
        out_specs=pl.BlockSpec((None, bq, LANES), qmap),
        out_shape=jax.ShapeDtypeStruct((b, tq, w), BF16),
        scratch_shapes=[pltpu.VMEM((heads, bq, LANES), BF16), pltpu.VMEM((heads, bq, LANES), F32),
                        pltpu.VMEM((heads, bq, LANES), F32),
                        pltpu.VMEM(past_buf, k_past.dtype), pltpu.VMEM(past_buf, v_past.dtype),
                        pltpu.SemaphoreType.DMA((2, 2)), pltpu.SMEM((heads * (bq // subn),), jnp.int32)],
        compiler_params=_cparams(("parallel", "parallel", "arbitrary")),
        name="stickbreak_attn",
    )(q, k_new, v_new, k_past, v_past, _tril_strict(subn), _tril_strict(subp), gain)


def _hgrn_halvings(c):
    return [c >> (i + 1) for i in range(int(np.log2(c)))]


def _hgrn_masks(tl):
    t = np.arange(tl)
    masks = [np.eye(tl, dtype=bool)]
    for h in _hgrn_halvings(tl):
        blk = t // (2 * h)
        second = (t // h) % 2 == 1
        masks.append((blk[:, None] == blk[None, :]) & second[:, None] & (~second)[None, :])
    return jnp.asarray(np.stack(masks).astype(np.float32))


def _boundary_rows(b, h, row):
    c, dk = b.shape
    if 2 * h >= 8:
        n = c // (2 * h)
        ref = b.reshape(n, 2 * h, dk)[:, h - 1:h, :]
        return jnp.broadcast_to(ref, (n, 2 * h, dk)).reshape(c, dk)
    down1 = pltpu.roll(b, 1, 0)
    if h == 1:
        return jnp.where((row & 1) == 1, down1, b)
    m = row & 3
    up1 = pltpu.roll(b, c - 1, 0)
    down2 = pltpu.roll(b, 2, 0)
    return jnp.where(m == 0, up1, jnp.where(m == 1, b, jnp.where(m == 2, down1, down2)))


def _hgrn_kernel(q_ref, k_ref, lf_ref, v_ref, g_ref, s0_ref, tri_ref, msk_ref, gain_ref,
                 y_ref, sout_ref, st_ref, *, c, n_chunks):
    t = pl.program_id(2)
    dk = q_ref.shape[-1]

    @pl.when(t == 0)
    def _():
        st_ref[...] = s0_ref[...].T

    row = lax.broadcasted_iota(jnp.int32, (c, dk), 0)
    tri = tri_ref[...]
    tl = msk_ref.shape[-1]
    for ci in range(n_chunks):
        sl = slice(ci * c, (ci + 1) * c)
        q = q_ref[sl, :]
        k = k_ref[sl, :]
        lf = lf_ref[sl, :]
        v = v_ref[sl, :].astype(BF16)
        g = g_ref[sl, :]
        hi = lf.astype(BF16)
        r1 = lf - hi.astype(F32)
        mid = r1.astype(BF16)
        lo = (r1 - mid.astype(F32)).astype(BF16)
        b = _dot(tri, hi) + _dot(tri, mid) + _dot(tri, lo)
        st = st_ref[...]
        b2 = b * LOG2_E
        o = _dot_nt((q * jnp.exp2(b2)).astype(BF16), st.astype(BF16))
        tiles = [slice(ti * tl, (ti + 1) * tl) for ti in range(c // tl)]
        qb, kb = q.astype(BF16), k.astype(BF16)
        a_diag = [msk_ref[0] * _dot_nt(qb[rs], kb[rs]) for rs in tiles]
        a_cross = None
        small = 0
        for h in _hgrn_halvings(c):
            el = jnp.exp2(-jnp.abs(b2 - _boundary_rows(b2, h, row)))
            qe, ke = (q * el).astype(BF16), (k * el).astype(BF16)
            if h >= tl:
                a_cross = _dot_nt(qe[tiles[1]], ke[tiles[0]])
            else:
                small += 1
                for ti, rs in enumerate(tiles):
                    a_diag[ti] += msk_ref[small] * _dot_nt(qe[rs], ke[rs])
        b2_last = b2[c - 1:c, :]
        k_end = (k * jnp.exp2(b2_last - b2)).astype(BF16)
        st_ref[...] = st * jnp.exp2(b2_last) + _dot_tn(v, k_end)
        for ti, rs in enumerate(tiles):
            o_t = o[rs] + _dot(a_diag[ti].astype(BF16), v[rs])
            if ti == 1:
                o_t += _dot(a_cross.astype(BF16), v[tiles[0]])
            y = (_rms(o_t) * gain_ref[...]) * (g[rs] * _sigmoid(g[rs]))
            y_ref[ci * c + ti * tl:ci * c + (ti + 1) * tl, :] = y.astype(y_ref.dtype)

    @pl.when(t == pl.num_programs(2) - 1)
    def _():
        sout_ref[...] = st_ref[...].T


def _hgrn(q, k, lf, v, g, s0, gain, *, dk):
    b, t, hw = q.shape
    nh = hw // dk
    c = _tile(t, 2 * LANES)
    tt = _tile(t, 1024)
    msk = _hgrn_masks(min(c, LANES))
    tri = jnp.asarray(np.tril(np.ones((c, c), np.float32)), BF16)
    tok = lambda bb, h, ti: (bb, ti, h)
    smap = lambda bb, h, ti: (bb, h, 0, 0)
    return pl.pallas_call(
        functools.partial(_hgrn_kernel, c=c, n_chunks=tt // c),
        grid=(b, nh, t // tt),
        in_specs=[pl.BlockSpec((None, tt, dk), tok)] * 5
        + [pl.BlockSpec((None, None, dk, dk), smap),
           pl.BlockSpec((c, c), lambda bb, h, ti: (0, 0)),
           pl.BlockSpec(msk.shape, lambda bb, h, ti: (0, 0, 0)),
           pl.BlockSpec((1, dk), lambda bb, h, ti: (0, 0))],
        out_specs=[pl.BlockSpec((None, tt, dk), tok), pl.BlockSpec((None, None, dk, dk), smap)],
        out_shape=[jax.ShapeDtypeStruct((b, t, hw), BF16), jax.ShapeDtypeStruct((b, nh, dk, dk), F32)],
        scratch_shapes=[pltpu.VMEM((dk, dk), F32)],
        compiler_params=_cparams(("parallel", "parallel", "arbitrary")),
        name="hgrn2",
    )(q, k, lf, v, g, s0, tri, msk, gain)


def _mixed_residual(x_ref, ya_ref, yb_ref, wa_ref, wb_ref):
    return x_ref[...] + _dot(ya_ref[...], wa_ref[...]) + _dot(yb_ref[...], wb_ref[...])


def _mix_specs(mix, tm, index_row, index_const):
    ya, yb, wa, wb = mix
    return [pl.BlockSpec((tm, ya.shape[1]), index_row), pl.BlockSpec((tm, yb.shape[1]), index_row),
            pl.BlockSpec(wa.shape, index_const), pl.BlockSpec(wb.shape, index_const)]


FF_SUB = 256
FF_ROWS = 512


def _swiglu(x, g_ref, wg_ref, wu_ref, wd_ref, acc_ref):
    h = (_rms(x) * g_ref[...]).astype(BF16)
    ff = wg_ref.shape[-1]
    for c0 in range(0, ff, FF_SUB):
        c1 = min(c0 + FF_SUB, ff)
        gate = _dot(h, wg_ref[:, c0:c1])
        up = _dot(h, wu_ref[:, c0:c1])
        part = _dot(((gate * _sigmoid(gate)) * up).astype(BF16), wd_ref[c0:c1, :])
        if c0 == 0:
            acc_ref[...] = part
        else:
            acc_ref[...] += part


def _ffn_kernel(x_ref, ya_ref, yb_ref, wa_ref, wb_ref, g_ref, wg_ref, wu_ref, wd_ref, o_ref, acc_ref):
    x = _mixed_residual(x_ref, ya_ref, yb_ref, wa_ref, wb_ref)
    _swiglu(x, g_ref, wg_ref, wu_ref, wd_ref, acc_ref)
    o_ref[...] = x + acc_ref[...]


def _ffn(x, mix, g, wg, wu, wd):
    n, d = x.shape
    tm = _tile(n, FF_ROWS)
    row = lambda i: (i, 0)
    const = lambda i: (0, 0)
    return pl.pallas_call(
        _ffn_kernel,
        grid=(n // tm,),
        in_specs=[pl.BlockSpec((tm, d), row)] + _mix_specs(mix, tm, row, const)
        + [pl.BlockSpec((1, d), const),
           pl.BlockSpec(wg.shape, const), pl.BlockSpec(wu.shape, const), pl.BlockSpec(wd.shape, const)],
        out_specs=pl.BlockSpec((tm, d), row),
        out_shape=jax.ShapeDtypeStruct((n, d), F32),
        scratch_shapes=[pltpu.VMEM((tm, d), F32)],
        compiler_params=_cparams(("parallel",)),
        name="ffn_dense",
    )(x, *mix, g, wg, wu, wd)


ROUTE_E1, ROUTE_E2, ROUTE_R1, ROUTE_R2, ROUTE_G1, ROUTE_G2 = range(6)


def _router_kernel(x_ref, ya_ref, yb_ref, wa_ref, wb_ref, g_ref, rhi_ref, tri_ref,
                   x1_ref, route_ref, cnt_ref, base_ref, *, n_experts):
    i = pl.program_id(0)
    lane = lax.broadcasted_iota(jnp.int32, (1, LANES), 1)

    @pl.when(i == 0)
    def _():
        base_ref[...] = jnp.zeros_like(base_ref)

    x1 = _mixed_residual(x_ref, ya_ref, yb_ref, wa_ref, wb_ref)
    x1_ref[...] = x1
    hf = _rms(x1) * g_ref[...]
    h_hi = hf.astype(BF16)
    h_lo = (hf - h_hi.astype(F32)).astype(BF16)
    hi_parts = _dot(h_hi, rhi_ref[...])
    logits = hi_parts[:, :LANES] + _dot(h_lo, rhi_ref[:, :LANES]) + hi_parts[:, LANES:]
    logits = jnp.where(lane < n_experts, logits, -jnp.inf)
    m1 = jnp.max(logits, axis=-1, keepdims=True)
    i1 = jnp.min(jnp.where(logits == m1, lane, LANES), axis=-1, keepdims=True)
    rest = jnp.where(lane == i1, -jnp.inf, logits)
    m2 = jnp.max(rest, axis=-1, keepdims=True)
    i2 = jnp.min(jnp.where(rest == m2, lane, LANES), axis=-1, keepdims=True)
    e2 = jnp.exp(m2 - m1)
    g1 = 1.0 / (1.0 + e2)
    g2 = e2 / (1.0 + e2)
    oh1 = (lane == i1).astype(F32)
    oh2 = (lane == i2).astype(F32)
    both = oh1 + oh2
    sub = tri_ref.shape[0]
    offset = base_ref[...]
    pieces = []
    for r0 in range(0, both.shape[0], sub):
        blk = both[r0:r0 + sub]
        pieces.append(_dot(tri_ref[...], blk.astype(BF16)) + offset)
        offset = offset + jnp.sum(blk, axis=0, keepdims=True)
    before = jnp.concatenate(pieces, axis=0)
    r1 = jnp.sum(oh1 * before, axis=-1, keepdims=True)
    r2 = jnp.sum(oh2 * before, axis=-1, keepdims=True)
    rec = jnp.zeros(route_ref.shape, F32)
    for slot, val in ((ROUTE_E1, i1.astype(F32)), (ROUTE_E2, i2.astype(F32)), (ROUTE_R1, r1),
                      (ROUTE_R2, r2), (ROUTE_G1, g1), (ROUTE_G2, g2)):
        rec = jnp.where(lane == slot, val, rec)
    route_ref[...] = rec
    base_ref[...] = offset
    cnt_ref[...] = offset


def _router(x, mix, g, router):
    n, d = x.shape
    ne = router.shape[1]
    tm = _tile(n, 1024)
    rpad = jnp.zeros((d, LANES), F32).at[:, :ne].set(router)
    rank_sub = _tile(tm, KEY_SUB)
    rhi = rpad.astype(BF16)
    rhi = jnp.concatenate([rhi, (rpad - rhi.astype(F32)).astype(BF16)], axis=1)
    row = lambda i: (i, 0)
    const = lambda i: (0, 0)
    return pl.pallas_call(
        functools.partial(_router_kernel, n_experts=ne),
        grid=(n // tm,),
        in_specs=[pl.BlockSpec((tm, d), row)] + _mix_specs(mix, tm, row, const)
        + [pl.BlockSpec((1, d), const),
           pl.BlockSpec((d, 2 * LANES), const), pl.BlockSpec((rank_sub, rank_sub), const)],
        out_specs=[pl.BlockSpec((tm, d), row), pl.BlockSpec((tm, LANES), row), pl.BlockSpec((1, LANES), const)],
        out_shape=[jax.ShapeDtypeStruct((n, d), F32), jax.ShapeDtypeStruct((n, LANES), F32),
                   jax.ShapeDtypeStruct((1, LANES), F32)],
        scratch_shapes=[pltpu.VMEM((1, LANES), F32)],
        compiler_params=_cparams(("arbitrary",)),
        name="moe_router",
    )(x, *mix, g, rhi, _tril_strict(rank_sub))


def _row_copy(src_ref, src_row, dst_ref, dst_row, sem):
    return pltpu.make_async_copy(src_ref.at[pl.ds(src_row, 1), :], dst_ref.at[pl.ds(dst_row, 1), :], sem)


def _dispatch_kernel(pad_lo_ref, pad_hi_ref, p1_ref, p2_ref, x_ref, xs_ref, sem, pad_sem, *, tb, tr, n_pad_ranges):
    def issue(t, carry):
        _row_copy(x_ref, t, xs_ref, p1_ref[0, 0, t], sem).start(priority=0)
        _row_copy(x_ref, t, xs_ref, p2_ref[0, 0, t], sem).start(priority=1)
        return carry

    lax.fori_loop(0, tb, issue, 0, unroll=8)

    @pl.when(pl.program_id(0) == 0)
    def _():
        runs = []

        def run(take, off, rows):
            if rows >= SUBLANES:
                off = pl.multiple_of(off, SUBLANES)
            cp = pltpu.make_async_copy(x_ref.at[pl.ds(0, rows), :], xs_ref.at[pl.ds(off, rows), :], pad_sem)
            pl.when(take)(cp.start)
            runs.append((take, cp))

        for e in range(n_pad_ranges - 1):
            lo = pad_lo_ref[e]
            hi = pad_hi_ref[e]
            mid = jnp.minimum((lo + SUBLANES - 1) // SUBLANES * SUBLANES, hi)
            for r in range(SUBLANES - 1):
                run(lo + r < mid, lo + r, 1)
            n = hi - mid
            off = mid
            for sh in range(tr.bit_length() - 2, SUBLANES.bit_length() - 2, -1):
                bit = 1 << sh
                take = (n & bit) != 0
                run(take, off, bit)
                off = off + jnp.where(take, bit, 0)

        tail_lo = pad_lo_ref[n_pad_ranges - 1]
        n_tail = (pad_hi_ref[n_pad_ranges - 1] - tail_lo) // tr

        def tail_copy(i):
            start = pl.multiple_of(tail_lo + i * tr, SUBLANES)
            return pltpu.make_async_copy(x_ref.at[pl.ds(0, tr), :], xs_ref.at[pl.ds(start, tr), :], pad_sem)

        lax.fori_loop(0, n_tail, lambda i, c: (tail_copy(i).start(), c)[1], 0)
        for take, cp in runs:
            pl.when(take)(cp.wait)
        lax.fori_loop(0, n_tail, lambda i, c: (tail_copy(i).wait(), c)[1], 0)

    for _ in range(TOP_K):
        pltpu.make_async_copy(x_ref, xs_ref.at[pl.ds(0, tb), :], sem).wait()


def _dispatch(x, pos1, pos2, n_rows, pad_lo, pad_hi, tr):
    n, d = x.shape
    tb = pos1.shape[-1]
    assert tr <= tb and tr & (tr - 1) == 0
    smem = lambda: pl.BlockSpec((1, 1, tb), lambda i, lo, hi: (i, 0, 0), memory_space=pltpu.SMEM)
    grid_spec = pltpu.PrefetchScalarGridSpec(
        num_scalar_prefetch=2,
        grid=(n // tb,),
        in_specs=[smem(), smem(), pl.BlockSpec((tb, d), lambda i, lo, hi: (i, 0))],
        out_specs=pl.BlockSpec(memory_space=pl.ANY),
        scratch_shapes=[pltpu.SemaphoreType.DMA(()), pltpu.SemaphoreType.DMA(())],
    )
    return pl.pallas_call(
        functools.partial(_dispatch_kernel, tb=tb, tr=tr, n_pad_ranges=pad_lo.shape[0]),
        grid_spec=grid_spec,
        out_shape=jax.ShapeDtypeStruct((n_rows, d), F32),
        compiler_params=_cparams(("arbitrary",)),
        name="moe_dispatch",
    )(pad_lo, pad_hi, pos1, pos2, x)


def _experts_kernel(te_ref, nv_ref, x_ref, g_ref, wg_ref, wu_ref, wd_ref, o_ref, acc_ref):
    del te_ref
    valid = pl.program_id(0) < nv_ref[0]

    @pl.when(valid)
    def _():
        _swiglu(x_ref[...], g_ref, wg_ref, wu_ref, wd_ref, acc_ref)
        o_ref[...] = acc_ref[...]

    @pl.when(jnp.logical_not(valid))
    def _():
        o_ref[...] = jnp.zeros_like(o_ref)


def _experts(xs, g, wg, wu, wd, tile_expert, n_valid, tr):
    n_rows, d = xs.shape
    ff = wg.shape[2]
    rowmap = lambda i, te, nv: (jnp.minimum(i, nv[0] - 1), 0)
    wmap = lambda i, te, nv: (te[i], 0, 0)
    grid_spec = pltpu.PrefetchScalarGridSpec(
        num_scalar_prefetch=2,
        grid=(n_rows // tr,),
        in_specs=[pl.BlockSpec((tr, d), rowmap), pl.BlockSpec((1, d), lambda i, te, nv: (0, 0)),
                  pl.BlockSpec((None, d, ff), wmap), pl.BlockSpec((None, d, ff), wmap),
                  pl.BlockSpec((None, ff, d), wmap)],
        out_specs=pl.BlockSpec((tr, d), lambda i, te, nv: (i, 0)),
        scratch_shapes=[pltpu.VMEM((tr, d), F32)],
    )
    return pl.pallas_call(
        _experts_kernel,
        grid_spec=grid_spec,
        out_shape=jax.ShapeDtypeStruct((n_rows, d), F32),
        compiler_params=_cparams(("arbitrary",)),
        name="moe_experts",
    )(tile_expert, n_valid, xs, g, wg, wu, wd)


def _combine_kernel(p1_ref, p2_ref, x_ref, route_ref, ys_ref, o_ref, buf_ref, sem, *, tc):
    def issue(t, carry):
        _row_copy(ys_ref, p1_ref[0, 0, t], buf_ref.at[0], t, sem).start(priority=0)
        _row_copy(ys_ref, p2_ref[0, 0, t], buf_ref.at[1], t, sem).start(priority=1)
        return carry

    lax.fori_loop(0, tc, issue, 0, unroll=8)
    for slot in range(TOP_K):
        pltpu.make_async_copy(ys_ref.at[pl.ds(0, tc), :], buf_ref.at[slot], sem).wait()
    lane = lax.broadcasted_iota(jnp.int32, (1, LANES), 1)
    route = route_ref[...]
    g1 = jnp.sum(jnp.where(lane == ROUTE_G1, route, 0.0), axis=-1, keepdims=True)
    g2 = jnp.sum(jnp.where(lane == ROUTE_G2, route, 0.0), axis=-1, keepdims=True)
    o_ref[...] = x_ref[...] + (g1 * buf_ref[0] + g2 * buf_ref[1])


def _combine(x, route, ys, pos1, pos2):
    n, d = x.shape
    tc = pos1.shape[-1]
    smem = lambda: pl.BlockSpec((1, 1, tc), lambda i: (i, 0, 0), memory_space=pltpu.SMEM)
    return pl.pallas_call(
        functools.partial(_combine_kernel, tc=tc),
        grid=(n // tc,),
        in_specs=[smem(), smem(), pl.BlockSpec((tc, d), lambda i: (i, 0)),
                  pl.BlockSpec((tc, LANES), lambda i: (i, 0)), pl.BlockSpec(memory_space=pl.ANY)],
        out_specs=pl.BlockSpec((tc, d), lambda i: (i, 0)),
        out_shape=jax.ShapeDtypeStruct((n, d), F32),
        scratch_shapes=[pltpu.VMEM((TOP_K, tc, d), F32), pltpu.SemaphoreType.DMA(())],
        compiler_params=_cparams(("arbitrary",)),
        name="moe_combine",
    )(pos1, pos2, x, route, ys)


def _moe(x, mix, g, router, wg, wu, wd):
    n, d = x.shape
    ne = wg.shape[0]
    tb = _tile(n, 512)
    tr = FF_ROWS if n >= 8192 else min(256, tb)
    x, route, cnt = _router(x, mix, g, router)
    counts = cnt[0, :ne].astype(jnp.int32)
    padded = ((counts + tr - 1) // tr) * tr
    ends = jnp.cumsum(padded)
    starts = ends - padded
    e1 = route[:, ROUTE_E1].astype(jnp.int32)
    e2 = route[:, ROUTE_E2].astype(jnp.int32)
    pos1 = (starts[e1] + route[:, ROUTE_R1].astype(jnp.int32)).reshape(n // tb, 1, tb)
    pos2 = (starts[e2] + route[:, ROUTE_R2].astype(jnp.int32)).reshape(n // tb, 1, tb)
    max_tiles = -(-(TOP_K * n + ne * (tr - 1)) // tr)
    n_valid = (ends[-1] // tr).reshape(1)
    tile_start = jnp.minimum(jnp.arange(max_tiles, dtype=jnp.int32), n_valid[0] - 1) * tr
    tile_expert = jnp.minimum(jnp.searchsorted(ends, tile_start, side="right"), ne - 1).astype(jnp.int32)
    n_rows = max_tiles * tr
    pad_lo = jnp.concatenate([starts + counts, ends[-1:]]).astype(jnp.int32)
    pad_hi = jnp.concatenate([ends, jnp.full((1,), n_rows, ends.dtype)]).astype(jnp.int32)
    xs = _dispatch(x, pos1, pos2, n_rows, pad_lo, pad_hi, tr)
    ys = _experts(xs, g, wg, wu, wd, tile_expert, n_valid.astype(jnp.int32), tr)
    return _combine(x, route, ys, pos1, pos2)


def kernel(x_prompt, x_sample, cache_k, cache_v, state_hgrn, norm_mix, w_in, sb_q_gain, sb_k_gain,
           hg_lower_bounds, sb_out_gain, hg_out_gain, w_out, norm_ffn, ffn_w_gate, ffn_w_up, ffn_w_down,
           moe_router, moe_w_gate, moe_w_up, moe_w_down):
    depth = w_in.shape[0]
    bp, tp, d = x_prompt.shape
    bs, ts, _ = x_sample.shape
    past = cache_k.shape[2]
    sb_heads, hd = cache_k.shape[3], cache_k.shape[4]
    sw = sb_heads * hd
    dk = hg_out_gain.shape[1]
    hw = hg_lower_bounds.shape[1]
    hg_heads = hw // dk
    sb_scale = hd ** -0.5

    lbs = jnp.cumsum(jax.nn.softmax(hg_lower_bounds.astype(F32), axis=0), axis=0)
    lbs = lbs - lbs[0:1]

    xp = x_prompt.reshape(bp * tp, d)
    xs = x_sample.reshape(bs * ts, d)
    zeros_state = jnp.zeros((bp, hg_heads, dk, dk), F32)
    outs = {k: [] for k in ("sp", "ss")}
    cache_kt = jnp.transpose(cache_k, (0, 1, 3, 4, 2)).reshape(depth, bs, sw, past)
    cache_vt = jnp.transpose(cache_v, (0, 1, 3, 4, 2)).reshape(depth, bs, sw, past)

    def mixer(x, b, t, l, kv_prev, *, cached, past_len, s0, bq):
        qg = (jnp.tile(sb_q_gain[l], sb_heads) * sb_scale)[None, :]
        kg = jnp.tile(sb_k_gain[l], sb_heads)[None, :]
        q, kf, kb, vf, vb, qh, kh, lf, ih, gh = _inproj(
            x, norm_mix[l][None, :], w_in[l].astype(BF16), qg, kg,
            jnp.log(lbs[l])[None, :], jnp.log1p(-lbs[l])[None, :], (1.0 - lbs[l])[None, :], kv_prev,
            sw=sw, hw=hw, hd=hd, seq_len=t)
        r3 = lambda a: a.reshape(b, t, a.shape[-1])
        kb3, vb3 = r3(kb), r3(vb)
        ya = _attention(r3(q), kb3, vb3, cache_kt if cached else kb3, cache_vt if cached else vb3,
                        jnp.tile(sb_out_gain[l], LANES // hd)[None, :],
                        bq=bq, past_len=past_len, hd=hd, past_layer=l if cached else None)
        yb, s_fin = _hgrn(r3(qh), r3(kh), r3(lf), r3(ih), r3(gh), s0, hg_out_gain[l][None, :], dk=dk)
        wo = w_out[l].astype(BF16)
        return (ya.reshape(b * t, sw), yb.reshape(b * t, hw), wo[:sw], wo[sw:]), (kf, vf), s_fin

    def channel(x, mix, l):
        j = l // 2
        if l % 2 == 0:
            return _ffn(x, mix, norm_ffn[l][None, :], ffn_w_gate[j].astype(BF16), ffn_w_up[j].astype(BF16),
                        ffn_w_down[j].astype(BF16))
        return _moe(x, mix, norm_ffn[l][None, :], moe_router[j], moe_w_gate[j].astype(BF16),
                    moe_w_up[j].astype(BF16), moe_w_down[j].astype(BF16))

    bq_p = _tile(tp, 512)
    kv_p = kv_s = None
    for l in range(depth):
        mix_p, kv_p, s_p = mixer(xp, bp, tp, l, kv_p, cached=False, past_len=None, s0=zeros_state, bq=bq_p)
        outs["sp"].append(s_p)
        mix_s, kv_s, s_s = mixer(xs, bs, ts, l, kv_s, cached=True, past_len=past,
                                 s0=state_hgrn[l].astype(F32), bq=ts)
        outs["ss"].append(s_s)
        xp = channel(xp, mix_p, l)
        xs = channel(xs, mix_s, l)

    heads5 = lambda a, b, t: jnp.transpose(a.reshape(depth, b, sb_heads, hd, t), (0, 1, 4, 2, 3))
    return (xp.reshape(bp, tp, d), xs.reshape(bs, ts, d), heads5(kv_p[0], bp, tp), heads5(kv_p[1], bp, tp),
            jnp.stack(outs["sp"]), heads5(kv_s[0], bs, ts), heads5(kv_s[1], bs, ts), jnp.stack(outs["ss"]))
```
